```python
import math
import numpy as np
import jax
import jax.numpy as jnp
from jax import lax

D_MODEL = 2048
BATCH = 8
SEQ = 2048
DEPTH = 2
DEC_BATCH = 2
DEC_SEQ = 4096
PAST_LEN = 128

BRANCH_W = D_MODEL // 2
N_BRANCH = 4
ROPE_THETA = 500000.0
EPS = 1e-6
Q_BLOCK = 128

MLA_HEADS = 8
MLA_NOPE = 128
MLA_ROPE = 64
MLA_V = BRANCH_W // MLA_HEADS
MLA_Q_LORA = 512
MLA_KV_LORA = 256

DIFF_HEADS = 4
DIFF_HD = BRANCH_W // (2 * DIFF_HEADS)
DIFF_ROT = DIFF_HD // 4

SSD_P = 64
SSD_HEADS = BRANCH_W // SSD_P
SSD_N = 128
SSD_G = 2
SSD_CONV = 4
SSD_CHUNK = 128
SSD_CONV_DIM = BRANCH_W + 2 * SSD_G * SSD_N

POOL_WINDOWS = (2, 4, 8, 16)
POOL_GROUP = BRANCH_W // 4

IN_SIZES = (MLA_Q_LORA, MLA_KV_LORA, MLA_ROPE, BRANCH_W,
            2 * BRANCH_W // 2 * 1, BRANCH_W, BRANCH_W, BRANCH_W,
            BRANCH_W, SSD_CONV_DIM, 2 * SSD_HEADS,
            BRANCH_W, BRANCH_W,
            N_BRANCH * D_MODEL)
IN_DIM = sum(IN_SIZES)
IN_SPLITS = tuple(int(v) for v in np.cumsum(IN_SIZES)[:-1])

kernel_name = 'hybrid_bidir_encoder'


def _rmsnorm(x, w):
    xf = x.astype(jnp.float32)
    y = xf * lax.rsqrt(jnp.mean(xf * xf, axis=-1, keepdims=True) + EPS)
    return (y * w.astype(jnp.float32)).astype(x.dtype)


def _rope(x, rot_dim):
    s = x.shape[1]
    half = rot_dim // 2
    inv_freq = jnp.power(ROPE_THETA, -jnp.arange(half, dtype=jnp.float32) * 2.0 / rot_dim)
    ang = jnp.arange(s, dtype=jnp.float32)[:, None] * inv_freq[None, :]
    cos = jnp.cos(ang)[None, :, None, :]
    sin = jnp.sin(ang)[None, :, None, :]
    xf = x.astype(jnp.float32)
    x1 = xf[..., :half]
    x2 = xf[..., half:rot_dim]
    out = jnp.concatenate([x1 * cos - x2 * sin, x1 * sin + x2 * cos, xf[..., rot_dim:]], axis=-1)
    return out.astype(x.dtype)


def _to_blocks(t):
    b, s, h, d = t.shape
    return t.reshape(b, s // Q_BLOCK, Q_BLOCK, h, d).transpose(1, 0, 3, 2, 4)


def _from_blocks(o):
    nb, b, h, qb, d = o.shape
    return o.transpose(1, 0, 3, 2, 4).reshape(b, nb * qb, h, d)


def _mla(cq, ckv, kr, gate, q_norm, w_uq, kv_norm, w_ukv):
    b, s, _ = cq.shape
    q = (_rmsnorm(cq, q_norm) @ w_uq).reshape(b, s, MLA_HEADS, MLA_NOPE + MLA_ROPE)
    q_nope = q[..., :MLA_NOPE]
    q_pe = _rope(q[..., MLA_NOPE:], MLA_ROPE)
    kv = (_rmsnorm(ckv, kv_norm) @ w_ukv).reshape(b, s, MLA_HEADS, MLA_NOPE + MLA_V)
    k_nope = kv[..., :MLA_NOPE]
    v = kv[..., MLA_NOPE:]
    k_pe = _rope(kr[:, :, None, :], MLA_ROPE)[:, :, 0]
    scale = (MLA_NOPE + MLA_ROPE) ** -0.5

    def attend(blk):
        qn, qp = blk
        sc = (jnp.einsum('bhqd,bkhd->bhqk', qn, k_nope)
              + jnp.einsum('bhqd,bkd->bhqk', qp, k_pe))
        p = jax.nn.softmax(sc.astype(jnp.float32) * scale, axis=-1)
        return jnp.einsum('bhqk,bkhd->bhqd', p.astype(v.dtype), v)

    o = _from_blocks(lax.map(attend, (_to_blocks(q_nope), _to_blocks(q_pe))))
    return o.reshape(b, s, BRANCH_W) * jax.nn.silu(gate)


def _diff_attn(dq, dk, dv, gate, lam_params, subln, lambda_init):
    b, s, _ = dq.shape
    q = _rope(dq.reshape(b, s, 2 * DIFF_HEADS, DIFF_HD), DIFF_ROT).reshape(b, s, DIFF_HEADS, 2, DIFF_HD)
    k = _rope(dk.reshape(b, s, 2 * DIFF_HEADS, DIFF_HD), DIFF_ROT).reshape(b, s, DIFF_HEADS, 2, DIFF_HD)
    v = dv.reshape(b, s, DIFF_HEADS, 2 * DIFF_HD)
    k1 = k[..., 0, :]
    k2 = k[..., 1, :]
    lp = lam_params.astype(jnp.float32)
    lam = jnp.exp(jnp.sum(lp[0] * lp[1])) - jnp.exp(jnp.sum(lp[2] * lp[3])) + lambda_init
    scale = DIFF_HD ** -0.5

    def attend(blk):
        q1, q2 = blk
        p1 = jax.nn.softmax(jnp.einsum('bhqd,bkhd->bhqk', q1, k1).astype(jnp.float32) * scale, axis=-1)
        p2 = jax.nn.softmax(jnp.einsum('bhqd,bkhd->bhqk', q2, k2).astype(jnp.float32) * scale, axis=-1)
        p = p1 - lam * p2
        return jnp.einsum('bhqk,bkhd->bhqd', p.astype(v.dtype), v)

    o = _from_blocks(lax.map(attend, (_to_blocks(q[..., 0, :]), _to_blocks(q[..., 1, :]))))
    o = _rmsnorm(o, subln) * (1.0 - lambda_init)
    return o.reshape(b, s, BRANCH_W) * jax.nn.silu(gate)


def _ssd_scan(x, dt, a, b_mat, c_mat):
    bsz, s, h, p = x.shape
    g, n = b_mat.shape[2], b_mat.shape[3]
    r = h // g
    nc = s // SSD_CHUNK
    xd = (x.astype(jnp.float32) * dt[..., None]).reshape(bsz, nc, SSD_CHUNK, g, r, p)
    da = (dt * a).reshape(bsz, nc, SSD_CHUNK, g, r).transpose(0, 3, 4, 1, 2)
    bc = b_mat.astype(jnp.float32).reshape(bsz, nc, SSD_CHUNK, g, n)
    cc = c_mat.astype(jnp.float32).reshape(bsz, nc, SSD_CHUNK, g, n)
    a_cs = jnp.cumsum(da, axis=-1)
    mask = jnp.tril(jnp.ones((SSD_CHUNK, SSD_CHUNK), dtype=bool))
    seg = a_cs[..., :, None] - a_cs[..., None, :]
    l_mat = jnp.exp(jnp.where(mask, seg, -jnp.inf))
    cb = jnp.einsum('bclgn,bcsgn->bcgls', cc, bc)
    y_diag = jnp.einsum('bcgls,bgrcls,bcsgrp->bclgrp', cb, l_mat, xd)
    decay_states = jnp.exp(a_cs[..., -1:] - a_cs)
    states = jnp.einsum('bclgn,bgrcl,bclgrp->bcgrpn', bc, decay_states, xd)
    chunk_decay = jnp.exp(a_cs[..., -1])

    def step(hst, inp):
        st, dec = inp
        return hst * dec[..., None, None] + st, hst

    h0 = jnp.zeros((bsz, g, r, p, n), jnp.float32)
    _, prev = lax.scan(step, h0, (jnp.moveaxis(states, 1, 0), jnp.moveaxis(chunk_decay, 3, 0)))
    prev = jnp.moveaxis(prev, 0, 1)
    y_off = jnp.einsum('bclgn,bcgrpn,bgrcl->bclgrp', cc, prev, jnp.exp(a_cs))
    return (y_diag + y_off).reshape(bsz, s, h, p)


def _ssd(z, xbc, dt_raw, conv_w, conv_b, dt_bias, a_log, d_skip, norm_w):
    b, s, _ = xbc.shape
    pad_l = SSD_CONV // 2
    pad_r = SSD_CONV - 1 - pad_l
    xbc = lax.conv_general_dilated(xbc, conv_w[:, None, :], window_strides=(1,), padding=[(pad_l, pad_r)],
                                   dimension_numbers=('NWC', 'WIO', 'NWC'),
                                   feature_group_count=SSD_CONV_DIM) + conv_b
    xbc = jax.nn.silu(xbc)
    xs = xbc[..., :BRANCH_W].reshape(b, s, SSD_HEADS, SSD_P)
    bm = xbc[..., BRANCH_W:BRANCH_W + SSD_G * SSD_N].reshape(b, s, SSD_G, SSD_N)
    cm = xbc[..., BRANCH_W + SSD_G * SSD_N:].reshape(b, s, SSD_G, SSD_N)
    dt = jax.nn.softplus(dt_raw.astype(jnp.float32).reshape(b, s, 2, SSD_HEADS) + dt_bias.astype(jnp.float32))
    a = -jnp.exp(a_log.astype(jnp.float32))
    y_f = _ssd_scan(xs, dt[:, :, 0], a[0], bm, cm)
    y_b = jnp.flip(_ssd_scan(jnp.flip(xs, 1), jnp.flip(dt[:, :, 1], 1), a[1],
                             jnp.flip(bm, 1), jnp.flip(cm, 1)), 1)
    y = y_f + y_b + xs.astype(jnp.float32) * d_skip.astype(jnp.float32)[:, None]
    y = y.reshape(b, s, BRANCH_W) * jax.nn.silu(z.astype(jnp.float32))
    y = _rmsnorm(y.reshape(b, s, SSD_G, BRANCH_W // SSD_G), norm_w.reshape(SSD_G, BRANCH_W // SSD_G))
    return y.reshape(b, s, BRANCH_W).astype(z.dtype)


def _pool(u, gate, pool_w, pool_scale):
    b, s, _ = u.shape
    uf = u.astype(jnp.float32).reshape(b, s, 4, POOL_GROUP)
    cs = jnp.concatenate([jnp.zeros((b, 1, 4, POOL_GROUP), jnp.float32), jnp.cumsum(uf, axis=1)], axis=1)
    t = np.arange(s)
    means = []
    for i, w in enumerate(POOL_WINDOWS):
        lo = w // 2
        hi = w - 1 - lo
        start = np.clip(t - lo, 0, s)
        end = np.clip(t + hi + 1, 0, s)
        cnt = jnp.asarray((end - start).astype(np.float32))
        means.append((cs[:, end, i] - cs[:, start, i]) / cnt[None, :, None])
    pooled = jnp.stack(means, axis=2) - uf
    mixed = jnp.einsum('bsgc,gcd->bsgd', pooled.astype(u.dtype), pool_w).reshape(b, s, BRANCH_W)
    return mixed * pool_scale * jax.nn.silu(gate)


def _trunk(x, norm_w, w_in, mla_q_norm, mla_w_uq, mla_kv_norm, mla_w_ukv, diff_lambda, diff_subln,
           ssd_conv_w, ssd_conv_b, ssd_dt_bias, ssd_a_log, ssd_d, ssd_norm, pool_w, pool_scale,
           w_branch, w_out, final_norm):
    b, s, _ = x.shape
    for l in range(DEPTH):
        h = _rmsnorm(x, norm_w[l])
        (cq, ckv, kr, g_mla, dq, dk, dv, g_diff, z, xbc, dt_raw, u, g_pool, mg) = jnp.split(
            h @ w_in[l], list(IN_SPLITS), axis=-1)
        lambda_init = 0.8 - 0.6 * math.exp(-0.3 * l)
        br_mla = _mla(cq, ckv, kr, g_mla, mla_q_norm[l], mla_w_uq[l], mla_kv_norm[l], mla_w_ukv[l])
        br_diff = _diff_attn(dq, dk, dv, g_diff, diff_lambda[l], diff_subln[l], lambda_init)
        br_ssd = _ssd(z, xbc, dt_raw, ssd_conv_w[l], ssd_conv_b[l], ssd_dt_bias[l], ssd_a_log[l],
                      ssd_d[l], ssd_norm[l])
        br_pool = _pool(u, g_pool, pool_w[l], pool_scale[l])
        gates = jax.nn.sigmoid(mg.reshape(b, s, N_BRANCH, D_MODEL))
        merged = gates[:, :, 0] * (br_mla @ w_branch[l, 0])
        merged = merged + gates[:, :, 1] * (br_diff @ w_branch[l, 1])
        merged = merged + gates[:, :, 2] * (br_ssd @ w_branch[l, 2])
        merged = merged + gates[:, :, 3] * (br_pool @ w_branch[l, 3])
        x = x + merged @ w_out[l]
    return _rmsnorm(x, final_norm)


def setup_inputs(seed: int = 0) -> dict:
    key = jax.random.key(seed)
    ks = jax.random.split(key, 24)
    f32 = jnp.float32
    nrm = lambda k, shape, sc: jax.random.normal(k, shape, f32) * sc
    dt0 = jnp.exp(jax.random.uniform(ks[12], (DEPTH, 2, SSD_HEADS), f32, math.log(1e-3), math.log(1e-1)))
    return {
        'x_prompt': jax.random.normal(ks[0], (BATCH, SEQ, D_MODEL), f32),
        'x_sample': jax.random.normal(ks[1], (DEC_BATCH, DEC_SEQ, D_MODEL), f32),
        'norm_w': 1.0 + nrm(ks[2], (DEPTH, D_MODEL), 0.02),
        'w_in': nrm(ks[3], (DEPTH, D_MODEL, IN_DIM), D_MODEL ** -0.5),
        'mla_q_norm': 1.0 + nrm(ks[4], (DEPTH, MLA_Q_LORA), 0.02),
        'mla_w_uq': nrm(ks[5], (DEPTH, MLA_Q_LORA, MLA_HEADS * (MLA_NOPE + MLA_ROPE)), MLA_Q_LORA ** -0.5),
        'mla_kv_norm': 1.0 + nrm(ks[6], (DEPTH, MLA_KV_LORA), 0.02),
        'mla_w_ukv': nrm(ks[7], (DEPTH, MLA_KV_LORA, MLA_HEADS * (MLA_NOPE + MLA_V)), MLA_KV_LORA ** -0.5),
        'diff_lambda': nrm(ks[8], (DEPTH, 4, DIFF_HD), 0.1),
        'diff_subln': 1.0 + nrm(ks[9], (DEPTH, 2 * DIFF_HD), 0.02),
        'ssd_conv_w': nrm(ks[10], (DEPTH, SSD_CONV, SSD_CONV_DIM), SSD_CONV ** -0.5),
        'ssd_conv_b': nrm(ks[11], (DEPTH, SSD_CONV_DIM), 0.01),
        'ssd_dt_bias': dt0 + jnp.log(-jnp.expm1(-dt0)),
        'ssd_a_log': jnp.log(jax.random.uniform(ks[13], (DEPTH, 2, SSD_HEADS), f32, 1.0, 16.0)),
        'ssd_d': 1.0 + nrm(ks[14], (DEPTH, SSD_HEADS), 0.02),
        'ssd_norm': 1.0 + nrm(ks[15], (DEPTH, BRANCH_W), 0.02),
        'pool_w': nrm(ks[16], (DEPTH, 4, POOL_GROUP, POOL_GROUP), POOL_GROUP ** -0.5),
        'pool_scale': 1.0 + nrm(ks[17], (DEPTH, BRANCH_W), 0.02),
        'w_branch': nrm(ks[18], (DEPTH, N_BRANCH, BRANCH_W, D_MODEL), BRANCH_W ** -0.5),
        'w_out': nrm(ks[19], (DEPTH, D_MODEL, D_MODEL), D_MODEL ** -0.5),
        'final_norm': 1.0 + nrm(ks[20], (D_MODEL,), 0.02),
    }


def reference(x_prompt, x_sample, norm_w, w_in, mla_q_norm, mla_w_uq, mla_kv_norm, mla_w_ukv,
              diff_lambda, diff_subln, ssd_conv_w, ssd_conv_b, ssd_dt_bias, ssd_a_log, ssd_d, ssd_norm,
              pool_w, pool_scale, w_branch, w_out, final_norm):
    weights = (norm_w, w_in, mla_q_norm, mla_w_uq, mla_kv_norm, mla_w_ukv, diff_lambda, diff_subln,
               ssd_conv_w, ssd_conv_b, ssd_dt_bias, ssd_a_log, ssd_d, ssd_norm, pool_w, pool_scale,
               w_branch, w_out, final_norm)
    y_prompt = _trunk(x_prompt, *weights)
    y_sample = _trunk(x_sample, *weights)
    return (y_prompt, y_sample)
```

```python
import functools
import math

import numpy as np
import jax
import jax.numpy as jnp
from jax import lax
from jax.experimental import pallas as pl
from jax.experimental.pallas import tpu as pltpu

F32 = jnp.float32
BF16 = jnp.bfloat16
HIGHEST = lax.Precision.HIGHEST

D_MODEL = 2048
DEPTH = 2
BRANCH_W = D_MODEL // 2
N_BRANCH = 4
ROPE_THETA = 500000.0
EPS = 1e-6
MLA_HEADS = 8
MLA_NOPE = 128
MLA_ROPE = 64
MLA_V = BRANCH_W // MLA_HEADS
MLA_Q_LORA = 512
MLA_KV_LORA = 256
DIFF_HEADS = 4
DIFF_HD = BRANCH_W // (2 * DIFF_HEADS)
DIFF_ROT = DIFF_HD // 4
SSD_P = 64
SSD_HEADS = BRANCH_W // SSD_P
SSD_N = 128
SSD_G = 2
SSD_CONV = 4
SSD_CHUNK = 128
SSD_CONV_DIM = BRANCH_W + 2 * SSD_G * SSD_N
POOL_WINDOWS = (2, 4, 8, 16)
POOL_GROUP = BRANCH_W // 4
IN_SIZES = (MLA_Q_LORA, MLA_KV_LORA, MLA_ROPE, BRANCH_W, BRANCH_W, BRANCH_W, BRANCH_W, BRANCH_W,
            BRANCH_W, SSD_CONV_DIM, 2 * SSD_HEADS, BRANCH_W, BRANCH_W, N_BRANCH * D_MODEL)
IN_OFFS = tuple(int(v) for v in np.cumsum((0,) + IN_SIZES))

LANE = 128
HALO = 16
VMEM_LIMIT = 56 * 1024 * 1024

OFF_GMLA = 0
OFF_DQ = 1 * BRANCH_W
OFF_DK = 2 * BRANCH_W
OFF_DV = 3 * BRANCH_W
OFF_GDIFF = 4 * BRANCH_W
OFF_Z = 5 * BRANCH_W
OFF_XS = 6 * BRANCH_W
OFF_U = 7 * BRANCH_W
OFF_GPOOL = 8 * BRANCH_W
OFF_MG = 9 * BRANCH_W
OFF_CQ = OFF_MG + N_BRANCH * D_MODEL
OFF_BC = OFF_CQ + MLA_Q_LORA
OFF_CKV = OFF_BC + 2 * SSD_G * SSD_N
OFF_KR = OFF_CKV + MLA_KV_LORA
OFF_KRP = OFF_KR + LANE
NP_MAIN = OFF_KRP + LANE
DT_W = 2 * LANE

MLA_QK = 2 * LANE
LOG2E = math.log2(math.e)


def _cparams(sem, vmem=VMEM_LIMIT):
    return pltpu.CompilerParams(dimension_semantics=sem, vmem_limit_bytes=vmem)


def _tile(n, pref):
    t = min(n, pref)
    while n % t:
        t //= 2
    return t


def _silu(x):
    return x * jax.nn.sigmoid(x)


def _softplus(x):
    return jnp.maximum(x, 0.0) + jnp.log1p(jnp.exp(-jnp.abs(x)))


def _inproj_kernel(x_ref, nw_ref, w_ref, wdt_ref, o_ref, dt_ref, h_ref):
    @pl.when(pl.program_id(1) == 0)
    def _():
        x = x_ref[...]
        y = x * lax.rsqrt(jnp.mean(x * x, axis=-1, keepdims=True) + EPS)
        hb = (y * nw_ref[...]).astype(BF16)
        h_ref[...] = hb
        dt_ref[...] = jnp.dot(hb, wdt_ref[...], preferred_element_type=F32)

    o_ref[...] = jnp.dot(h_ref[...], w_ref[...], preferred_element_type=F32).astype(BF16)


def _inproj(x2, norm_w, w_main, w_dt):
    t = x2.shape[0]
    tm = _tile(t, 512)
    tn = 1024 if NP_MAIN % 1024 == 0 else 512
    return pl.pallas_call(
        _inproj_kernel,
        grid=(t // tm, NP_MAIN // tn),
        in_specs=[
            pl.BlockSpec((tm, D_MODEL), lambda i, j: (i, 0)),
            pl.BlockSpec((1, D_MODEL), lambda i, j: (0, 0)),
            pl.BlockSpec((D_MODEL, tn), lambda i, j: (0, j)),
            pl.BlockSpec((D_MODEL, DT_W), lambda i, j: (0, 0)),
        ],
        out_specs=[
            pl.BlockSpec((tm, tn), lambda i, j: (i, j)),
            pl.BlockSpec((tm, DT_W), lambda i, j: (i, 0)),
        ],
        out_shape=[jax.ShapeDtypeStruct((t, NP_MAIN), BF16), jax.ShapeDtypeStruct((t, DT_W), F32)],
        scratch_shapes=[pltpu.VMEM((tm, D_MODEL), BF16)],
        compiler_params=_cparams(("parallel", "arbitrary")),
        name="inproj",
    )(x2, norm_w.reshape(1, D_MODEL), w_main, w_dt)


def _mla_prep_kernel(cq_ref, ckv_ref, kr_ref, krp_ref, cos_ref, sin_ref, qn_ref, kvn_ref,
                     wq_ref, wqp_ref, wk_ref, wv_ref, q_ref, k_ref, v_ref):
    cos = cos_ref[...]
    sin = sin_ref[...]
    cq = cq_ref[...].astype(F32)
    ncq = (cq * lax.rsqrt(jnp.mean(cq * cq, axis=-1, keepdims=True) + EPS) * qn_ref[...]).astype(BF16)
    qm = jnp.dot(ncq, wq_ref[...], preferred_element_type=F32)
    qp = jnp.dot(ncq, wqp_ref[...], preferred_element_type=F32)
    ckv = ckv_ref[...].astype(F32)
    nkv = (ckv * lax.rsqrt(jnp.mean(ckv * ckv, axis=-1, keepdims=True) + EPS) * kvn_ref[...]).astype(BF16)
    kn = jnp.dot(nkv, wk_ref[...], preferred_element_type=F32)
    v_ref[...] = jnp.dot(nkv, wv_ref[...], preferred_element_type=F32).astype(BF16)
    kpe = (kr_ref[...].astype(F32) * cos + krp_ref[...].astype(F32) * sin).astype(BF16)
    qscale = (MLA_NOPE + MLA_ROPE) ** -0.5 * LOG2E
    for h in range(MLA_HEADS):
        lo = h * MLA_QK
        q_ref[:, lo:lo + LANE] = (qm[:, lo:lo + LANE] * qscale).astype(BF16)
        qr = qm[:, lo + LANE:lo + 2 * LANE] * cos + qp[:, h * LANE:(h + 1) * LANE] * sin
        q_ref[:, lo + LANE:lo + 2 * LANE] = (qr * qscale).astype(BF16)
        k_ref[:, lo:lo + LANE] = kn[:, h * LANE:(h + 1) * LANE].astype(BF16)
        k_ref[:, lo + LANE:lo + 2 * LANE] = kpe


def _mla_prep(proj, s, cos_t, sin_t, q_norm, kv_norm, wq, wqp, wk, wv):
    t = proj.shape[0]
    tm = _tile(s, 512)
    ns = s // tm
    hq = MLA_HEADS * MLA_QK
    const = lambda shape: pl.BlockSpec(shape, lambda i: (0, 0))
    return pl.pallas_call(
        _mla_prep_kernel,
        grid=(t // tm,),
        in_specs=[
            pl.BlockSpec((tm, MLA_Q_LORA), lambda i: (i, OFF_CQ // MLA_Q_LORA)),
            pl.BlockSpec((tm, MLA_KV_LORA), lambda i: (i, OFF_CKV // MLA_KV_LORA)),
            pl.BlockSpec((tm, LANE), lambda i: (i, OFF_KR // LANE)),
            pl.BlockSpec((tm, LANE), lambda i: (i, OFF_KRP // LANE)),
            pl.BlockSpec((tm, LANE), lambda i: (i % ns, 0)),
            pl.BlockSpec((tm, LANE), lambda i: (i % ns, 0)),
            const((1, MLA_Q_LORA)),
            const((1, MLA_KV_LORA)),
            const((MLA_Q_LORA, hq)),
            const((MLA_Q_LORA, MLA_HEADS * LANE)),
            const((MLA_KV_LORA, MLA_HEADS * MLA_NOPE)),
            const((MLA_KV_LORA, MLA_HEADS * MLA_V)),
        ],
        out_specs=[
            pl.BlockSpec((tm, hq), lambda i: (i, 0)),
            pl.BlockSpec((tm, hq), lambda i: (i, 0)),
            pl.BlockSpec((tm, MLA_HEADS * MLA_V), lambda i: (i, 0)),
        ],
        out_shape=[jax.ShapeDtypeStruct((t, hq), BF16), jax.ShapeDtypeStruct((t, hq), BF16),
                   jax.ShapeDtypeStruct((t, MLA_HEADS * MLA_V), BF16)],
        compiler_params=_cparams(("parallel",)),
        name="mla_prep",
    )(proj, proj, proj, proj, cos_t, sin_t, q_norm.reshape(1, -1), kv_norm.reshape(1, -1), wq, wqp, wk, wv)


def _softmax_pv(q, k_ref, v_ref, tk):
    tq = q.shape[0]
    s_len = k_ref.shape[0]
    dv = v_ref.shape[1]

    def step(c, carry):
        m, l, acc = carry
        start = pl.multiple_of(c * tk, tk)
        k = k_ref[pl.ds(start, tk), :]
        sc = lax.dot_general(q, k, (((1,), (1,)), ((), ())), preferred_element_type=F32)
        m_new = jnp.maximum(m, jnp.max(sc, axis=-1, keepdims=True))
        alpha = jnp.exp2(m - m_new)
        p = jnp.exp2(sc - m_new)
        l = alpha * l + jnp.sum(p, axis=-1, keepdims=True)
        pv = jnp.dot(p.astype(BF16), v_ref[pl.ds(start, tk), :], preferred_element_type=F32)
        return m_new, l, alpha * acc + pv

    init = (jnp.full((tq, 1), -jnp.inf, F32), jnp.zeros((tq, 1), F32), jnp.zeros((tq, dv), F32))
    _, l, acc = lax.fori_loop(0, s_len // tk, step, init)
    return acc, l


def _mla_attn_kernel(q_ref, k_ref, v_ref, g_ref, o_ref, *, tk):
    acc, l = _softmax_pv(q_ref[...], k_ref, v_ref, tk)
    o_ref[...] = ((acc / l) * _silu(g_ref[...].astype(F32))).astype(BF16)


def _mla_attn(q, k, v, proj3):
    b, s, _ = q.shape
    tq = _tile(s, 256)
    tk = _tile(s, 512)
    return pl.pallas_call(
        functools.partial(_mla_attn_kernel, tk=tk),
        grid=(b, MLA_HEADS, s // tq),
        in_specs=[
            pl.BlockSpec((None, tq, MLA_QK), lambda bi, h, qi: (bi, qi, h)),
            pl.BlockSpec((None, s, MLA_QK), lambda bi, h, qi: (bi, 0, h)),
            pl.BlockSpec((None, s, MLA_V), lambda bi, h, qi: (bi, 0, h)),
            pl.BlockSpec((None, tq, MLA_V), lambda bi, h, qi: (bi, qi, OFF_GMLA // MLA_V + h)),
        ],
        out_specs=pl.BlockSpec((None, tq, MLA_V), lambda bi, h, qi: (bi, qi, h)),
        out_shape=jax.ShapeDtypeStruct((b, s, BRANCH_W), BF16),
        compiler_params=_cparams(("parallel", "parallel", "parallel")),
        name="mla_attn",
    )(q, k, v, proj3)


def _diff_prep_kernel(dq_ref, dk_ref, cos_ref, sa_ref, sb_ref, q_ref, k_ref):
    cos = cos_ref[...]
    sa = sa_ref[...]
    sb = sb_ref[...]
    half = DIFF_ROT // 2
    qscale = DIFF_HD ** -0.5 * LOG2E
    for h in range(2 * DIFF_HEADS):
        sl = slice(h * DIFF_HD, (h + 1) * DIFF_HD)
        xq = dq_ref[:, sl].astype(F32)
        xk = dk_ref[:, sl].astype(F32)
        rq = xq * cos + pltpu.roll(xq, half, 1) * sa + pltpu.roll(xq, DIFF_HD - half, 1) * sb
        rk = xk * cos + pltpu.roll(xk, half, 1) * sa + pltpu.roll(xk, DIFF_HD - half, 1) * sb
        q_ref[:, sl] = (rq * qscale).astype(BF16)
        k_ref[:, sl] = rk.astype(BF16)


def _diff_prep(proj, s, cos_t, sa_t, sb_t):
    t = proj.shape[0]
    tm = _tile(s, 512)
    ns = s // tm
    tab = pl.BlockSpec((tm, LANE), lambda i: (i % ns, 0))
    return pl.pallas_call(
        _diff_prep_kernel,
        grid=(t // tm,),
        in_specs=[
            pl.BlockSpec((tm, BRANCH_W), lambda i: (i, OFF_DQ // BRANCH_W)),
            pl.BlockSpec((tm, BRANCH_W), lambda i: (i, OFF_DK // BRANCH_W)),
            tab, tab, tab,
        ],
        out_specs=[pl.BlockSpec((tm, BRANCH_W), lambda i: (i, 0))] * 2,
        out_shape=[jax.ShapeDtypeStruct((t, BRANCH_W), BF16)] * 2,
        compiler_params=_cparams(("parallel",)),
        name="diff_prep",
    )(proj, proj, cos_t, sa_t, sb_t)


def _diff_attn_kernel(q1_ref, q2_ref, k1_ref, k2_ref, v_ref, g_ref, lam_ref, sub_ref, o_ref, *,
                      tk, lambda_init):
    acc1, l1 = _softmax_pv(q1_ref[...], k1_ref, v_ref, tk)
    acc2, l2 = _softmax_pv(q2_ref[...], k2_ref, v_ref, tk)
    lp = lam_ref[...]
    lam = (jnp.exp(jnp.sum(lp[0:1] * lp[1:2], axis=-1, keepdims=True))
           - jnp.exp(jnp.sum(lp[2:3] * lp[3:4], axis=-1, keepdims=True)) + lambda_init)
    o = acc1 / l1 - lam * (acc2 / l2)
    o = o * lax.rsqrt(jnp.mean(o * o, axis=-1, keepdims=True) + EPS) * sub_ref[...]
    o = o * (1.0 - lambda_init)
    o_ref[...] = (o * _silu(g_ref[...].astype(F32))).astype(BF16)


def _diff_attn(q, k, proj3, lam_params, subln, lambda_init):
    b, s, _ = q.shape
    tq = _tile(s, 256)
    tk = _tile(s, 512)
    dv = 2 * DIFF_HD
    return pl.pallas_call(
        functools.partial(_diff_attn_kernel, tk=tk, lambda_init=lambda_init),
        grid=(b, DIFF_HEADS, s // tq),
        in_specs=[
            pl.BlockSpec((None, tq, DIFF_HD), lambda bi, h, qi: (bi, qi, 2 * h)),
            pl.BlockSpec((None, tq, DIFF_HD), lambda bi, h, qi: (bi, qi, 2 * h + 1)),
            pl.BlockSpec((None, s, DIFF_HD), lambda bi, h, qi: (bi, 0, 2 * h)),
            pl.BlockSpec((None, s, DIFF_HD), lambda bi, h, qi: (bi, 0, 2 * h + 1)),
            pl.BlockSpec((None, s, dv), lambda bi, h, qi: (bi, 0, OFF_DV // dv + h)),
            pl.BlockSpec((None, tq, dv), lambda bi, h, qi: (bi, qi, OFF_GDIFF // dv + h)),
            pl.BlockSpec((4, DIFF_HD), lambda bi, h, qi: (0, 0)),
            pl.BlockSpec((1, dv), lambda bi, h, qi: (0, 0)),
        ],
        out_specs=pl.BlockSpec((None, tq, dv), lambda bi, h, qi: (bi, qi, h)),
        out_shape=jax.ShapeDtypeStruct((b, s, BRANCH_W), BF16),
        compiler_params=_cparams(("parallel", "parallel", "parallel")),
        name="diff_attn",
    )(q, q, k, k, proj3, proj3, lam_params, subln.reshape(1, dv))


def _ssd_conv_kernel(xs_ref, xsp_ref, xsn_ref, bc_ref, bcp_ref, bcn_ref, w_ref, b_ref,
                     oxs_ref, obc_ref, ext_ref, *, tm):
    i = pl.program_id(1)
    has_prev = i > 0
    has_next = i < pl.num_programs(1) - 1
    pad_l = SSD_CONV // 2

    def conv(cur_ref, prev_ref, next_ref, out_ref, c0, width):
        ext_ref[0:HALO, 0:width] = jnp.where(has_prev, prev_ref[...].astype(F32), 0.0)
        ext_ref[HALO:HALO + tm, 0:width] = cur_ref[...].astype(F32)
        ext_ref[HALO + tm:, 0:width] = jnp.where(has_next, next_ref[...].astype(F32), 0.0)
        acc = b_ref[:, c0:c0 + width]
        for j in range(SSD_CONV):
            off = HALO - pad_l + j
            acc = acc + ext_ref[off:off + tm, 0:width] * w_ref[j:j + 1, c0:c0 + width]
        out_ref[...] = _silu(acc).astype(BF16)

    conv(xs_ref, xsp_ref, xsn_ref, oxs_ref, 0, BRANCH_W)
    conv(bc_ref, bcp_ref, bcn_ref, obc_ref, BRANCH_W, SSD_CONV_DIM - BRANCH_W)


def _halo_specs(tm, s, width, col):
    r = tm // HALO
    last = s // HALO - 1
    cur = pl.BlockSpec((None, tm, width), lambda bi, i: (bi, i, col))
    prev = pl.BlockSpec((None, HALO, width), lambda bi, i: (bi, jnp.maximum(i * r - 1, 0), col))
    nxt = pl.BlockSpec((None, HALO, width), lambda bi, i: (bi, jnp.minimum((i + 1) * r, last), col))
    return [cur, prev, nxt]


def _ssd_conv(proj3, conv_w, conv_b):
    b, s, _ = proj3.shape
    tm = _tile(s, 512)
    wbc = SSD_CONV_DIM - BRANCH_W
    return pl.pallas_call(
        functools.partial(_ssd_conv_kernel, tm=tm),
        grid=(b, s // tm),
        in_specs=(_halo_specs(tm, s, BRANCH_W, OFF_XS // BRANCH_W) + _halo_specs(tm, s, wbc, OFF_BC // wbc)
                  + [pl.BlockSpec((SSD_CONV, SSD_CONV_DIM), lambda bi, i: (0, 0)),
                     pl.BlockSpec((1, SSD_CONV_DIM), lambda bi, i: (0, 0))]),
        out_specs=[pl.BlockSpec((None, tm, BRANCH_W), lambda bi, i: (bi, i, 0)),
                   pl.BlockSpec((None, tm, wbc), lambda bi, i: (bi, i, 0))],
        out_shape=[jax.ShapeDtypeStruct((b, s, BRANCH_W), BF16), jax.ShapeDtypeStruct((b, s, wbc), BF16)],
        scratch_shapes=[pltpu.VMEM((tm + 2 * HALO, BRANCH_W), F32)],
        compiler_params=_cparams(("parallel", "parallel")),
        name="ssd_conv",
    )(proj3, proj3, proj3, proj3, proj3, proj3, conv_w, conv_b.reshape(1, -1))


def _ssd_chunk(xs_ref, bc_ref, dt_ref, bias_ref, alog_ref, st_ref, reverse):
    ln = SSD_CHUNK
    gw = BRANCH_W // SSD_G
    xs = xs_ref[...].astype(F32)
    bc = bc_ref[...]
    dt = _softplus(dt_ref[...] + bias_ref[...])
    da = dt * (-jnp.exp(alog_ref[...]))
    ri = lax.broadcasted_iota(jnp.int32, (ln, ln), 0)
    ci = lax.broadcasted_iota(jnp.int32, (ln, ln), 1)
    tri = (ci >= ri) if reverse else (ri >= ci)
    cs = jnp.dot(tri.astype(F32), da, precision=HIGHEST, preferred_element_type=F32)
    cs_end = cs[0:1, :] if reverse else cs[ln - 1:ln, :]
    er = lax.broadcasted_iota(jnp.int32, (LANE, BRANCH_W), 0)
    ec = lax.broadcasted_iota(jnp.int32, (LANE, BRANCH_W), 1)
    expand = (ec // SSD_P == er).astype(F32)
    widen = lambda a: jnp.dot(a, expand, precision=HIGHEST, preferred_element_type=F32)
    dt_e = widen(dt)
    dec_out_e = widen(jnp.exp(cs))
    dec_st_e = widen(jnp.exp(cs_end - cs))
    tot_e = dec_out_e[0:1, :] if reverse else dec_out_e[ln - 1:ln, :]
    cs_t = cs.T
    xd = xs * dt_e
    xdb = xd.astype(BF16)
    xst = (xd * dec_st_e).astype(BF16)
    low = lax.broadcasted_iota(jnp.int32, (ln, LANE), 1) < SSD_P
    zero = jnp.zeros((ln, LANE), BF16)
    outs = []
    for g in range(SSD_G):
        bg = bc[:, g * SSD_N:(g + 1) * SSD_N]
        cg = bc[:, (SSD_G + g) * SSD_N:(SSD_G + g + 1) * SSD_N]
        cb = lax.dot_general(cg, bg, (((1,), (1,)), ((), ())), preferred_element_type=F32)
        st = st_ref[g]
        y_off = jnp.dot(cg, st.astype(BF16), preferred_element_type=F32) * dec_out_e[:, g * gw:(g + 1) * gw]
        bt = bg.astype(F32).T.astype(BF16)
        st_ref[g] = (st * tot_e[:, g * gw:(g + 1) * gw]
                     + jnp.dot(bt, xst[:, g * gw:(g + 1) * gw], preferred_element_type=F32))
        for kp in range(gw // LANE):
            mats = []
            for hh in range(2):
                hd = g * (SSD_HEADS // SSD_G) + 2 * kp + hh
                seg = cs[:, hd:hd + 1] - cs_t[hd:hd + 1, :]
                lm = jnp.exp(jnp.where(tri, seg, -jnp.inf))
                mats.append((cb * lm).astype(BF16))
            lhs = jnp.concatenate(mats, axis=1)
            col = g * gw + kp * LANE
            xp = xdb[:, col:col + LANE]
            rhs = jnp.concatenate([jnp.where(low, xp, zero), jnp.where(low, zero, xp)], axis=0)
            y_diag = jnp.dot(lhs, rhs, preferred_element_type=F32)
            outs.append(y_diag + y_off[:, kp * LANE:(kp + 1) * LANE])
    return jnp.concatenate(outs, axis=1), xs


def _ssd_fwd_kernel(xs_ref, bc_ref, dt_ref, bias_ref, alog_ref, y_ref, st_ref):
    @pl.when(pl.program_id(1) == 0)
    def _():
        st_ref[...] = jnp.zeros_like(st_ref)

    y, _ = _ssd_chunk(xs_ref, bc_ref, dt_ref, bias_ref, alog_ref, st_ref, reverse=False)
    y_ref[...] = y


def _ssd_bwd_kernel(xs_ref, bc_ref, dt_ref, bias_ref, alog_ref, yf_ref, z_ref, d_ref, nw_ref,
                    o_ref, st_ref):
    @pl.when(pl.program_id(1) == 0)
    def _():
        st_ref[...] = jnp.zeros_like(st_ref)

    yb, xs = _ssd_chunk(xs_ref, bc_ref, dt_ref, bias_ref, alog_ref, st_ref, reverse=True)
    y = yf_ref[...] + yb + xs * d_ref[...]
    y = y * _silu(z_ref[...].astype(F32))
    gw = BRANCH_W // SSD_G
    for g in range(SSD_G):
        yg = y[:, g * gw:(g + 1) * gw]
        yn = yg * lax.rsqrt(jnp.mean(yg * yg, axis=-1, keepdims=True) + EPS)
        o_ref[:, g * gw:(g + 1) * gw] = (yn * nw_ref[:, g * gw:(g + 1) * gw]).astype(BF16)


def _ssd_scan(xs_act, bc_act, dt3, proj3, dt_bias, a_log, d_skip, norm_w):
    b, s, _ = xs_act.shape
    nc = s // SSD_CHUNK
    wbc = SSD_CONV_DIM - BRANCH_W
    gw = BRANCH_W // SSD_G
    pad = lambda v: jnp.pad(v.astype(F32), (0, LANE - SSD_HEADS)).reshape(1, LANE)
    scratch = [pltpu.VMEM((SSD_G, SSD_N, gw), F32)]
    vec = lambda w: pl.BlockSpec((1, w), lambda bi, c: (0, 0))

    def specs(cmap, d):
        return [pl.BlockSpec((None, SSD_CHUNK, BRANCH_W), lambda bi, c: (bi, cmap(c), 0)),
                pl.BlockSpec((None, SSD_CHUNK, wbc), lambda bi, c: (bi, cmap(c), 0)),
                pl.BlockSpec((None, SSD_CHUNK, LANE), lambda bi, c: (bi, cmap(c), d)),
                vec(LANE), vec(LANE)]

    y_f = pl.pallas_call(
        _ssd_fwd_kernel,
        grid=(b, nc),
        in_specs=specs(lambda c: c, 0),
        out_specs=pl.BlockSpec((None, SSD_CHUNK, BRANCH_W), lambda bi, c: (bi, c, 0)),
        out_shape=jax.ShapeDtypeStruct((b, s, BRANCH_W), F32),
        scratch_shapes=scratch,
        compiler_params=_cparams(("parallel", "arbitrary")),
        name="ssd_fwd",
    )(xs_act, bc_act, dt3, pad(dt_bias[0]), pad(a_log[0]))

    rev = lambda c: nc - 1 - c
    return pl.pallas_call(
        _ssd_bwd_kernel,
        grid=(b, nc),
        in_specs=specs(rev, 1) + [
            pl.BlockSpec((None, SSD_CHUNK, BRANCH_W), lambda bi, c: (bi, rev(c), 0)),
            pl.BlockSpec((None, SSD_CHUNK, BRANCH_W), lambda bi, c: (bi, rev(c), OFF_Z // BRANCH_W)),
            vec(BRANCH_W), vec(BRANCH_W)],
        out_specs=pl.BlockSpec((None, SSD_CHUNK, BRANCH_W), lambda bi, c: (bi, rev(c), 0)),
        out_shape=jax.ShapeDtypeStruct((b, s, BRANCH_W), BF16),
        scratch_shapes=scratch,
        compiler_params=_cparams(("parallel", "arbitrary")),
        name="ssd_bwd",
    )(xs_act, bc_act, dt3, pad(dt_bias[1]), pad(a_log[1]), y_f, proj3,
      jnp.repeat(d_skip.astype(F32), SSD_P).reshape(1, BRANCH_W), norm_w.reshape(1, BRANCH_W))


def _pool_kernel(u_ref, up_ref, un_ref, g_ref, w_ref, sc_ref, o_ref, ext_ref, *, tm, s):
    i = pl.program_id(1)
    cur = u_ref[...].astype(F32)
    ext_ref[0:HALO, :] = jnp.where(i > 0, up_ref[...].astype(F32), 0.0)
    ext_ref[HALO:HALO + tm, :] = cur
    ext_ref[HALO + tm:, :] = jnp.where(i < pl.num_programs(1) - 1, un_ref[...].astype(F32), 0.0)
    pos = i * tm + lax.broadcasted_iota(jnp.int32, (tm, 1), 0)
    for gi, w in enumerate(POOL_WINDOWS):
        lo = w // 2
        hi = w - 1 - lo
        sl = slice(gi * POOL_GROUP, (gi + 1) * POOL_GROUP)
        acc = ext_ref[HALO - lo:HALO - lo + tm, sl]
        for d in range(-lo + 1, hi + 1):
            acc = acc + ext_ref[HALO + d:HALO + d + tm, sl]
        cnt = (jnp.minimum(pos + hi + 1, s) - jnp.maximum(pos - lo, 0)).astype(F32)
        pooled = acc / cnt - cur[:, sl]
        mixed = jnp.dot(pooled.astype(BF16), w_ref[gi], preferred_element_type=F32)
        o_ref[:, sl] = (mixed * sc_ref[:, sl] * _silu(g_ref[:, sl].astype(F32))).astype(BF16)


def _pool(proj3, pool_w, pool_scale):
    b, s, _ = proj3.shape
    tm = _tile(s, 512)
    return pl.pallas_call(
        functools.partial(_pool_kernel, tm=tm, s=s),
        grid=(b, s // tm),
        in_specs=_halo_specs(tm, s, BRANCH_W, OFF_U // BRANCH_W) + [
            pl.BlockSpec((None, tm, BRANCH_W), lambda bi, i: (bi, i, OFF_GPOOL // BRANCH_W)),
            pl.BlockSpec((len(POOL_WINDOWS), POOL_GROUP, POOL_GROUP), lambda bi, i: (0, 0, 0)),
            pl.BlockSpec((1, BRANCH_W), lambda bi, i: (0, 0))],
        out_specs=pl.BlockSpec((None, tm, BRANCH_W), lambda bi, i: (bi, i, 0)),
        out_shape=jax.ShapeDtypeStruct((b, s, BRANCH_W), BF16),
        scratch_shapes=[pltpu.VMEM((tm + 2 * HALO, BRANCH_W), F32)],
        compiler_params=_cparams(("parallel", "parallel")),
        name="pool",
    )(proj3, proj3, proj3, proj3, pool_w, pool_scale.reshape(1, BRANCH_W))


def _merge_kernel(*refs):
    brs, ws, gs, o_ref = refs[0:4], refs[4:8], refs[8:12], refs[12]
    acc = None
    for br, w, g in zip(brs, ws, gs):
        term = jax.nn.sigmoid(g[...].astype(F32)) * jnp.dot(br[...], w[...], preferred_element_type=F32)
        acc = term if acc is None else acc + term
    o_ref[...] = acc.astype(BF16)


def _merge(branches, w_branch, proj):
    t = proj.shape[0]
    tm = _tile(t, 512)
    tn = 512
    br_spec = pl.BlockSpec((tm, BRANCH_W), lambda i, j: (i, 0))
    w_specs = [pl.BlockSpec((None, BRANCH_W, tn), lambda i, j, k=k: (k, 0, j)) for k in range(N_BRANCH)]
    g_specs = [pl.BlockSpec((tm, tn), lambda i, j, k=k: (i, (OFF_MG + k * D_MODEL) // tn + j))
               for k in range(N_BRANCH)]
    return pl.pallas_call(
        _merge_kernel,
        grid=(t // tm, D_MODEL // tn),
        in_specs=[br_spec] * N_BRANCH + w_specs + g_specs,
        out_specs=pl.BlockSpec((tm, tn), lambda i, j: (i, j)),
        out_shape=jax.ShapeDtypeStruct((t, D_MODEL), BF16),
        compiler_params=_cparams(("parallel", "parallel")),
        name="merge",
    )(*branches, *([w_branch] * N_BRANCH), *([proj] * N_BRANCH))


def _out_kernel(m_ref, w_ref, x_ref, fn_ref, o_ref, *, final):
    y = x_ref[...] + jnp.dot(m_ref[...], w_ref[...], preferred_element_type=F32)
    if final:
        y = y * lax.rsqrt(jnp.mean(y * y, axis=-1, keepdims=True) + EPS) * fn_ref[...]
    o_ref[...] = y


def _out_proj(merged, w_out, x2, final_norm, final):
    t = x2.shape[0]
    tm = _tile(t, 512)
    return pl.pallas_call(
        functools.partial(_out_kernel, final=final),
        grid=(t // tm,),
        in_specs=[pl.BlockSpec((tm, D_MODEL), lambda i: (i, 0)),
                  pl.BlockSpec((D_MODEL, D_MODEL), lambda i: (0, 0)),
                  pl.BlockSpec((tm, D_MODEL), lambda i: (i, 0)),
                  pl.BlockSpec((1, D_MODEL), lambda i: (0, 0))],
        out_specs=pl.BlockSpec((tm, D_MODEL), lambda i: (i, 0)),
        out_shape=jax.ShapeDtypeStruct((t, D_MODEL), F32),
        compiler_params=_cparams(("parallel",)),
        name="out_proj",
    )(merged, w_out, x2, final_norm.reshape(1, D_MODEL))


def _pack_layer(w_in, w_uq, w_ukv):
    seg = [w_in[:, IN_OFFS[k]:IN_OFFS[k + 1]] for k in range(len(IN_SIZES))]
    cq, ckv, kr, gmla, dq, dk, dv, gdiff, z, xbc, dt, u, gpool, mg = seg
    zpad = lambda n: jnp.zeros((w_in.shape[0], n), w_in.dtype)
    half = MLA_ROPE // 2
    kr_main = jnp.concatenate([kr, zpad(LANE - MLA_ROPE)], axis=1)
    kr_part = jnp.concatenate([kr[:, half:], kr[:, :half], zpad(LANE - MLA_ROPE)], axis=1)
    w_main = jnp.concatenate([gmla, dq, dk, dv, gdiff, z, xbc[:, :BRANCH_W], u, gpool, mg, cq,
                              xbc[:, BRANCH_W:], ckv, kr_main, kr_part], axis=1).astype(BF16)
    w_dt = jnp.concatenate([dt[:, :SSD_HEADS], zpad(LANE - SSD_HEADS),
                            dt[:, SSD_HEADS:], zpad(LANE - SSD_HEADS)], axis=1).astype(BF16)
    wq3 = w_uq.reshape(MLA_Q_LORA, MLA_HEADS, MLA_NOPE + MLA_ROPE)
    zq = jnp.zeros((MLA_Q_LORA, MLA_HEADS, LANE - MLA_ROPE), w_uq.dtype)
    wq = jnp.concatenate([wq3, zq], axis=2).reshape(MLA_Q_LORA, MLA_HEADS * MLA_QK).astype(BF16)
    wqp = jnp.concatenate([wq3[:, :, MLA_NOPE + half:], wq3[:, :, MLA_NOPE:MLA_NOPE + half], zq],
                          axis=2).reshape(MLA_Q_LORA, MLA_HEADS * LANE).astype(BF16)
    wkv3 = w_ukv.reshape(MLA_KV_LORA, MLA_HEADS, MLA_NOPE + MLA_V)
    wk = wkv3[:, :, :MLA_NOPE].reshape(MLA_KV_LORA, MLA_HEADS * MLA_NOPE).astype(BF16)
    wv = wkv3[:, :, MLA_NOPE:].reshape(MLA_KV_LORA, MLA_HEADS * MLA_V).astype(BF16)
    return w_main, w_dt, wq, wqp, wk, wv


def _rope_tables(s):
    def angles(rot):
        half = rot // 2
        inv_freq = jnp.power(ROPE_THETA, -jnp.arange(half, dtype=F32) * 2.0 / rot)
        ang = jnp.arange(s, dtype=F32)[:, None] * inv_freq[None, :]
        return jnp.cos(ang), jnp.sin(ang)

    c, sn = angles(MLA_ROPE)
    z = jnp.zeros((s, LANE - MLA_ROPE), F32)
    mla_cos = jnp.concatenate([c, c, z], axis=1)
    mla_sin = jnp.concatenate([-sn, sn, z], axis=1)
    c, sn = angles(DIFF_ROT)
    zh = jnp.zeros_like(sn)
    rest = LANE - DIFF_ROT
    d_cos = jnp.concatenate([c, c, jnp.ones((s, rest), F32)], axis=1)
    d_sa = jnp.concatenate([zh, sn, jnp.zeros((s, rest), F32)], axis=1)
    d_sb = jnp.concatenate([-sn, zh, jnp.zeros((s, rest), F32)], axis=1)
    return mla_cos, mla_sin, d_cos, d_sa, d_sb


def _trunk(x, packed, norm_w, mla_q_norm, mla_kv_norm, diff_lambda, diff_subln, ssd_conv_w, ssd_conv_b,
           ssd_dt_bias, ssd_a_log, ssd_d, ssd_norm, pool_w, pool_scale, w_branch, w_out, final_norm):
    b, s, _ = x.shape
    t = b * s
    mla_cos, mla_sin, d_cos, d_sa, d_sb = _rope_tables(s)
    x2 = x.reshape(t, D_MODEL)
    for l in range(DEPTH):
        w_main, w_dt, wq, wqp, wk, wv = packed[l]
        lambda_init = 0.8 - 0.6 * math.exp(-0.3 * l)
        proj, dt = _inproj(x2, norm_w[l], w_main, w_dt)
        proj3 = proj.reshape(b, s, NP_MAIN)
        q, k, v = _mla_prep(proj, s, mla_cos, mla_sin, mla_q_norm[l], mla_kv_norm[l], wq, wqp, wk, wv)
        br_mla = _mla_attn(q.reshape(b, s, -1), k.reshape(b, s, -1), v.reshape(b, s, -1), proj3)
        dq, dk = _diff_prep(proj, s, d_cos, d_sa, d_sb)
        br_diff = _diff_attn(dq.reshape(b, s, -1), dk.reshape(b, s, -1), proj3, diff_lambda[l],
                             diff_subln[l], lambda_init)
        xs_act, bc_act = _ssd_conv(proj3, ssd_conv_w[l], ssd_conv_b[l])
        br_ssd = _ssd_scan(xs_act, bc_act, dt.reshape(b, s, DT_W), proj3, ssd_dt_bias[l], ssd_a_log[l],
                           ssd_d[l], ssd_norm[l])
        br_pool = _pool(proj3, pool_w[l].astype(BF16), pool_scale[l])
        branches = [a.reshape(t, BRANCH_W) for a in (br_mla, br_diff, br_ssd, br_pool)]
        merged = _merge(branches, w_branch[l].astype(BF16), proj)
        x2 = _out_proj(merged, w_out[l].astype(BF16), x2, final_norm, final=(l == DEPTH - 1))
    return x2.reshape(b, s, D_MODEL)


def kernel(x_prompt, x_sample, norm_w, w_in, mla_q_norm, mla_w_uq, mla_kv_norm, mla_w_ukv, diff_lambda,
           diff_subln, ssd_conv_w, ssd_conv_b, ssd_dt_bias, ssd_a_log, ssd_d, ssd_norm, pool_w, pool_scale,
           w_branch, w_out, final_norm):
    packed = [_pack_layer(w_in[l], mla_w_uq[l], mla_w_ukv[l]) for l in range(DEPTH)]
    rest = (norm_w, mla_q_norm, mla_kv_norm, diff_lambda, diff_subln, ssd_conv_w, ssd_conv_b, ssd_dt_bias,
            ssd_a_log, ssd_d, ssd_norm, pool_w, pool_scale, w_branch, w_out, final_norm)
    return (_trunk(x_prompt, packed, *rest), _trunk(x_sample, packed, *rest))
```

```python
import functools
import math

import numpy as np
import jax
import jax.numpy as jnp
from jax import lax
from jax.experimental import pallas as pl
from jax.experimental.pallas import tpu as pltpu

F32 = jnp.float32
BF16 = jnp.bfloat16
HIGHEST = lax.Precision.HIGHEST

D_MODEL = 2048
DEPTH = 2
BRANCH_W = D_MODEL // 2
N_BRANCH = 4
ROPE_THETA = 500000.0
EPS = 1e-6
MLA_HEADS = 8
MLA_NOPE = 128
MLA_ROPE = 64
MLA_V = BRANCH_W // MLA_HEADS
MLA_Q_LORA = 512
MLA_KV_LORA = 256
DIFF_HEADS = 4
DIFF_HD = BRANCH_W // (2 * DIFF_HEADS)
DIFF_ROT = DIFF_HD // 4
SSD_P = 64
SSD_HEADS = BRANCH_W // SSD_P
SSD_N = 128
SSD_G = 2
SSD_CONV = 4
SSD_CHUNK = 128
SSD_CONV_DIM = BRANCH_W + 2 * SSD_G * SSD_N
POOL_WINDOWS = (2, 4, 8, 16)
POOL_GROUP = BRANCH_W // 4
IN_SIZES = (MLA_Q_LORA, MLA_KV_LORA, MLA_ROPE, BRANCH_W, BRANCH_W, BRANCH_W, BRANCH_W, BRANCH_W,
            BRANCH_W, SSD_CONV_DIM, 2 * SSD_HEADS, BRANCH_W, BRANCH_W, N_BRANCH * D_MODEL)
IN_OFFS = tuple(int(v) for v in np.cumsum((0,) + IN_SIZES))

LANE = 128
HALO = 16
VMEM_LIMIT = 56 * 1024 * 1024

OFF_GMLA = 0
OFF_DQ = 1 * BRANCH_W
OFF_DK = 2 * BRANCH_W
OFF_DV = 3 * BRANCH_W
OFF_GDIFF = 4 * BRANCH_W
OFF_Z = 5 * BRANCH_W
OFF_XS = 6 * BRANCH_W
OFF_U = 7 * BRANCH_W
OFF_GPOOL = 8 * BRANCH_W
OFF_MG = 9 * BRANCH_W
OFF_CQ = OFF_MG + N_BRANCH * D_MODEL
OFF_BC = OFF_CQ + MLA_Q_LORA
OFF_CKV = OFF_BC + 2 * SSD_G * SSD_N
OFF_KR = OFF_CKV + MLA_KV_LORA
OFF_KRP = OFF_KR + LANE
NP_MAIN = OFF_KRP + LANE
DT_W = 2 * LANE

MLA_QK = 2 * LANE
LOG2E = math.log2(math.e)


def _cparams(sem, vmem=VMEM_LIMIT):
    return pltpu.CompilerParams(dimension_semantics=sem, vmem_limit_bytes=vmem)


def _tile(n, pref):
    t = min(n, pref)
    while n % t:
        t //= 2
    return t


def _silu(x):
    return x * jax.nn.sigmoid(x)


def _softplus(x):
    return jnp.maximum(x, 0.0) + jnp.log1p(jnp.exp(-jnp.abs(x)))


def _inproj_kernel(x_ref, nw_ref, w_ref, wdt_ref, o_ref, dt_ref, h_ref):
    @pl.when(pl.program_id(1) == 0)
    def _():
        x = x_ref[...]
        y = x * lax.rsqrt(jnp.mean(x * x, axis=-1, keepdims=True) + EPS)
        hb = (y * nw_ref[...]).astype(BF16)
        h_ref[...] = hb
        dt_ref[...] = jnp.dot(hb, wdt_ref[...], preferred_element_type=F32)

    o_ref[...] = jnp.dot(h_ref[...], w_ref[...], preferred_element_type=F32).astype(BF16)


def _inproj(x2, norm_w, w_main, w_dt):
    t = x2.shape[0]
    tm = _tile(t, 512)
    tn = 1024 if NP_MAIN % 1024 == 0 else 512
    return pl.pallas_call(
        _inproj_kernel,
        grid=(t // tm, NP_MAIN // tn),
        in_specs=[
            pl.BlockSpec((tm, D_MODEL), lambda i, j: (i, 0)),
            pl.BlockSpec((1, D_MODEL), lambda i, j: (0, 0)),
            pl.BlockSpec((D_MODEL, tn), lambda i, j: (0, j)),
            pl.BlockSpec((D_MODEL, DT_W), lambda i, j: (0, 0)),
        ],
        out_specs=[
            pl.BlockSpec((tm, tn), lambda i, j: (i, j)),
            pl.BlockSpec((tm, DT_W), lambda i, j: (i, 0)),
        ],
        out_shape=[jax.ShapeDtypeStruct((t, NP_MAIN), BF16), jax.ShapeDtypeStruct((t, DT_W), F32)],
        scratch_shapes=[pltpu.VMEM((tm, D_MODEL), BF16)],
        compiler_params=_cparams(("parallel", "arbitrary")),
        name="inproj",
    )(x2, norm_w.reshape(1, D_MODEL), w_main, w_dt)


def _mla_prep_kernel(cq_ref, ckv_ref, kr_ref, krp_ref, cos_ref, sin_ref, qn_ref, kvn_ref,
                     wq_ref, wqp_ref, wk_ref, wv_ref, q_ref, k_ref, v_ref):
    cos = cos_ref[...]
    sin = sin_ref[...]
    cq = cq_ref[...].astype(F32)
    ncq = (cq * lax.rsqrt(jnp.mean(cq * cq, axis=-1, keepdims=True) + EPS) * qn_ref[...]).astype(BF16)
    qm = jnp.dot(ncq, wq_ref[...], preferred_element_type=F32)
    qp = jnp.dot(ncq, wqp_ref[...], preferred_element_type=F32)
    ckv = ckv_ref[...].astype(F32)
    nkv = (ckv * lax.rsqrt(jnp.mean(ckv * ckv, axis=-1, keepdims=True) + EPS) * kvn_ref[...]).astype(BF16)
    kn = jnp.dot(nkv, wk_ref[...], preferred_element_type=F32)
    v_ref[...] = jnp.dot(nkv, wv_ref[...], preferred_element_type=F32).astype(BF16)
    kpe = (kr_ref[...].astype(F32) * cos + krp_ref[...].astype(F32) * sin).astype(BF16)
    qscale = (MLA_NOPE + MLA_ROPE) ** -0.5 * LOG2E
    for h in range(MLA_HEADS):
        lo = h * MLA_QK
        q_ref[:, lo:lo + LANE] = (qm[:, lo:lo + LANE] * qscale).astype(BF16)
        qr = qm[:, lo + LANE:lo + 2 * LANE] * cos + qp[:, h * LANE:(h + 1) * LANE] * sin
        q_ref[:, lo + LANE:lo + 2 * LANE] = (qr * qscale).astype(BF16)
        k_ref[:, lo:lo + LANE] = kn[:, h * LANE:(h + 1) * LANE].astype(BF16)
        k_ref[:, lo + LANE:lo + 2 * LANE] = kpe


def _mla_prep(proj, s, cos_t, sin_t, q_norm, kv_norm, wq, wqp, wk, wv):
    t = proj.shape[0]
    tm = _tile(s, 512)
    ns = s // tm
    hq = MLA_HEADS * MLA_QK
    const = lambda shape: pl.BlockSpec(shape, lambda i: (0, 0))
    return pl.pallas_call(
        _mla_prep_kernel,
        grid=(t // tm,),
        in_specs=[
            pl.BlockSpec((tm, MLA_Q_LORA), lambda i: (i, OFF_CQ // MLA_Q_LORA)),
            pl.BlockSpec((tm, MLA_KV_LORA), lambda i: (i, OFF_CKV // MLA_KV_LORA)),
            pl.BlockSpec((tm, LANE), lambda i: (i, OFF_KR // LANE)),
            pl.BlockSpec((tm, LANE), lambda i: (i, OFF_KRP // LANE)),
            pl.BlockSpec((tm, LANE), lambda i: (i % ns, 0)),
            pl.BlockSpec((tm, LANE), lambda i: (i % ns, 0)),
            const((1, MLA_Q_LORA)),
            const((1, MLA_KV_LORA)),
            const((MLA_Q_LORA, hq)),
            const((MLA_Q_LORA, MLA_HEADS * LANE)),
            const((MLA_KV_LORA, MLA_HEADS * MLA_NOPE)),
            const((MLA_KV_LORA, MLA_HEADS * MLA_V)),
        ],
        out_specs=[
            pl.BlockSpec((tm, hq), lambda i: (i, 0)),
            pl.BlockSpec((tm, hq), lambda i: (i, 0)),
            pl.BlockSpec((tm, MLA_HEADS * MLA_V), lambda i: (i, 0)),
        ],
        out_shape=[jax.ShapeDtypeStruct((t, hq), BF16), jax.ShapeDtypeStruct((t, hq), BF16),
                   jax.ShapeDtypeStruct((t, MLA_HEADS * MLA_V), BF16)],
        compiler_params=_cparams(("parallel",)),
        name="mla_prep",
    )(proj, proj, proj, proj, cos_t, sin_t, q_norm.reshape(1, -1), kv_norm.reshape(1, -1), wq, wqp, wk, wv)


def _softmax_pv(q, k_ref, v_ref, tk):
    del tk
    sc = lax.dot_general(q, k_ref[...], (((1,), (1,)), ((), ())), preferred_element_type=F32)
    m = jnp.max(sc, axis=-1, keepdims=True)
    p = jnp.exp2(sc - m)
    l = jnp.sum(p, axis=-1, keepdims=True)
    acc = jnp.dot(p.astype(BF16), v_ref[...], preferred_element_type=F32)
    return acc, l


def _mla_attn_kernel(q_ref, k_ref, v_ref, g_ref, o_ref, *, tk):
    acc, l = _softmax_pv(q_ref[...], k_ref, v_ref, tk)
    o_ref[...] = ((acc / l) * _silu(g_ref[...].astype(F32))).astype(BF16)


def _mla_attn(q, k, v, proj3):
    b, s, _ = q.shape
    tq = _tile(s, 256)
    tk = _tile(s, 512)
    return pl.pallas_call(
        functools.partial(_mla_attn_kernel, tk=tk),
        grid=(b, MLA_HEADS, s // tq),
        in_specs=[
            pl.BlockSpec((None, tq, MLA_QK), lambda bi, h, qi: (bi, qi, h)),
            pl.BlockSpec((None, s, MLA_QK), lambda bi, h, qi: (bi, 0, h)),
            pl.BlockSpec((None, s, MLA_V), lambda bi, h, qi: (bi, 0, h)),
            pl.BlockSpec((None, tq, MLA_V), lambda bi, h, qi: (bi, qi, OFF_GMLA // MLA_V + h)),
        ],
        out_specs=pl.BlockSpec((None, tq, MLA_V), lambda bi, h, qi: (bi, qi, h)),
        out_shape=jax.ShapeDtypeStruct((b, s, BRANCH_W), BF16),
        compiler_params=_cparams(("parallel", "parallel", "parallel")),
        name="mla_attn",
    )(q, k, v, proj3)


def _diff_prep_kernel(dq_ref, dk_ref, cos_ref, sa_ref, sb_ref, q_ref, k_ref):
    cos = cos_ref[...]
    sa = sa_ref[...]
    sb = sb_ref[...]
    half = DIFF_ROT // 2
    qscale = DIFF_HD ** -0.5 * LOG2E
    for h in range(2 * DIFF_HEADS):
        sl = slice(h * DIFF_HD, (h + 1) * DIFF_HD)
        xq = dq_ref[:, sl].astype(F32)
        xk = dk_ref[:, sl].astype(F32)
        rq = xq * cos + pltpu.roll(xq, half, 1) * sa + pltpu.roll(xq, DIFF_HD - half, 1) * sb
        rk = xk * cos + pltpu.roll(xk, half, 1) * sa + pltpu.roll(xk, DIFF_HD - half, 1) * sb
        q_ref[:, sl] = (rq * qscale).astype(BF16)
        k_ref[:, sl] = rk.astype(BF16)


def _diff_prep(proj, s, cos_t, sa_t, sb_t):
    t = proj.shape[0]
    tm = _tile(s, 512)
    ns = s // tm
    tab = pl.BlockSpec((tm, LANE), lambda i: (i % ns, 0))
    return pl.pallas_call(
        _diff_prep_kernel,
        grid=(t // tm,),
        in_specs=[
            pl.BlockSpec((tm, BRANCH_W), lambda i: (i, OFF_DQ // BRANCH_W)),
            pl.BlockSpec((tm, BRANCH_W), lambda i: (i, OFF_DK // BRANCH_W)),
            tab, tab, tab,
        ],
        out_specs=[pl.BlockSpec((tm, BRANCH_W), lambda i: (i, 0))] * 2,
        out_shape=[jax.ShapeDtypeStruct((t, BRANCH_W), BF16)] * 2,
        compiler_params=_cparams(("parallel",)),
        name="diff_prep",
    )(proj, proj, cos_t, sa_t, sb_t)


def _diff_attn_kernel(q1_ref, q2_ref, k1_ref, k2_ref, v_ref, g_ref, lam_ref, sub_ref, o_ref, *,
                      tk, lambda_init):
    acc1, l1 = _softmax_pv(q1_ref[...], k1_ref, v_ref, tk)
    acc2, l2 = _softmax_pv(q2_ref[...], k2_ref, v_ref, tk)
    lp = lam_ref[...]
    lam = (jnp.exp(jnp.sum(lp[0:1] * lp[1:2], axis=-1, keepdims=True))
           - jnp.exp(jnp.sum(lp[2:3] * lp[3:4], axis=-1, keepdims=True)) + lambda_init)
    o = acc1 / l1 - lam * (acc2 / l2)
    o = o * lax.rsqrt(jnp.mean(o * o, axis=-1, keepdims=True) + EPS) * sub_ref[...]
    o = o * (1.0 - lambda_init)
    o_ref[...] = (o * _silu(g_ref[...].astype(F32))).astype(BF16)


def _diff_attn(q, k, proj3, lam_params, subln, lambda_init):
    b, s, _ = q.shape
    tq = _tile(s, 256)
    tk = _tile(s, 512)
    dv = 2 * DIFF_HD
    return pl.pallas_call(
        functools.partial(_diff_attn_kernel, tk=tk, lambda_init=lambda_init),
        grid=(b, DIFF_HEADS, s // tq),
        in_specs=[
            pl.BlockSpec((None, tq, DIFF_HD), lambda bi, h, qi: (bi, qi, 2 * h)),
            pl.BlockSpec((None, tq, DIFF_HD), lambda bi, h, qi: (bi, qi, 2 * h + 1)),
            pl.BlockSpec((None, s, DIFF_HD), lambda bi, h, qi: (bi, 0, 2 * h)),
            pl.BlockSpec((None, s, DIFF_HD), lambda bi, h, qi: (bi, 0, 2 * h + 1)),
            pl.BlockSpec((None, s, dv), lambda bi, h, qi: (bi, 0, OFF_DV // dv + h)),
            pl.BlockSpec((None, tq, dv), lambda bi, h, qi: (bi, qi, OFF_GDIFF // dv + h)),
            pl.BlockSpec((4, DIFF_HD), lambda bi, h, qi: (0, 0)),
            pl.BlockSpec((1, dv), lambda bi, h, qi: (0, 0)),
        ],
        out_specs=pl.BlockSpec((None, tq, dv), lambda bi, h, qi: (bi, qi, h)),
        out_shape=jax.ShapeDtypeStruct((b, s, BRANCH_W), BF16),
        compiler_params=_cparams(("parallel", "parallel", "parallel")),
        name="diff_attn",
    )(q, q, k, k, proj3, proj3, lam_params, subln.reshape(1, dv))


def _ssd_conv_kernel(xs_ref, xsp_ref, xsn_ref, bc_ref, bcp_ref, bcn_ref, w_ref, b_ref,
                     oxs_ref, obc_ref, ext_ref, *, tm):
    i = pl.program_id(1)
    has_prev = i > 0
    has_next = i < pl.num_programs(1) - 1
    pad_l = SSD_CONV // 2

    def conv(cur_ref, prev_ref, next_ref, out_ref, c0, width):
        ext_ref[0:HALO, 0:width] = jnp.where(has_prev, prev_ref[...].astype(F32), 0.0)
        ext_ref[HALO:HALO + tm, 0:width] = cur_ref[...].astype(F32)
        ext_ref[HALO + tm:, 0:width] = jnp.where(has_next, next_ref[...].astype(F32), 0.0)
        acc = b_ref[:, c0:c0 + width]
        for j in range(SSD_CONV):
            off = HALO - pad_l + j
            acc = acc + ext_ref[off:off + tm, 0:width] * w_ref[j:j + 1, c0:c0 + width]
        out_ref[...] = _silu(acc).astype(BF16)

    conv(xs_ref, xsp_ref, xsn_ref, oxs_ref, 0, BRANCH_W)
    conv(bc_ref, bcp_ref, bcn_ref, obc_ref, BRANCH_W, SSD_CONV_DIM - BRANCH_W)


def _halo_specs(tm, s, width, col):
    r = tm // HALO
    last = s // HALO - 1
    cur = pl.BlockSpec((None, tm, width), lambda bi, i: (bi, i, col))
    prev = pl.BlockSpec((None, HALO, width), lambda bi, i: (bi, jnp.maximum(i * r - 1, 0), col))
    nxt = pl.BlockSpec((None, HALO, width), lambda bi, i: (bi, jnp.minimum((i + 1) * r, last), col))
    return [cur, prev, nxt]


def _ssd_conv(proj3, conv_w, conv_b):
    b, s, _ = proj3.shape
    tm = _tile(s, 512)
    wbc = SSD_CONV_DIM - BRANCH_W
    return pl.pallas_call(
        functools.partial(_ssd_conv_kernel, tm=tm),
        grid=(b, s // tm),
        in_specs=(_halo_specs(tm, s, BRANCH_W, OFF_XS // BRANCH_W) + _halo_specs(tm, s, wbc, OFF_BC // wbc)
                  + [pl.BlockSpec((SSD_CONV, SSD_CONV_DIM), lambda bi, i: (0, 0)),
                     pl.BlockSpec((1, SSD_CONV_DIM), lambda bi, i: (0, 0))]),
        out_specs=[pl.BlockSpec((None, tm, BRANCH_W), lambda bi, i: (bi, i, 0)),
                   pl.BlockSpec((None, tm, wbc), lambda bi, i: (bi, i, 0))],
        out_shape=[jax.ShapeDtypeStruct((b, s, BRANCH_W), BF16), jax.ShapeDtypeStruct((b, s, wbc), BF16)],
        scratch_shapes=[pltpu.VMEM((tm + 2 * HALO, BRANCH_W), F32)],
        compiler_params=_cparams(("parallel", "parallel")),
        name="ssd_conv",
    )(proj3, proj3, proj3, proj3, proj3, proj3, conv_w, conv_b.reshape(1, -1))


def _ssd_chunk(xs_ref, bc_ref, dt_ref, bias_ref, alog_ref, st_ref, reverse):
    ln = SSD_CHUNK
    gw = BRANCH_W // SSD_G
    xs = xs_ref[...].astype(F32)
    bc = bc_ref[...]
    dt = _softplus(dt_ref[...] + bias_ref[...])
    da = dt * (-jnp.exp(alog_ref[...]))
    ri = lax.broadcasted_iota(jnp.int32, (ln, ln), 0)
    ci = lax.broadcasted_iota(jnp.int32, (ln, ln), 1)
    tri = (ci >= ri) if reverse else (ri >= ci)
    cs = jnp.dot(tri.astype(F32), da, precision=HIGHEST, preferred_element_type=F32)
    cs_end = cs[0:1, :] if reverse else cs[ln - 1:ln, :]
    er = lax.broadcasted_iota(jnp.int32, (LANE, BRANCH_W), 0)
    ec = lax.broadcasted_iota(jnp.int32, (LANE, BRANCH_W), 1)
    expand = (ec // SSD_P == er).astype(F32)
    widen = lambda a: jnp.dot(a, expand, precision=HIGHEST, preferred_element_type=F32)
    dt_e = widen(dt)
    dec_out_e = widen(jnp.exp(cs))
    dec_st_e = widen(jnp.exp(cs_end - cs))
    tot_e = dec_out_e[0:1, :] if reverse else dec_out_e[ln - 1:ln, :]
    cs_t = cs.T
    xd = xs * dt_e
    xdb = xd.astype(BF16)
    xst = (xd * dec_st_e).astype(BF16)
    low = lax.broadcasted_iota(jnp.int32, (ln, LANE), 1) < SSD_P
    zero = jnp.zeros((ln, LANE), BF16)
    outs = []
    for g in range(SSD_G):
        bg = bc[:, g * SSD_N:(g + 1) * SSD_N]
        cg = bc[:, (SSD_G + g) * SSD_N:(SSD_G + g + 1) * SSD_N]
        cb = lax.dot_general(cg, bg, (((1,), (1,)), ((), ())), preferred_element_type=F32)
        st = st_ref[g]
        y_off = jnp.dot(cg, st.astype(BF16), preferred_element_type=F32) * dec_out_e[:, g * gw:(g + 1) * gw]
        bt = bg.astype(F32).T.astype(BF16)
        st_ref[g] = (st * tot_e[:, g * gw:(g + 1) * gw]
                     + jnp.dot(bt, xst[:, g * gw:(g + 1) * gw], preferred_element_type=F32))
        for kp in range(gw // LANE):
            mats = []
            for hh in range(2):
                hd = g * (SSD_HEADS // SSD_G) + 2 * kp + hh
                seg = cs[:, hd:hd + 1] - cs_t[hd:hd + 1, :]
                lm = jnp.exp(jnp.where(tri, seg, -jnp.inf))
                mats.append((cb * lm).astype(BF16))
            lhs = jnp.concatenate(mats, axis=1)
            col = g * gw + kp * LANE
            xp = xdb[:, col:col + LANE]
            rhs = jnp.concatenate([jnp.where(low, xp, zero), jnp.where(low, zero, xp)], axis=0)
            y_diag = jnp.dot(lhs, rhs, preferred_element_type=F32)
            outs.append(y_diag + y_off[:, kp * LANE:(kp + 1) * LANE])
    return jnp.concatenate(outs, axis=1), xs


def _ssd_fwd_kernel(xs_ref, bc_ref, dt_ref, bias_ref, alog_ref, y_ref, st_ref):
    @pl.when(pl.program_id(1) == 0)
    def _():
        st_ref[...] = jnp.zeros_like(st_ref)

    y, _ = _ssd_chunk(xs_ref, bc_ref, dt_ref, bias_ref, alog_ref, st_ref, reverse=False)
    y_ref[...] = y


def _ssd_bwd_kernel(xs_ref, bc_ref, dt_ref, bias_ref, alog_ref, yf_ref, z_ref, d_ref, nw_ref,
                    o_ref, st_ref):
    @pl.when(pl.program_id(1) == 0)
    def _():
        st_ref[...] = jnp.zeros_like(st_ref)

    yb, xs = _ssd_chunk(xs_ref, bc_ref, dt_ref, bias_ref, alog_ref, st_ref, reverse=True)
    y = yf_ref[...] + yb + xs * d_ref[...]
    y = y * _silu(z_ref[...].astype(F32))
    gw = BRANCH_W // SSD_G
    for g in range(SSD_G):
        yg = y[:, g * gw:(g + 1) * gw]
        yn = yg * lax.rsqrt(jnp.mean(yg * yg, axis=-1, keepdims=True) + EPS)
        o_ref[:, g * gw:(g + 1) * gw] = (yn * nw_ref[:, g * gw:(g + 1) * gw]).astype(BF16)


def _ssd_scan(xs_act, bc_act, dt3, proj3, dt_bias, a_log, d_skip, norm_w):
    b, s, _ = xs_act.shape
    nc = s // SSD_CHUNK
    wbc = SSD_CONV_DIM - BRANCH_W
    gw = BRANCH_W // SSD_G
    pad = lambda v: jnp.pad(v.astype(F32), (0, LANE - SSD_HEADS)).reshape(1, LANE)
    scratch = [pltpu.VMEM((SSD_G, SSD_N, gw), F32)]
    vec = lambda w: pl.BlockSpec((1, w), lambda bi, c: (0, 0))

    def specs(cmap, d):
        return [pl.BlockSpec((None, SSD_CHUNK, BRANCH_W), lambda bi, c: (bi, cmap(c), 0)),
                pl.BlockSpec((None, SSD_CHUNK, wbc), lambda bi, c: (bi, cmap(c), 0)),
                pl.BlockSpec((None, SSD_CHUNK, LANE), lambda bi, c: (bi, cmap(c), d)),
                vec(LANE), vec(LANE)]

    y_f = pl.pallas_call(
        _ssd_fwd_kernel,
        grid=(b, nc),
        in_specs=specs(lambda c: c, 0),
        out_specs=pl.BlockSpec((None, SSD_CHUNK, BRANCH_W), lambda bi, c: (bi, c, 0)),
        out_shape=jax.ShapeDtypeStruct((b, s, BRANCH_W), F32),
        scratch_shapes=scratch,
        compiler_params=_cparams(("parallel", "arbitrary")),
        name="ssd_fwd",
    )(xs_act, bc_act, dt3, pad(dt_bias[0]), pad(a_log[0]))

    rev = lambda c: nc - 1 - c
    return pl.pallas_call(
        _ssd_bwd_kernel,
        grid=(b, nc),
        in_specs=specs(rev, 1) + [
            pl.BlockSpec((None, SSD_CHUNK, BRANCH_W), lambda bi, c: (bi, rev(c), 0)),
            pl.BlockSpec((None, SSD_CHUNK, BRANCH_W), lambda bi, c: (bi, rev(c), OFF_Z // BRANCH_W)),
            vec(BRANCH_W), vec(BRANCH_W)],
        out_specs=pl.BlockSpec((None, SSD_CHUNK, BRANCH_W), lambda bi, c: (bi, rev(c), 0)),
        out_shape=jax.ShapeDtypeStruct((b, s, BRANCH_W), BF16),
        scratch_shapes=scratch,
        compiler_params=_cparams(("parallel", "arbitrary")),
        name="ssd_bwd",
    )(xs_act, bc_act, dt3, pad(dt_bias[1]), pad(a_log[1]), y_f, proj3,
      jnp.repeat(d_skip.astype(F32), SSD_P).reshape(1, BRANCH_W), norm_w.reshape(1, BRANCH_W))


def _pool_kernel(u_ref, up_ref, un_ref, g_ref, w_ref, sc_ref, o_ref, ext_ref, *, tm, s):
    i = pl.program_id(1)
    cur = u_ref[...].astype(F32)
    ext_ref[0:HALO, :] = jnp.where(i > 0, up_ref[...].astype(F32), 0.0)
    ext_ref[HALO:HALO + tm, :] = cur
    ext_ref[HALO + tm:, :] = jnp.where(i < pl.num_programs(1) - 1, un_ref[...].astype(F32), 0.0)
    pos = i * tm + lax.broadcasted_iota(jnp.int32, (tm, 1), 0)
    for gi, w in enumerate(POOL_WINDOWS):
        lo = w // 2
        hi = w - 1 - lo
        sl = slice(gi * POOL_GROUP, (gi + 1) * POOL_GROUP)
        acc = ext_ref[HALO - lo:HALO - lo + tm, sl]
        for d in range(-lo + 1, hi + 1):
            acc = acc + ext_ref[HALO + d:HALO + d + tm, sl]
        cnt = (jnp.minimum(pos + hi + 1, s) - jnp.maximum(pos - lo, 0)).astype(F32)
        pooled = acc / cnt - cur[:, sl]
        mixed = jnp.dot(pooled.astype(BF16), w_ref[gi], preferred_element_type=F32)
        o_ref[:, sl] = (mixed * sc_ref[:, sl] * _silu(g_ref[:, sl].astype(F32))).astype(BF16)


def _pool(proj3, pool_w, pool_scale):
    b, s, _ = proj3.shape
    tm = _tile(s, 512)
    return pl.pallas_call(
        functools.partial(_pool_kernel, tm=tm, s=s),
        grid=(b, s // tm),
        in_specs=_halo_specs(tm, s, BRANCH_W, OFF_U // BRANCH_W) + [
            pl.BlockSpec((None, tm, BRANCH_W), lambda bi, i: (bi, i, OFF_GPOOL // BRANCH_W)),
            pl.BlockSpec((len(POOL_WINDOWS), POOL_GROUP, POOL_GROUP), lambda bi, i: (0, 0, 0)),
            pl.BlockSpec((1, BRANCH_W), lambda bi, i: (0, 0))],
        out_specs=pl.BlockSpec((None, tm, BRANCH_W), lambda bi, i: (bi, i, 0)),
        out_shape=jax.ShapeDtypeStruct((b, s, BRANCH_W), BF16),
        scratch_shapes=[pltpu.VMEM((tm + 2 * HALO, BRANCH_W), F32)],
        compiler_params=_cparams(("parallel", "parallel")),
        name="pool",
    )(proj3, proj3, proj3, proj3, pool_w, pool_scale.reshape(1, BRANCH_W))


def _merge_kernel(*refs):
    brs, ws, gs, o_ref = refs[0:4], refs[4:8], refs[8:12], refs[12]
    acc = None
    for br, w, g in zip(brs, ws, gs):
        term = jax.nn.sigmoid(g[...].astype(F32)) * jnp.dot(br[...], w[...], preferred_element_type=F32)
        acc = term if acc is None else acc + term
    o_ref[...] = acc.astype(BF16)


def _merge(branches, w_branch, proj):
    t = proj.shape[0]
    tm = _tile(t, 512)
    tn = 512
    br_spec = pl.BlockSpec((tm, BRANCH_W), lambda i, j: (i, 0))
    w_specs = [pl.BlockSpec((None, BRANCH_W, tn), lambda i, j, k=k: (k, 0, j)) for k in range(N_BRANCH)]
    g_specs = [pl.BlockSpec((tm, tn), lambda i, j, k=k: (i, (OFF_MG + k * D_MODEL) // tn + j))
               for k in range(N_BRANCH)]
    return pl.pallas_call(
        _merge_kernel,
        grid=(t // tm, D_MODEL // tn),
        in_specs=[br_spec] * N_BRANCH + w_specs + g_specs,
        out_specs=pl.BlockSpec((tm, tn), lambda i, j: (i, j)),
        out_shape=jax.ShapeDtypeStruct((t, D_MODEL), BF16),
        compiler_params=_cparams(("parallel", "parallel")),
        name="merge",
    )(*branches, *([w_branch] * N_BRANCH), *([proj] * N_BRANCH))


def _out_kernel(m_ref, w_ref, x_ref, fn_ref, o_ref, *, final):
    y = x_ref[...] + jnp.dot(m_ref[...], w_ref[...], preferred_element_type=F32)
    if final:
        y = y * lax.rsqrt(jnp.mean(y * y, axis=-1, keepdims=True) + EPS) * fn_ref[...]
    o_ref[...] = y


def _out_proj(merged, w_out, x2, final_norm, final):
    t = x2.shape[0]
    tm = _tile(t, 512)
    return pl.pallas_call(
        functools.partial(_out_kernel, final=final),
        grid=(t // tm,),
        in_specs=[pl.BlockSpec((tm, D_MODEL), lambda i: (i, 0)),
                  pl.BlockSpec((D_MODEL, D_MODEL), lambda i: (0, 0)),
                  pl.BlockSpec((tm, D_MODEL), lambda i: (i, 0)),
                  pl.BlockSpec((1, D_MODEL), lambda i: (0, 0))],
        out_specs=pl.BlockSpec((tm, D_MODEL), lambda i: (i, 0)),
        out_shape=jax.ShapeDtypeStruct((t, D_MODEL), F32),
        compiler_params=_cparams(("parallel",)),
        name="out_proj",
    )(merged, w_out, x2, final_norm.reshape(1, D_MODEL))


def _pack_layer(w_in, w_uq, w_ukv):
    seg = [w_in[:, IN_OFFS[k]:IN_OFFS[k + 1]] for k in range(len(IN_SIZES))]
    cq, ckv, kr, gmla, dq, dk, dv, gdiff, z, xbc, dt, u, gpool, mg = seg
    zpad = lambda n: jnp.zeros((w_in.shape[0], n), w_in.dtype)
    half = MLA_ROPE // 2
    kr_main = jnp.concatenate([kr, zpad(LANE - MLA_ROPE)], axis=1)
    kr_part = jnp.concatenate([kr[:, half:], kr[:, :half], zpad(LANE - MLA_ROPE)], axis=1)
    w_main = jnp.concatenate([gmla, dq, dk, dv, gdiff, z, xbc[:, :BRANCH_W], u, gpool, mg, cq,
                              xbc[:, BRANCH_W:], ckv, kr_main, kr_part], axis=1).astype(BF16)
    w_dt = jnp.concatenate([dt[:, :SSD_HEADS], zpad(LANE - SSD_HEADS),
                            dt[:, SSD_HEADS:], zpad(LANE - SSD_HEADS)], axis=1).astype(BF16)
    wq3 = w_uq.reshape(MLA_Q_LORA, MLA_HEADS, MLA_NOPE + MLA_ROPE)
    zq = jnp.zeros((MLA_Q_LORA, MLA_HEADS, LANE - MLA_ROPE), w_uq.dtype)
    wq = jnp.concatenate([wq3, zq], axis=2).reshape(MLA_Q_LORA, MLA_HEADS * MLA_QK).astype(BF16)
    wqp = jnp.concatenate([wq3[:, :, MLA_NOPE + half:], wq3[:, :, MLA_NOPE:MLA_NOPE + half], zq],
                          axis=2).reshape(MLA_Q_LORA, MLA_HEADS * LANE).astype(BF16)
    wkv3 = w_ukv.reshape(MLA_KV_LORA, MLA_HEADS, MLA_NOPE + MLA_V)
    wk = wkv3[:, :, :MLA_NOPE].reshape(MLA_KV_LORA, MLA_HEADS * MLA_NOPE).astype(BF16)
    wv = wkv3[:, :, MLA_NOPE:].reshape(MLA_KV_LORA, MLA_HEADS * MLA_V).astype(BF16)
    return w_main, w_dt, wq, wqp, wk, wv


def _rope_tables(s):
    def angles(rot):
        half = rot // 2
        inv_freq = jnp.power(ROPE_THETA, -jnp.arange(half, dtype=F32) * 2.0 / rot)
        ang = jnp.arange(s, dtype=F32)[:, None] * inv_freq[None, :]
        return jnp.cos(ang), jnp.sin(ang)

    c, sn = angles(MLA_ROPE)
    z = jnp.zeros((s, LANE - MLA_ROPE), F32)
    mla_cos = jnp.concatenate([c, c, z], axis=1)
    mla_sin = jnp.concatenate([-sn, sn, z], axis=1)
    c, sn = angles(DIFF_ROT)
    zh = jnp.zeros_like(sn)
    rest = LANE - DIFF_ROT
    d_cos = jnp.concatenate([c, c, jnp.ones((s, rest), F32)], axis=1)
    d_sa = jnp.concatenate([zh, sn, jnp.zeros((s, rest), F32)], axis=1)
    d_sb = jnp.concatenate([-sn, zh, jnp.zeros((s, rest), F32)], axis=1)
    return mla_cos, mla_sin, d_cos, d_sa, d_sb


def _trunk(x, packed, norm_w, mla_q_norm, mla_kv_norm, diff_lambda, diff_subln, ssd_conv_w, ssd_conv_b,
           ssd_dt_bias, ssd_a_log, ssd_d, ssd_norm, pool_w, pool_scale, w_branch, w_out, final_norm):
    b, s, _ = x.shape
    t = b * s
    mla_cos, mla_sin, d_cos, d_sa, d_sb = _rope_tables(s)
    x2 = x.reshape(t, D_MODEL)
    for l in range(DEPTH):
        w_main, w_dt, wq, wqp, wk, wv = packed[l]
        lambda_init = 0.8 - 0.6 * math.exp(-0.3 * l)
        proj, dt = _inproj(x2, norm_w[l], w_main, w_dt)
        proj3 = proj.reshape(b, s, NP_MAIN)
        q, k, v = _mla_prep(proj, s, mla_cos, mla_sin, mla_q_norm[l], mla_kv_norm[l], wq, wqp, wk, wv)
        br_mla = _mla_attn(q.reshape(b, s, -1), k.reshape(b, s, -1), v.reshape(b, s, -1), proj3)
        dq, dk = _diff_prep(proj, s, d_cos, d_sa, d_sb)
        br_diff = _diff_attn(dq.reshape(b, s, -1), dk.reshape(b, s, -1), proj3, diff_lambda[l],
                             diff_subln[l], lambda_init)
        xs_act, bc_act = _ssd_conv(proj3, ssd_conv_w[l], ssd_conv_b[l])
        br_ssd = _ssd_scan(xs_act, bc_act, dt.reshape(b, s, DT_W), proj3, ssd_dt_bias[l], ssd_a_log[l],
                           ssd_d[l], ssd_norm[l])
        br_pool = _pool(proj3, pool_w[l].astype(BF16), pool_scale[l])
        branches = [a.reshape(t, BRANCH_W) for a in (br_mla, br_diff, br_ssd, br_pool)]
        merged = _merge(branches, w_branch[l].astype(BF16), proj)
        x2 = _out_proj(merged, w_out[l].astype(BF16), x2, final_norm, final=(l == DEPTH - 1))
    return x2.reshape(b, s, D_MODEL)


def kernel(x_prompt, x_sample, norm_w, w_in, mla_q_norm, mla_w_uq, mla_kv_norm, mla_w_ukv, diff_lambda,
           diff_subln, ssd_conv_w, ssd_conv_b, ssd_dt_bias, ssd_a_log, ssd_d, ssd_norm, pool_w, pool_scale,
           w_branch, w_out, final_norm):
    packed = [_pack_layer(w_in[l], mla_w_uq[l], mla_w_ukv[l]) for l in range(DEPTH)]
    rest = (norm_w, mla_q_norm, mla_kv_norm, diff_lambda, diff_subln, ssd_conv_w, ssd_conv_b, ssd_dt_bias,
            ssd_a_log, ssd_d, ssd_norm, pool_w, pool_scale, w_branch, w_out, final_norm)
    return (_trunk(x_prompt, packed, *rest), _trunk(x_sample, packed, *rest))
```

```python
import functools
import math

import numpy as np
import jax
import jax.numpy as jnp
from jax import lax
from jax.experimental import pallas as pl
from jax.experimental.pallas import tpu as pltpu

F32 = jnp.float32
BF16 = jnp.bfloat16
HIGHEST = lax.Precision.HIGHEST

D_MODEL = 2048
DEPTH = 2
BRANCH_W = D_MODEL // 2
N_BRANCH = 4
ROPE_THETA = 500000.0
EPS = 1e-6
MLA_HEADS = 8
MLA_NOPE = 128
MLA_ROPE = 64
MLA_V = BRANCH_W // MLA_HEADS
MLA_Q_LORA = 512
MLA_KV_LORA = 256
DIFF_HEADS = 4
DIFF_HD = BRANCH_W // (2 * DIFF_HEADS)
DIFF_ROT = DIFF_HD // 4
SSD_P = 64
SSD_HEADS = BRANCH_W // SSD_P
SSD_N = 128
SSD_G = 2
SSD_CONV = 4
SSD_CHUNK = 128
SSD_BLOCK = 512
SSD_CONV_DIM = BRANCH_W + 2 * SSD_G * SSD_N
POOL_WINDOWS = (2, 4, 8, 16)
POOL_GROUP = BRANCH_W // 4
IN_SIZES = (MLA_Q_LORA, MLA_KV_LORA, MLA_ROPE, BRANCH_W, BRANCH_W, BRANCH_W, BRANCH_W, BRANCH_W,
            BRANCH_W, SSD_CONV_DIM, 2 * SSD_HEADS, BRANCH_W, BRANCH_W, N_BRANCH * D_MODEL)
IN_OFFS = tuple(int(v) for v in np.cumsum((0,) + IN_SIZES))

LANE = 128
HALO = 16
VMEM_LIMIT = 56 * 1024 * 1024

OFF_GMLA = 0
OFF_DQ = 1 * BRANCH_W
OFF_DK = 2 * BRANCH_W
OFF_DV = 3 * BRANCH_W
OFF_GDIFF = 4 * BRANCH_W
OFF_Z = 5 * BRANCH_W
OFF_XS = 6 * BRANCH_W
OFF_U = 7 * BRANCH_W
OFF_GPOOL = 8 * BRANCH_W
OFF_MG = 9 * BRANCH_W
OFF_CQ = OFF_MG + N_BRANCH * D_MODEL
OFF_BC = OFF_CQ + MLA_Q_LORA
OFF_CKV = OFF_BC + 2 * SSD_G * SSD_N
OFF_KR = OFF_CKV + MLA_KV_LORA
OFF_KRP = OFF_KR + LANE
NP_MAIN = OFF_KRP + LANE
DT_W = 2 * LANE

ATTN_TQ = 1024
ATTN_SUB = 256
MLA_QK = 2 * LANE
LOG2E = math.log2(math.e)


def _cparams(sem, vmem=VMEM_LIMIT):
    return pltpu.CompilerParams(dimension_semantics=sem, vmem_limit_bytes=vmem)


def _tile(n, pref):
    t = min(n, pref)
    while n % t:
        t //= 2
    return t


def _silu(x):
    return x * jax.nn.sigmoid(x)


def _softplus(x):
    return jnp.maximum(x, 0.0) + jnp.log1p(jnp.exp(-jnp.abs(x)))


def _inproj_kernel(x_ref, nw_ref, w_ref, wdt_ref, o_ref, dt_ref, h_ref):
    @pl.when(pl.program_id(1) == 0)
    def _():
        x = x_ref[...]
        y = x * lax.rsqrt(jnp.mean(x * x, axis=-1, keepdims=True) + EPS)
        hb = (y * nw_ref[...]).astype(BF16)
        h_ref[...] = hb
        dt_ref[...] = jnp.dot(hb, wdt_ref[...], preferred_element_type=F32)

    o_ref[...] = jnp.dot(h_ref[...], w_ref[...], preferred_element_type=F32).astype(BF16)


def _inproj(x2, norm_w, w_main, w_dt):
    t = x2.shape[0]
    tm = _tile(t, 1024)
    tn = 1024 if NP_MAIN % 1024 == 0 else 512
    return pl.pallas_call(
        _inproj_kernel,
        grid=(t // tm, NP_MAIN // tn),
        in_specs=[
            pl.BlockSpec((tm, D_MODEL), lambda i, j: (i, 0)),
            pl.BlockSpec((1, D_MODEL), lambda i, j: (0, 0)),
            pl.BlockSpec((D_MODEL, tn), lambda i, j: (0, j)),
            pl.BlockSpec((D_MODEL, DT_W), lambda i, j: (0, 0)),
        ],
        out_specs=[
            pl.BlockSpec((tm, tn), lambda i, j: (i, j)),
            pl.BlockSpec((tm, DT_W), lambda i, j: (i, 0)),
        ],
        out_shape=[jax.ShapeDtypeStruct((t, NP_MAIN), BF16), jax.ShapeDtypeStruct((t, DT_W), F32)],
        scratch_shapes=[pltpu.VMEM((tm, D_MODEL), BF16)],
        compiler_params=_cparams(("parallel", "arbitrary")),
        name="inproj",
    )(x2, norm_w.reshape(1, D_MODEL), w_main, w_dt)


def _mla_prep_kernel(cq_ref, ckv_ref, kr_ref, krp_ref, cos_ref, sin_ref, qn_ref, kvn_ref,
                     wq_ref, wqp_ref, wk_ref, wv_ref, q_ref, k_ref, v_ref):
    cos = cos_ref[...]
    sin = sin_ref[...]
    cq = cq_ref[...].astype(F32)
    ncq = (cq * lax.rsqrt(jnp.mean(cq * cq, axis=-1, keepdims=True) + EPS) * qn_ref[...]).astype(BF16)
    qm = jnp.dot(ncq, wq_ref[...], preferred_element_type=F32)
    qp = jnp.dot(ncq, wqp_ref[...], preferred_element_type=F32)
    ckv = ckv_ref[...].astype(F32)
    nkv = (ckv * lax.rsqrt(jnp.mean(ckv * ckv, axis=-1, keepdims=True) + EPS) * kvn_ref[...]).astype(BF16)
    kn = jnp.dot(nkv, wk_ref[...], preferred_element_type=F32)
    v_ref[...] = jnp.dot(nkv, wv_ref[...], preferred_element_type=F32).astype(BF16)
    kpe = (kr_ref[...].astype(F32) * cos + krp_ref[...].astype(F32) * sin).astype(BF16)
    qscale = (MLA_NOPE + MLA_ROPE) ** -0.5 * LOG2E
    for h in range(MLA_HEADS):
        lo = h * MLA_QK
        q_ref[:, lo:lo + LANE] = (qm[:, lo:lo + LANE] * qscale).astype(BF16)
        qr = qm[:, lo + LANE:lo + 2 * LANE] * cos + qp[:, h * LANE:(h + 1) * LANE] * sin
        q_ref[:, lo + LANE:lo + 2 * LANE] = (qr * qscale).astype(BF16)
        k_ref[:, lo:lo + LANE] = kn[:, h * LANE:(h + 1) * LANE].astype(BF16)
        k_ref[:, lo + LANE:lo + 2 * LANE] = kpe


def _mla_prep(proj, s, cos_t, sin_t, q_norm, kv_norm, wq, wqp, wk, wv):
    t = proj.shape[0]
    tm = _tile(s, 512)
    ns = s // tm
    hq = MLA_HEADS * MLA_QK
    const = lambda shape: pl.BlockSpec(shape, lambda i: (0, 0))
    return pl.pallas_call(
        _mla_prep_kernel,
        grid=(t // tm,),
        in_specs=[
            pl.BlockSpec((tm, MLA_Q_LORA), lambda i: (i, OFF_CQ // MLA_Q_LORA)),
            pl.BlockSpec((tm, MLA_KV_LORA), lambda i: (i, OFF_CKV // MLA_KV_LORA)),
            pl.BlockSpec((tm, LANE), lambda i: (i, OFF_KR // LANE)),
            pl.BlockSpec((tm, LANE), lambda i: (i, OFF_KRP // LANE)),
            pl.BlockSpec((tm, LANE), lambda i: (i % ns, 0)),
            pl.BlockSpec((tm, LANE), lambda i: (i % ns, 0)),
            const((1, MLA_Q_LORA)),
            const((1, MLA_KV_LORA)),
            const((MLA_Q_LORA, hq)),
            const((MLA_Q_LORA, MLA_HEADS * LANE)),
            const((MLA_KV_LORA, MLA_HEADS * MLA_NOPE)),
            const((MLA_KV_LORA, MLA_HEADS * MLA_V)),
        ],
        out_specs=[
            pl.BlockSpec((tm, hq), lambda i: (i, 0)),
            pl.BlockSpec((tm, hq), lambda i: (i, 0)),
            pl.BlockSpec((tm, MLA_HEADS * MLA_V), lambda i: (i, 0)),
        ],
        out_shape=[jax.ShapeDtypeStruct((t, hq), BF16), jax.ShapeDtypeStruct((t, hq), BF16),
                   jax.ShapeDtypeStruct((t, MLA_HEADS * MLA_V), BF16)],
        compiler_params=_cparams(("parallel",)),
        name="mla_prep",
    )(proj, proj, proj, proj, cos_t, sin_t, q_norm.reshape(1, -1), kv_norm.reshape(1, -1), wq, wqp, wk, wv)


def _softmax_pv(q, k_ref, v_ref):
    sc = lax.dot_general(q, k_ref[...], (((1,), (1,)), ((), ())), preferred_element_type=F32)
    m = jnp.max(sc, axis=-1, keepdims=True)
    p = jnp.exp2(sc - m)
    l = jnp.sum(p, axis=-1, keepdims=True)
    acc = jnp.dot(p.astype(BF16), v_ref[...], preferred_element_type=F32)
    return acc, l


def _mla_attn_kernel(q_ref, k_ref, v_ref, g_ref, o_ref, *, sub):
    for r in range(0, q_ref.shape[0], sub):
        acc, l = _softmax_pv(q_ref[r:r + sub, :], k_ref, v_ref)
        o_ref[r:r + sub, :] = ((acc / l) * _silu(g_ref[r:r + sub, :].astype(F32))).astype(BF16)


def _mla_attn(q, k, v, proj3):
    b, s, _ = q.shape
    tq = _tile(s, ATTN_TQ)
    return pl.pallas_call(
        functools.partial(_mla_attn_kernel, sub=min(tq, ATTN_SUB)),
        grid=(b, MLA_HEADS, s // tq),
        in_specs=[
            pl.BlockSpec((None, tq, MLA_QK), lambda bi, h, qi: (bi, qi, h)),
            pl.BlockSpec((None, s, MLA_QK), lambda bi, h, qi: (bi, 0, h)),
            pl.BlockSpec((None, s, MLA_V), lambda bi, h, qi: (bi, 0, h)),
            pl.BlockSpec((None, tq, MLA_V), lambda bi, h, qi: (bi, qi, OFF_GMLA // MLA_V + h)),
        ],
        out_specs=pl.BlockSpec((None, tq, MLA_V), lambda bi, h, qi: (bi, qi, h)),
        out_shape=jax.ShapeDtypeStruct((b, s, BRANCH_W), BF16),
        compiler_params=_cparams(("parallel", "parallel", "parallel")),
        name="mla_attn",
    )(q, k, v, proj3)


def _diff_prep_kernel(dq_ref, dk_ref, cos_ref, sa_ref, sb_ref, q_ref, k_ref):
    cos = cos_ref[...]
    sa = sa_ref[...]
    sb = sb_ref[...]
    half = DIFF_ROT // 2
    qscale = DIFF_HD ** -0.5 * LOG2E
    for h in range(2 * DIFF_HEADS):
        sl = slice(h * DIFF_HD, (h + 1) * DIFF_HD)
        xq = dq_ref[:, sl].astype(F32)
        xk = dk_ref[:, sl].astype(F32)
        rq = xq * cos + pltpu.roll(xq, half, 1) * sa + pltpu.roll(xq, DIFF_HD - half, 1) * sb
        rk = xk * cos + pltpu.roll(xk, half, 1) * sa + pltpu.roll(xk, DIFF_HD - half, 1) * sb
        q_ref[:, sl] = (rq * qscale).astype(BF16)
        k_ref[:, sl] = rk.astype(BF16)


def _diff_prep(proj, s, cos_t, sa_t, sb_t):
    t = proj.shape[0]
    tm = _tile(s, 512)
    ns = s // tm
    tab = pl.BlockSpec((tm, LANE), lambda i: (i % ns, 0))
    return pl.pallas_call(
        _diff_prep_kernel,
        grid=(t // tm,),
        in_specs=[
            pl.BlockSpec((tm, BRANCH_W), lambda i: (i, OFF_DQ // BRANCH_W)),
            pl.BlockSpec((tm, BRANCH_W), lambda i: (i, OFF_DK // BRANCH_W)),
            tab, tab, tab,
        ],
        out_specs=[pl.BlockSpec((tm, BRANCH_W), lambda i: (i, 0))] * 2,
        out_shape=[jax.ShapeDtypeStruct((t, BRANCH_W), BF16)] * 2,
        compiler_params=_cparams(("parallel",)),
        name="diff_prep",
    )(proj, proj, cos_t, sa_t, sb_t)


def _diff_attn_kernel(q1_ref, q2_ref, k1_ref, k2_ref, v_ref, g_ref, lam_ref, sub_ref, o_ref, *,
                      sub, lambda_init):
    lp = lam_ref[...]
    lam = (jnp.exp(jnp.sum(lp[0:1] * lp[1:2], axis=-1, keepdims=True))
           - jnp.exp(jnp.sum(lp[2:3] * lp[3:4], axis=-1, keepdims=True)) + lambda_init)
    for r in range(0, q1_ref.shape[0], sub):
        acc1, l1 = _softmax_pv(q1_ref[r:r + sub, :], k1_ref, v_ref)
        acc2, l2 = _softmax_pv(q2_ref[r:r + sub, :], k2_ref, v_ref)
        o = acc1 / l1 - lam * (acc2 / l2)
        o = o * lax.rsqrt(jnp.mean(o * o, axis=-1, keepdims=True) + EPS) * sub_ref[...]
        o = o * (1.0 - lambda_init)
        o_ref[r:r + sub, :] = (o * _silu(g_ref[r:r + sub, :].astype(F32))).astype(BF16)


def _diff_attn(q, k, proj3, lam_params, subln, lambda_init):
    b, s, _ = q.shape
    tq = _tile(s, ATTN_TQ)
    dv = 2 * DIFF_HD
    return pl.pallas_call(
        functools.partial(_diff_attn_kernel, sub=min(tq, ATTN_SUB), lambda_init=lambda_init),
        grid=(b, DIFF_HEADS, s // tq),
        in_specs=[
            pl.BlockSpec((None, tq, DIFF_HD), lambda bi, h, qi: (bi, qi, 2 * h)),
            pl.BlockSpec((None, tq, DIFF_HD), lambda bi, h, qi: (bi, qi, 2 * h + 1)),
            pl.BlockSpec((None, s, DIFF_HD), lambda bi, h, qi: (bi, 0, 2 * h)),
            pl.BlockSpec((None, s, DIFF_HD), lambda bi, h, qi: (bi, 0, 2 * h + 1)),
            pl.BlockSpec((None, s, dv), lambda bi, h, qi: (bi, 0, OFF_DV // dv + h)),
            pl.BlockSpec((None, tq, dv), lambda bi, h, qi: (bi, qi, OFF_GDIFF // dv + h)),
            pl.BlockSpec((4, DIFF_HD), lambda bi, h, qi: (0, 0)),
            pl.BlockSpec((1, dv), lambda bi, h, qi: (0, 0)),
        ],
        out_specs=pl.BlockSpec((None, tq, dv), lambda bi, h, qi: (bi, qi, h)),
        out_shape=jax.ShapeDtypeStruct((b, s, BRANCH_W), BF16),
        compiler_params=_cparams(("parallel", "parallel", "parallel")),
        name="diff_attn",
    )(q, q, k, k, proj3, proj3, lam_params, subln.reshape(1, dv))


def _ssd_conv_kernel(xs_ref, xsp_ref, xsn_ref, bc_ref, bcp_ref, bcn_ref, w_ref, b_ref,
                     oxs_ref, obc_ref, ext_ref, *, tm):
    i = pl.program_id(1)
    has_prev = i > 0
    has_next = i < pl.num_programs(1) - 1
    pad_l = SSD_CONV // 2

    def conv(cur_ref, prev_ref, next_ref, out_ref, c0, width):
        ext_ref[0:HALO, 0:width] = jnp.where(has_prev, prev_ref[...].astype(F32), 0.0)
        ext_ref[HALO:HALO + tm, 0:width] = cur_ref[...].astype(F32)
        ext_ref[HALO + tm:, 0:width] = jnp.where(has_next, next_ref[...].astype(F32), 0.0)
        acc = b_ref[:, c0:c0 + width]
        for j in range(SSD_CONV):
            off = HALO - pad_l + j
            acc = acc + ext_ref[off:off + tm, 0:width] * w_ref[j:j + 1, c0:c0 + width]
        out_ref[...] = _silu(acc).astype(BF16)

    conv(xs_ref, xsp_ref, xsn_ref, oxs_ref, 0, BRANCH_W)
    conv(bc_ref, bcp_ref, bcn_ref, obc_ref, BRANCH_W, SSD_CONV_DIM - BRANCH_W)


def _halo_specs(tm, s, width, col):
    r = tm // HALO
    last = s // HALO - 1
    cur = pl.BlockSpec((None, tm, width), lambda bi, i: (bi, i, col))
    prev = pl.BlockSpec((None, HALO, width), lambda bi, i: (bi, jnp.maximum(i * r - 1, 0), col))
    nxt = pl.BlockSpec((None, HALO, width), lambda bi, i: (bi, jnp.minimum((i + 1) * r, last), col))
    return [cur, prev, nxt]


def _ssd_conv(proj3, conv_w, conv_b):
    b, s, _ = proj3.shape
    tm = _tile(s, 512)
    wbc = SSD_CONV_DIM - BRANCH_W
    return pl.pallas_call(
        functools.partial(_ssd_conv_kernel, tm=tm),
        grid=(b, s // tm),
        in_specs=(_halo_specs(tm, s, BRANCH_W, OFF_XS // BRANCH_W) + _halo_specs(tm, s, wbc, OFF_BC // wbc)
                  + [pl.BlockSpec((SSD_CONV, SSD_CONV_DIM), lambda bi, i: (0, 0)),
                     pl.BlockSpec((1, SSD_CONV_DIM), lambda bi, i: (0, 0))]),
        out_specs=[pl.BlockSpec((None, tm, BRANCH_W), lambda bi, i: (bi, i, 0)),
                   pl.BlockSpec((None, tm, wbc), lambda bi, i: (bi, i, 0))],
        out_shape=[jax.ShapeDtypeStruct((b, s, BRANCH_W), BF16), jax.ShapeDtypeStruct((b, s, wbc), BF16)],
        scratch_shapes=[pltpu.VMEM((tm + 2 * HALO, BRANCH_W), F32)],
        compiler_params=_cparams(("parallel", "parallel")),
        name="ssd_conv",
    )(proj3, proj3, proj3, proj3, proj3, proj3, conv_w, conv_b.reshape(1, -1))


def _ssd_chunk(xs_b, bc, dt_raw, bias, alog, st_ref, reverse):
    ln = SSD_CHUNK
    gw = BRANCH_W // SSD_G
    xs = xs_b.astype(F32)
    dt = _softplus(dt_raw + bias)
    da = dt * (-jnp.exp(alog))
    ri = lax.broadcasted_iota(jnp.int32, (ln, ln), 0)
    ci = lax.broadcasted_iota(jnp.int32, (ln, ln), 1)
    tri = (ci >= ri) if reverse else (ri >= ci)
    cs = jnp.dot(tri.astype(F32), da, precision=HIGHEST, preferred_element_type=F32)
    cs_end = cs[0:1, :] if reverse else cs[ln - 1:ln, :]
    er = lax.broadcasted_iota(jnp.int32, (2 * LANE, BRANCH_W), 0)
    ec = lax.broadcasted_iota(jnp.int32, (2 * LANE, BRANCH_W), 1)
    expand = jnp.where(ec // SSD_P == er % LANE, 1.0, 0.0).astype(BF16)
    cols = jnp.concatenate([dt, jnp.exp(cs), jnp.exp(cs_end - cs)], axis=0)
    hi = cols.astype(BF16)
    lo = (cols - hi.astype(F32)).astype(BF16)
    wide = jnp.dot(jnp.concatenate([hi, lo], axis=1), expand, preferred_element_type=F32)
    dt_e = wide[0:ln]
    dec_out_e = wide[ln:2 * ln]
    dec_st_e = wide[2 * ln:3 * ln]
    tot_e = dec_out_e[0:1, :] if reverse else dec_out_e[ln - 1:ln, :]
    cs_t = cs.T
    xd = xs * dt_e
    xdb = xd.astype(BF16)
    xst = (xd * dec_st_e).astype(BF16)
    low = lax.broadcasted_iota(jnp.int32, (ln, LANE), 1) < SSD_P
    zero = jnp.zeros((ln, LANE), BF16)
    outs = []
    for g in range(SSD_G):
        bg = bc[:, g * SSD_N:(g + 1) * SSD_N]
        cg = bc[:, (SSD_G + g) * SSD_N:(SSD_G + g + 1) * SSD_N]
        cb = lax.dot_general(cg, bg, (((1,), (1,)), ((), ())), preferred_element_type=F32)
        st = st_ref[g]
        y_off = jnp.dot(cg, st.astype(BF16), preferred_element_type=F32) * dec_out_e[:, g * gw:(g + 1) * gw]
        bt = bg.astype(F32).T.astype(BF16)
        st_ref[g] = (st * tot_e[:, g * gw:(g + 1) * gw]
                     + jnp.dot(bt, xst[:, g * gw:(g + 1) * gw], preferred_element_type=F32))
        for kp in range(gw // LANE):
            mats = []
            for hh in range(2):
                hd = g * (SSD_HEADS // SSD_G) + 2 * kp + hh
                seg = cs[:, hd:hd + 1] - cs_t[hd:hd + 1, :]
                lm = jnp.exp(jnp.where(tri, seg, -jnp.inf))
                mats.append((cb * lm).astype(BF16))
            lhs = jnp.concatenate(mats, axis=1)
            col = g * gw + kp * LANE
            xp = xdb[:, col:col + LANE]
            rhs = jnp.concatenate([jnp.where(low, xp, zero), jnp.where(low, zero, xp)], axis=0)
            y_diag = jnp.dot(lhs, rhs, preferred_element_type=F32)
            outs.append(y_diag + y_off[:, kp * LANE:(kp + 1) * LANE])
    return jnp.concatenate(outs, axis=1), xs


def _chunk_rows(n_rows, reverse):
    starts = range(0, n_rows, SSD_CHUNK)
    return [slice(r, r + SSD_CHUNK) for r in (reversed(starts) if reverse else starts)]


def _ssd_fwd_kernel(xs_ref, bc_ref, dt_ref, bias_ref, alog_ref, y_ref, st_ref):
    @pl.when(pl.program_id(1) == 0)
    def _():
        st_ref[...] = jnp.zeros_like(st_ref)

    for rows in _chunk_rows(xs_ref.shape[0], reverse=False):
        y, _ = _ssd_chunk(xs_ref[rows, :], bc_ref[rows, :], dt_ref[rows, :], bias_ref[...], alog_ref[...],
                          st_ref, reverse=False)
        y_ref[rows, :] = y


def _ssd_bwd_kernel(xs_ref, bc_ref, dt_ref, bias_ref, alog_ref, yf_ref, z_ref, d_ref, nw_ref,
                    o_ref, st_ref):
    @pl.when(pl.program_id(1) == 0)
    def _():
        st_ref[...] = jnp.zeros_like(st_ref)

    gw = BRANCH_W // SSD_G
    for rows in _chunk_rows(xs_ref.shape[0], reverse=True):
        yb, xs = _ssd_chunk(xs_ref[rows, :], bc_ref[rows, :], dt_ref[rows, :], bias_ref[...], alog_ref[...],
                            st_ref, reverse=True)
        y = yf_ref[rows, :] + yb + xs * d_ref[...]
        y = y * _silu(z_ref[rows, :].astype(F32))
        for g in range(SSD_G):
            yg = y[:, g * gw:(g + 1) * gw]
            yn = yg * lax.rsqrt(jnp.mean(yg * yg, axis=-1, keepdims=True) + EPS)
            o_ref[rows, g * gw:(g + 1) * gw] = (yn * nw_ref[:, g * gw:(g + 1) * gw]).astype(BF16)


def _ssd_scan(xs_act, bc_act, dt3, proj3, dt_bias, a_log, d_skip, norm_w):
    b, s, _ = xs_act.shape
    tb = _tile(s, SSD_BLOCK)
    nc = s // tb
    wbc = SSD_CONV_DIM - BRANCH_W
    gw = BRANCH_W // SSD_G
    pad = lambda v: jnp.pad(v.astype(F32), (0, LANE - SSD_HEADS)).reshape(1, LANE)
    scratch = [pltpu.VMEM((SSD_G, SSD_N, gw), F32)]
    vec = lambda w: pl.BlockSpec((1, w), lambda bi, c: (0, 0))

    def specs(cmap, d):
        return [pl.BlockSpec((None, tb,BRANCH_W), lambda bi, c: (bi, cmap(c), 0)),
                pl.BlockSpec((None, tb,wbc), lambda bi, c: (bi, cmap(c), 0)),
                pl.BlockSpec((None, tb,LANE), lambda bi, c: (bi, cmap(c), d)),
                vec(LANE), vec(LANE)]

    y_f = pl.pallas_call(
        _ssd_fwd_kernel,
        grid=(b, nc),
        in_specs=specs(lambda c: c, 0),
        out_specs=pl.BlockSpec((None, tb,BRANCH_W), lambda bi, c: (bi, c, 0)),
        out_shape=jax.ShapeDtypeStruct((b, s, BRANCH_W), F32),
        scratch_shapes=scratch,
        compiler_params=_cparams(("parallel", "arbitrary")),
        name="ssd_fwd",
    )(xs_act, bc_act, dt3, pad(dt_bias[0]), pad(a_log[0]))

    rev = lambda c: nc - 1 - c
    return pl.pallas_call(
        _ssd_bwd_kernel,
        grid=(b, nc),
        in_specs=specs(rev, 1) + [
            pl.BlockSpec((None, tb,BRANCH_W), lambda bi, c: (bi, rev(c), 0)),
            pl.BlockSpec((None, tb,BRANCH_W), lambda bi, c: (bi, rev(c), OFF_Z // BRANCH_W)),
            vec(BRANCH_W), vec(BRANCH_W)],
        out_specs=pl.BlockSpec((None, tb,BRANCH_W), lambda bi, c: (bi, rev(c), 0)),
        out_shape=jax.ShapeDtypeStruct((b, s, BRANCH_W), BF16),
        scratch_shapes=scratch,
        compiler_params=_cparams(("parallel", "arbitrary")),
        name="ssd_bwd",
    )(xs_act, bc_act, dt3, pad(dt_bias[1]), pad(a_log[1]), y_f, proj3,
      jnp.repeat(d_skip.astype(F32), SSD_P).reshape(1, BRANCH_W), norm_w.reshape(1, BRANCH_W))


def _pool_kernel(u_ref, up_ref, un_ref, g_ref, w_ref, sc_ref, o_ref, ext_ref, *, tm, s):
    i = pl.program_id(1)
    cur = u_ref[...].astype(F32)
    ext_ref[0:HALO, :] = jnp.where(i > 0, up_ref[...].astype(F32), 0.0)
    ext_ref[HALO:HALO + tm, :] = cur
    ext_ref[HALO + tm:, :] = jnp.where(i < pl.num_programs(1) - 1, un_ref[...].astype(F32), 0.0)
    pos = i * tm + lax.broadcasted_iota(jnp.int32, (tm, 1), 0)
    for gi, w in enumerate(POOL_WINDOWS):
        lo = w // 2
        hi = w - 1 - lo
        sl = slice(gi * POOL_GROUP, (gi + 1) * POOL_GROUP)
        acc = ext_ref[HALO - lo:HALO - lo + tm, sl]
        for d in range(-lo + 1, hi + 1):
            acc = acc + ext_ref[HALO + d:HALO + d + tm, sl]
        cnt = (jnp.minimum(pos + hi + 1, s) - jnp.maximum(pos - lo, 0)).astype(F32)
        pooled = acc / cnt - cur[:, sl]
        mixed = jnp.dot(pooled.astype(BF16), w_ref[gi], preferred_element_type=F32)
        o_ref[:, sl] = (mixed * sc_ref[:, sl] * _silu(g_ref[:, sl].astype(F32))).astype(BF16)


def _pool(proj3, pool_w, pool_scale):
    b, s, _ = proj3.shape
    tm = _tile(s, 512)
    return pl.pallas_call(
        functools.partial(_pool_kernel, tm=tm, s=s),
        grid=(b, s // tm),
        in_specs=_halo_specs(tm, s, BRANCH_W, OFF_U // BRANCH_W) + [
            pl.BlockSpec((None, tm, BRANCH_W), lambda bi, i: (bi, i, OFF_GPOOL // BRANCH_W)),
            pl.BlockSpec((len(POOL_WINDOWS), POOL_GROUP, POOL_GROUP), lambda bi, i: (0, 0, 0)),
            pl.BlockSpec((1, BRANCH_W), lambda bi, i: (0, 0))],
        out_specs=pl.BlockSpec((None, tm, BRANCH_W), lambda bi, i: (bi, i, 0)),
        out_shape=jax.ShapeDtypeStruct((b, s, BRANCH_W), BF16),
        scratch_shapes=[pltpu.VMEM((tm + 2 * HALO, BRANCH_W), F32)],
        compiler_params=_cparams(("parallel", "parallel")),
        name="pool",
    )(proj3, proj3, proj3, proj3, pool_w, pool_scale.reshape(1, BRANCH_W))


def _merge_kernel(*refs):
    brs, ws, gs, o_ref = refs[0:4], refs[4:8], refs[8:12], refs[12]
    acc = None
    for br, w, g in zip(brs, ws, gs):
        term = jax.nn.sigmoid(g[...].astype(F32)) * jnp.dot(br[...], w[...], preferred_element_type=F32)
        acc = term if acc is None else acc + term
    o_ref[...] = acc.astype(BF16)


def _merge(branches, w_branch, proj):
    t = proj.shape[0]
    tm = _tile(t, 1024)
    tn = 512
    br_spec = pl.BlockSpec((tm, BRANCH_W), lambda i, j: (i, 0))
    w_specs = [pl.BlockSpec((None, BRANCH_W, tn), lambda i, j, k=k: (k, 0, j)) for k in range(N_BRANCH)]
    g_specs = [pl.BlockSpec((tm, tn), lambda i, j, k=k: (i, (OFF_MG + k * D_MODEL) // tn + j))
               for k in range(N_BRANCH)]
    return pl.pallas_call(
        _merge_kernel,
        grid=(t // tm, D_MODEL // tn),
        in_specs=[br_spec] * N_BRANCH + w_specs + g_specs,
        out_specs=pl.BlockSpec((tm, tn), lambda i, j: (i, j)),
        out_shape=jax.ShapeDtypeStruct((t, D_MODEL), BF16),
        compiler_params=_cparams(("parallel", "parallel")),
        name="merge",
    )(*branches, *([w_branch] * N_BRANCH), *([proj] * N_BRANCH))


def _out_kernel(m_ref, w_ref, x_ref, fn_ref, o_ref, *, final):
    y = x_ref[...] + jnp.dot(m_ref[...], w_ref[...], preferred_element_type=F32)
    if final:
        y = y * lax.rsqrt(jnp.mean(y * y, axis=-1, keepdims=True) + EPS) * fn_ref[...]
    o_ref[...] = y


def _out_proj(merged, w_out, x2, final_norm, final):
    t = x2.shape[0]
    tm = _tile(t, 512)
    return pl.pallas_call(
        functools.partial(_out_kernel, final=final),
        grid=(t // tm,),
        in_specs=[pl.BlockSpec((tm, D_MODEL), lambda i: (i, 0)),
                  pl.BlockSpec((D_MODEL, D_MODEL), lambda i: (0, 0)),
                  pl.BlockSpec((tm, D_MODEL), lambda i: (i, 0)),
                  pl.BlockSpec((1, D_MODEL), lambda i: (0, 0))],
        out_specs=pl.BlockSpec((tm, D_MODEL), lambda i: (i, 0)),
        out_shape=jax.ShapeDtypeStruct((t, D_MODEL), F32),
        compiler_params=_cparams(("parallel",)),
        name="out_proj",
    )(merged, w_out, x2, final_norm.reshape(1, D_MODEL))


def _pack_layer(w_in, w_uq, w_ukv):
    w_in = w_in.astype(BF16)
    seg =[w_in[:, IN_OFFS[k]:IN_OFFS[k + 1]] for k in range(len(IN_SIZES))]
    cq, ckv, kr, gmla, dq, dk, dv, gdiff, z, xbc, dt, u, gpool, mg = seg
    zpad = lambda n: jnp.zeros((w_in.shape[0], n), w_in.dtype)
    half = MLA_ROPE // 2
    kr_main = jnp.concatenate([kr, zpad(LANE - MLA_ROPE)], axis=1)
    kr_part = jnp.concatenate([kr[:, half:], kr[:, :half], zpad(LANE - MLA_ROPE)], axis=1)
    w_main = jnp.concatenate([gmla, dq, dk, dv, gdiff, z, xbc[:, :BRANCH_W], u, gpool, mg, cq,
                              xbc[:, BRANCH_W:], ckv, kr_main, kr_part], axis=1).astype(BF16)
    w_dt = jnp.concatenate([dt[:, :SSD_HEADS], zpad(LANE - SSD_HEADS),
                            dt[:, SSD_HEADS:], zpad(LANE - SSD_HEADS)], axis=1).astype(BF16)
    wq3 = w_uq.reshape(MLA_Q_LORA, MLA_HEADS, MLA_NOPE + MLA_ROPE)
    zq = jnp.zeros((MLA_Q_LORA, MLA_HEADS, LANE - MLA_ROPE), w_uq.dtype)
    wq = jnp.concatenate([wq3, zq], axis=2).reshape(MLA_Q_LORA, MLA_HEADS * MLA_QK).astype(BF16)
    wqp = jnp.concatenate([wq3[:, :, MLA_NOPE + half:], wq3[:, :, MLA_NOPE:MLA_NOPE + half], zq],
                          axis=2).reshape(MLA_Q_LORA, MLA_HEADS * LANE).astype(BF16)
    wkv3 = w_ukv.reshape(MLA_KV_LORA, MLA_HEADS, MLA_NOPE + MLA_V)
    wk = wkv3[:, :, :MLA_NOPE].reshape(MLA_KV_LORA, MLA_HEADS * MLA_NOPE).astype(BF16)
    wv = wkv3[:, :, MLA_NOPE:].reshape(MLA_KV_LORA, MLA_HEADS * MLA_V).astype(BF16)
    return w_main, w_dt, wq, wqp, wk, wv


def _rope_tables(s):
    def angles(rot):
        half = rot // 2
        inv_freq = jnp.power(ROPE_THETA, -jnp.arange(half, dtype=F32) * 2.0 / rot)
        ang = jnp.arange(s, dtype=F32)[:, None] * inv_freq[None, :]
        return jnp.cos(ang), jnp.sin(ang)

    c, sn = angles(MLA_ROPE)
    z = jnp.zeros((s, LANE - MLA_ROPE), F32)
    mla_cos = jnp.concatenate([c, c, z], axis=1)
    mla_sin = jnp.concatenate([-sn, sn, z], axis=1)
    c, sn = angles(DIFF_ROT)
    zh = jnp.zeros_like(sn)
    rest = LANE - DIFF_ROT
    d_cos = jnp.concatenate([c, c, jnp.ones((s, rest), F32)], axis=1)
    d_sa = jnp.concatenate([zh, sn, jnp.zeros((s, rest), F32)], axis=1)
    d_sb = jnp.concatenate([-sn, zh, jnp.zeros((s, rest), F32)], axis=1)
    return mla_cos, mla_sin, d_cos, d_sa, d_sb


def _trunk(x, packed, norm_w, mla_q_norm, mla_kv_norm, diff_lambda, diff_subln, ssd_conv_w, ssd_conv_b,
           ssd_dt_bias, ssd_a_log, ssd_d, ssd_norm, pool_w, pool_scale, w_branch, w_out, final_norm):
    b, s, _ = x.shape
    t = b * s
    mla_cos, mla_sin, d_cos, d_sa, d_sb = _rope_tables(s)
    x2 = x.reshape(t, D_MODEL)
    for l in range(DEPTH):
        w_main, w_dt, wq, wqp, wk, wv = packed[l]
        lambda_init = 0.8 - 0.6 * math.exp(-0.3 * l)
        proj, dt = _inproj(x2, norm_w[l], w_main, w_dt)
        proj3 = proj.reshape(b, s, NP_MAIN)
        q, k, v = _mla_prep(proj, s, mla_cos, mla_sin, mla_q_norm[l], mla_kv_norm[l], wq, wqp, wk, wv)
        br_mla = _mla_attn(q.reshape(b, s, -1), k.reshape(b, s, -1), v.reshape(b, s, -1), proj3)
        dq, dk = _diff_prep(proj, s, d_cos, d_sa, d_sb)
        br_diff = _diff_attn(dq.reshape(b, s, -1), dk.reshape(b, s, -1), proj3, diff_lambda[l],
                             diff_subln[l], lambda_init)
        xs_act, bc_act = _ssd_conv(proj3, ssd_conv_w[l], ssd_conv_b[l])
        br_ssd = _ssd_scan(xs_act, bc_act, dt.reshape(b, s, DT_W), proj3, ssd_dt_bias[l], ssd_a_log[l],
                           ssd_d[l], ssd_norm[l])
        br_pool = _pool(proj3, pool_w[l].astype(BF16), pool_scale[l])
        branches = [a.reshape(t, BRANCH_W) for a in (br_mla, br_diff, br_ssd, br_pool)]
        merged = _merge(branches, w_branch[l].astype(BF16), proj)
        x2 = _out_proj(merged, w_out[l].astype(BF16), x2, final_norm, final=(l == DEPTH - 1))
    return x2.reshape(b, s, D_MODEL)


def kernel(x_prompt, x_sample, norm_w, w_in, mla_q_norm, mla_w_uq, mla_kv_norm, mla_w_ukv, diff_lambda,
           diff_subln, ssd_conv_w, ssd_conv_b, ssd_dt_bias, ssd_a_log, ssd_d, ssd_norm, pool_w, pool_scale,
           w_branch, w_out, final_norm):
    packed = [_pack_layer(w_in[l], mla_w_uq[l], mla_w_ukv[l]) for l in range(DEPTH)]
    rest = (norm_w, mla_q_norm, mla_kv_norm, diff_lambda, diff_subln, ssd_conv_w, ssd_conv_b, ssd_dt_bias,
            ssd_a_log, ssd_d, ssd_norm, pool_w, pool_scale, w_branch, w_out, final_norm)
    return (_trunk(x_prompt, packed, *rest), _trunk(x_sample, packed, *rest))
```

```python
import functools
import math

import numpy as np
import jax
import jax.numpy as jnp
from jax import lax
from jax.experimental import pallas as pl
from jax.experimental.pallas import tpu as pltpu

F32 = jnp.float32
BF16 = jnp.bfloat16
HIGHEST = lax.Precision.HIGHEST

D_MODEL = 2048
DEPTH = 2
BRANCH_W = D_MODEL // 2
N_BRANCH = 4
ROPE_THETA = 500000.0
EPS = 1e-6
MLA_HEADS = 8
MLA_NOPE = 128
MLA_ROPE = 64
MLA_V = BRANCH_W // MLA_HEADS
MLA_Q_LORA = 512
MLA_KV_LORA = 256
DIFF_HEADS = 4
DIFF_HD = BRANCH_W // (2 * DIFF_HEADS)
DIFF_ROT = DIFF_HD // 4
SSD_P = 64
SSD_HEADS = BRANCH_W // SSD_P
SSD_N = 128
SSD_G = 2
SSD_CONV = 4
SSD_CHUNK = 128
SSD_BLOCK = 512
SSD_CONV_DIM = BRANCH_W + 2 * SSD_G * SSD_N
POOL_WINDOWS = (2, 4, 8, 16)
POOL_GROUP = BRANCH_W // 4
IN_SIZES = (MLA_Q_LORA, MLA_KV_LORA, MLA_ROPE, BRANCH_W, BRANCH_W, BRANCH_W, BRANCH_W, BRANCH_W,
            BRANCH_W, SSD_CONV_DIM, 2 * SSD_HEADS, BRANCH_W, BRANCH_W, N_BRANCH * D_MODEL)
IN_OFFS = tuple(int(v) for v in np.cumsum((0,) + IN_SIZES))

LANE = 128
HALO = 16
VMEM_LIMIT = 56 * 1024 * 1024

OFF_GMLA = 0
OFF_DQ = 1 * BRANCH_W
OFF_DK = 2 * BRANCH_W
OFF_DV = 3 * BRANCH_W
OFF_GDIFF = 4 * BRANCH_W
OFF_Z = 5 * BRANCH_W
OFF_XS = 6 * BRANCH_W
OFF_U = 7 * BRANCH_W
OFF_GPOOL = 8 * BRANCH_W
OFF_MG = 9 * BRANCH_W
OFF_CQ = OFF_MG + N_BRANCH * D_MODEL
OFF_BC = OFF_CQ + MLA_Q_LORA
OFF_CKV = OFF_BC + 2 * SSD_G * SSD_N
OFF_KR = OFF_CKV + MLA_KV_LORA
OFF_KRP = OFF_KR + LANE
NP_MAIN = OFF_KRP + LANE
DT_W = 2 * LANE

ATTN_TQ = 1024
ATTN_SUB = 256
MLA_QK = 2 * LANE
LOG2E = math.log2(math.e)


def _cparams(sem, vmem=VMEM_LIMIT):
    return pltpu.CompilerParams(dimension_semantics=sem, vmem_limit_bytes=vmem)


def _tile(n, pref):
    t = min(n, pref)
    while n % t:
        t //= 2
    return t


def _silu(x):
    return x * jax.nn.sigmoid(x)


def _softplus(x):
    return jnp.maximum(x, 0.0) + jnp.log1p(jnp.exp(-jnp.abs(x)))


def _inproj_kernel(x_ref, nw_ref, w_ref, wdt_ref, o_ref, dt_ref, h_ref):
    @pl.when(pl.program_id(1) == 0)
    def _():
        x = x_ref[...]
        y = x * lax.rsqrt(jnp.mean(x * x, axis=-1, keepdims=True) + EPS)
        hb = (y * nw_ref[...]).astype(BF16)
        h_ref[...] = hb
        dt_ref[...] = jnp.dot(hb, wdt_ref[...], preferred_element_type=F32)

    o_ref[...] = jnp.dot(h_ref[...], w_ref[...], preferred_element_type=F32).astype(BF16)


def _inproj(x2, norm_w, w_main, w_dt, l):
    t = x2.shape[0]
    tm = _tile(t, 1024)
    tn = 1024 if NP_MAIN % 1024 == 0 else 512
    return pl.pallas_call(
        _inproj_kernel,
        grid=(t // tm, NP_MAIN // tn),
        in_specs=[
            pl.BlockSpec((tm, D_MODEL), lambda i, j: (i, 0)),
            pl.BlockSpec((1, D_MODEL), lambda i, j: (0, 0)),
            pl.BlockSpec((None, D_MODEL, tn), lambda i, j: (l, 0, j)),
            pl.BlockSpec((None, D_MODEL, DT_W), lambda i, j: (l, 0, 0)),
        ],
        out_specs=[
            pl.BlockSpec((tm, tn), lambda i, j: (i, j)),
            pl.BlockSpec((tm, DT_W), lambda i, j: (i, 0)),
        ],
        out_shape=[jax.ShapeDtypeStruct((t, NP_MAIN), BF16), jax.ShapeDtypeStruct((t, DT_W), F32)],
        scratch_shapes=[pltpu.VMEM((tm, D_MODEL), BF16)],
        compiler_params=_cparams(("parallel", "arbitrary")),
        name="inproj",
    )(x2, norm_w.reshape(1, D_MODEL), w_main, w_dt)


def _mla_prep_kernel(cq_ref, ckv_ref, kr_ref, krp_ref, cos_ref, sin_ref, qn_ref, kvn_ref,
                     wq_ref, wqp_ref, wk_ref, wv_ref, q_ref, k_ref, v_ref):
    cos = cos_ref[...]
    sin = sin_ref[...]
    cq = cq_ref[...].astype(F32)
    ncq = (cq * lax.rsqrt(jnp.mean(cq * cq, axis=-1, keepdims=True) + EPS) * qn_ref[...]).astype(BF16)
    qm = jnp.dot(ncq, wq_ref[...], preferred_element_type=F32)
    qp = jnp.dot(ncq, wqp_ref[...], preferred_element_type=F32)
    ckv = ckv_ref[...].astype(F32)
    nkv = (ckv * lax.rsqrt(jnp.mean(ckv * ckv, axis=-1, keepdims=True) + EPS) * kvn_ref[...]).astype(BF16)
    kn = jnp.dot(nkv, wk_ref[...], preferred_element_type=F32)
    v_ref[...] = jnp.dot(nkv, wv_ref[...], preferred_element_type=F32).astype(BF16)
    kpe = (kr_ref[...].astype(F32) * cos + krp_ref[...].astype(F32) * sin).astype(BF16)
    qscale = (MLA_NOPE + MLA_ROPE) ** -0.5 * LOG2E
    for h in range(MLA_HEADS):
        lo = h * MLA_QK
        q_ref[:, lo:lo + LANE] = (qm[:, lo:lo + LANE] * qscale).astype(BF16)
        qr = qm[:, lo + LANE:lo + 2 * LANE] * cos + qp[:, h * LANE:(h + 1) * LANE] * sin
        q_ref[:, lo + LANE:lo + 2 * LANE] = (qr * qscale).astype(BF16)
        k_ref[:, lo:lo + LANE] = kn[:, h * LANE:(h + 1) * LANE].astype(BF16)
        k_ref[:, lo + LANE:lo + 2 * LANE] = kpe


def _mla_prep(proj, s, cos_t, sin_t, q_norm, kv_norm, wq, wqp, wk, wv):
    t = proj.shape[0]
    tm = _tile(s, 512)
    ns = s // tm
    hq = MLA_HEADS * MLA_QK
    const = lambda shape: pl.BlockSpec(shape, lambda i: (0, 0))
    return pl.pallas_call(
        _mla_prep_kernel,
        grid=(t // tm,),
        in_specs=[
            pl.BlockSpec((tm, MLA_Q_LORA), lambda i: (i, OFF_CQ // MLA_Q_LORA)),
            pl.BlockSpec((tm, MLA_KV_LORA), lambda i: (i, OFF_CKV // MLA_KV_LORA)),
            pl.BlockSpec((tm, LANE), lambda i: (i, OFF_KR // LANE)),
            pl.BlockSpec((tm, LANE), lambda i: (i, OFF_KRP // LANE)),
            pl.BlockSpec((tm, LANE), lambda i: (i % ns, 0)),
            pl.BlockSpec((tm, LANE), lambda i: (i % ns, 0)),
            const((1, MLA_Q_LORA)),
            const((1, MLA_KV_LORA)),
            const((MLA_Q_LORA, hq)),
            const((MLA_Q_LORA, MLA_HEADS * LANE)),
            const((MLA_KV_LORA, MLA_HEADS * MLA_NOPE)),
            const((MLA_KV_LORA, MLA_HEADS * MLA_V)),
        ],
        out_specs=[
            pl.BlockSpec((tm, hq), lambda i: (i, 0)),
            pl.BlockSpec((tm, hq), lambda i: (i, 0)),
            pl.BlockSpec((tm, MLA_HEADS * MLA_V), lambda i: (i, 0)),
        ],
        out_shape=[jax.ShapeDtypeStruct((t, hq), BF16), jax.ShapeDtypeStruct((t, hq), BF16),
                   jax.ShapeDtypeStruct((t, MLA_HEADS * MLA_V), BF16)],
        compiler_params=_cparams(("parallel",)),
        name="mla_prep",
    )(proj, proj, proj, proj, cos_t, sin_t, q_norm.reshape(1, -1), kv_norm.reshape(1, -1), wq, wqp, wk, wv)


def _softmax_pv(q, k_ref, v_ref):
    sc = lax.dot_general(q, k_ref[...], (((1,), (1,)), ((), ())), preferred_element_type=F32)
    m = jnp.max(sc, axis=-1, keepdims=True)
    p = jnp.exp2(sc - m)
    l = jnp.sum(p, axis=-1, keepdims=True)
    acc = jnp.dot(p.astype(BF16), v_ref[...], preferred_element_type=F32)
    return acc, l


def _mla_attn_kernel(q_ref, k_ref, v_ref, g_ref, o_ref, *, sub):
    for r in range(0, q_ref.shape[0], sub):
        acc, l = _softmax_pv(q_ref[r:r + sub, :], k_ref, v_ref)
        o_ref[r:r + sub, :] = ((acc / l) * _silu(g_ref[r:r + sub, :].astype(F32))).astype(BF16)


def _mla_attn(q, k, v, proj3):
    b, s, _ = q.shape
    tq = _tile(s, ATTN_TQ)
    return pl.pallas_call(
        functools.partial(_mla_attn_kernel, sub=min(tq, ATTN_SUB)),
        grid=(b, MLA_HEADS, s // tq),
        in_specs=[
            pl.BlockSpec((None, tq, MLA_QK), lambda bi, h, qi: (bi, qi, h)),
            pl.BlockSpec((None, s, MLA_QK), lambda bi, h, qi: (bi, 0, h)),
            pl.BlockSpec((None, s, MLA_V), lambda bi, h, qi: (bi, 0, h)),
            pl.BlockSpec((None, tq, MLA_V), lambda bi, h, qi: (bi, qi, OFF_GMLA // MLA_V + h)),
        ],
        out_specs=pl.BlockSpec((None, tq, MLA_V), lambda bi, h, qi: (bi, qi, h)),
        out_shape=jax.ShapeDtypeStruct((b, s, BRANCH_W), BF16),
        compiler_params=_cparams(("parallel", "parallel", "parallel")),
        name="mla_attn",
    )(q, k, v, proj3)


def _diff_prep_kernel(dq_ref, dk_ref, cos_ref, sa_ref, sb_ref, q_ref, k_ref):
    cos = cos_ref[...]
    sa = sa_ref[...]
    sb = sb_ref[...]
    half = DIFF_ROT // 2
    qscale = DIFF_HD ** -0.5 * LOG2E
    for h in range(2 * DIFF_HEADS):
        sl = slice(h * DIFF_HD, (h + 1) * DIFF_HD)
        xq = dq_ref[:, sl].astype(F32)
        xk = dk_ref[:, sl].astype(F32)
        rq = xq * cos + pltpu.roll(xq, half, 1) * sa + pltpu.roll(xq, DIFF_HD - half, 1) * sb
        rk = xk * cos + pltpu.roll(xk, half, 1) * sa + pltpu.roll(xk, DIFF_HD - half, 1) * sb
        q_ref[:, sl] = (rq * qscale).astype(BF16)
        k_ref[:, sl] = rk.astype(BF16)


def _diff_prep(proj, s, cos_t, sa_t, sb_t):
    t = proj.shape[0]
    tm = _tile(s, 512)
    ns = s // tm
    tab = pl.BlockSpec((tm, LANE), lambda i: (i % ns, 0))
    return pl.pallas_call(
        _diff_prep_kernel,
        grid=(t // tm,),
        in_specs=[
            pl.BlockSpec((tm, BRANCH_W), lambda i: (i, OFF_DQ // BRANCH_W)),
            pl.BlockSpec((tm, BRANCH_W), lambda i: (i, OFF_DK // BRANCH_W)),
            tab, tab, tab,
        ],
        out_specs=[pl.BlockSpec((tm, BRANCH_W), lambda i: (i, 0))] * 2,
        out_shape=[jax.ShapeDtypeStruct((t, BRANCH_W), BF16)] * 2,
        compiler_params=_cparams(("parallel",)),
        name="diff_prep",
    )(proj, proj, cos_t, sa_t, sb_t)


def _diff_attn_kernel(q1_ref, q2_ref, k1_ref, k2_ref, v_ref, g_ref, lam_ref, sub_ref, o_ref, *,
                      sub, lambda_init):
    lp = lam_ref[...]
    lam = (jnp.exp(jnp.sum(lp[0:1] * lp[1:2], axis=-1, keepdims=True))
           - jnp.exp(jnp.sum(lp[2:3] * lp[3:4], axis=-1, keepdims=True)) + lambda_init)
    for r in range(0, q1_ref.shape[0], sub):
        acc1, l1 = _softmax_pv(q1_ref[r:r + sub, :], k1_ref, v_ref)
        acc2, l2 = _softmax_pv(q2_ref[r:r + sub, :], k2_ref, v_ref)
        o = acc1 / l1 - lam * (acc2 / l2)
        o = o * lax.rsqrt(jnp.mean(o * o, axis=-1, keepdims=True) + EPS) * sub_ref[...]
        o = o * (1.0 - lambda_init)
        o_ref[r:r + sub, :] = (o * _silu(g_ref[r:r + sub, :].astype(F32))).astype(BF16)


def _diff_attn(q, k, proj3, lam_params, subln, lambda_init):
    b, s, _ = q.shape
    tq = _tile(s, ATTN_TQ)
    dv = 2 * DIFF_HD
    return pl.pallas_call(
        functools.partial(_diff_attn_kernel, sub=min(tq, ATTN_SUB), lambda_init=lambda_init),
        grid=(b, DIFF_HEADS, s // tq),
        in_specs=[
            pl.BlockSpec((None, tq, DIFF_HD), lambda bi, h, qi: (bi, qi, 2 * h)),
            pl.BlockSpec((None, tq, DIFF_HD), lambda bi, h, qi: (bi, qi, 2 * h + 1)),
            pl.BlockSpec((None, s, DIFF_HD), lambda bi, h, qi: (bi, 0, 2 * h)),
            pl.BlockSpec((None, s, DIFF_HD), lambda bi, h, qi: (bi, 0, 2 * h + 1)),
            pl.BlockSpec((None, s, dv), lambda bi, h, qi: (bi, 0, OFF_DV // dv + h)),
            pl.BlockSpec((None, tq, dv), lambda bi, h, qi: (bi, qi, OFF_GDIFF // dv + h)),
            pl.BlockSpec((4, DIFF_HD), lambda bi, h, qi: (0, 0)),
            pl.BlockSpec((1, dv), lambda bi, h, qi: (0, 0)),
        ],
        out_specs=pl.BlockSpec((None, tq, dv), lambda bi, h, qi: (bi, qi, h)),
        out_shape=jax.ShapeDtypeStruct((b, s, BRANCH_W), BF16),
        compiler_params=_cparams(("parallel", "parallel", "parallel")),
        name="diff_attn",
    )(q, q, k, k, proj3, proj3, lam_params, subln.reshape(1, dv))


def _ssd_conv_kernel(xs_ref, xsp_ref, xsn_ref, bc_ref, bcp_ref, bcn_ref, w_ref, b_ref,
                     oxs_ref, obc_ref, ext_ref, *, tm):
    i = pl.program_id(1)
    has_prev = i > 0
    has_next = i < pl.num_programs(1) - 1
    pad_l = SSD_CONV // 2

    def conv(cur_ref, prev_ref, next_ref, out_ref, c0, width):
        ext_ref[0:HALO, 0:width] = jnp.where(has_prev, prev_ref[...].astype(F32), 0.0)
        ext_ref[HALO:HALO + tm, 0:width] = cur_ref[...].astype(F32)
        ext_ref[HALO + tm:, 0:width] = jnp.where(has_next, next_ref[...].astype(F32), 0.0)
        acc = b_ref[:, c0:c0 + width]
        for j in range(SSD_CONV):
            off = HALO - pad_l + j
            acc = acc + ext_ref[off:off + tm, 0:width] * w_ref[j:j + 1, c0:c0 + width]
        out_ref[...] = _silu(acc).astype(BF16)

    conv(xs_ref, xsp_ref, xsn_ref, oxs_ref, 0, BRANCH_W)
    conv(bc_ref, bcp_ref, bcn_ref, obc_ref, BRANCH_W, SSD_CONV_DIM - BRANCH_W)


def _halo_specs(tm, s, width, col):
    r = tm // HALO
    last = s // HALO - 1
    cur = pl.BlockSpec((None, tm, width), lambda bi, i: (bi, i, col))
    prev = pl.BlockSpec((None, HALO, width), lambda bi, i: (bi, jnp.maximum(i * r - 1, 0), col))
    nxt = pl.BlockSpec((None, HALO, width), lambda bi, i: (bi, jnp.minimum((i + 1) * r, last), col))
    return [cur, prev, nxt]


def _ssd_conv(proj3, conv_w, conv_b):
    b, s, _ = proj3.shape
    tm = _tile(s, 512)
    wbc = SSD_CONV_DIM - BRANCH_W
    return pl.pallas_call(
        functools.partial(_ssd_conv_kernel, tm=tm),
        grid=(b, s // tm),
        in_specs=(_halo_specs(tm, s, BRANCH_W, OFF_XS // BRANCH_W) + _halo_specs(tm, s, wbc, OFF_BC // wbc)
                  + [pl.BlockSpec((SSD_CONV, SSD_CONV_DIM), lambda bi, i: (0, 0)),
                     pl.BlockSpec((1, SSD_CONV_DIM), lambda bi, i: (0, 0))]),
        out_specs=[pl.BlockSpec((None, tm, BRANCH_W), lambda bi, i: (bi, i, 0)),
                   pl.BlockSpec((None, tm, wbc), lambda bi, i: (bi, i, 0))],
        out_shape=[jax.ShapeDtypeStruct((b, s, BRANCH_W), BF16), jax.ShapeDtypeStruct((b, s, wbc), BF16)],
        scratch_shapes=[pltpu.VMEM((tm + 2 * HALO, BRANCH_W), F32)],
        compiler_params=_cparams(("parallel", "parallel")),
        name="ssd_conv",
    )(proj3, proj3, proj3, proj3, proj3, proj3, conv_w, conv_b.reshape(1, -1))


def _ssd_chunk(xs_b, bc, dt_raw, bias, alog, st_ref, reverse):
    ln = SSD_CHUNK
    gw = BRANCH_W // SSD_G
    xs = xs_b.astype(F32)
    dt = _softplus(dt_raw + bias)
    da = dt * (-jnp.exp(alog))
    ri = lax.broadcasted_iota(jnp.int32, (ln, ln), 0)
    ci = lax.broadcasted_iota(jnp.int32, (ln, ln), 1)
    tri = (ci >= ri) if reverse else (ri >= ci)
    cs = jnp.dot(tri.astype(F32), da, precision=HIGHEST, preferred_element_type=F32)
    cs_end = cs[0:1, :] if reverse else cs[ln - 1:ln, :]
    er = lax.broadcasted_iota(jnp.int32, (2 * LANE, BRANCH_W), 0)
    ec = lax.broadcasted_iota(jnp.int32, (2 * LANE, BRANCH_W), 1)
    expand = jnp.where(ec // SSD_P == er % LANE, 1.0, 0.0).astype(BF16)
    cols = jnp.concatenate([dt, jnp.exp(cs), jnp.exp(cs_end - cs)], axis=0)
    hi = cols.astype(BF16)
    lo = (cols - hi.astype(F32)).astype(BF16)
    wide = jnp.dot(jnp.concatenate([hi, lo], axis=1), expand, preferred_element_type=F32)
    dt_e = wide[0:ln]
    dec_out_e = wide[ln:2 * ln]
    dec_st_e = wide[2 * ln:3 * ln]
    tot_e = dec_out_e[0:1, :] if reverse else dec_out_e[ln - 1:ln, :]
    cs_t = cs.T
    xd = xs * dt_e
    xdb = xd.astype(BF16)
    xst = (xd * dec_st_e).astype(BF16)
    low = lax.broadcasted_iota(jnp.int32, (ln, LANE), 1) < SSD_P
    zero = jnp.zeros((ln, LANE), BF16)
    outs = []
    for g in range(SSD_G):
        bg = bc[:, g * SSD_N:(g + 1) * SSD_N]
        cg = bc[:, (SSD_G + g) * SSD_N:(SSD_G + g + 1) * SSD_N]
        cb = lax.dot_general(cg, bg, (((1,), (1,)), ((), ())), preferred_element_type=F32)
        st = st_ref[g]
        y_off = jnp.dot(cg, st.astype(BF16), preferred_element_type=F32) * dec_out_e[:, g * gw:(g + 1) * gw]
        bt = bg.astype(F32).T.astype(BF16)
        st_ref[g] = (st * tot_e[:, g * gw:(g + 1) * gw]
                     + jnp.dot(bt, xst[:, g * gw:(g + 1) * gw], preferred_element_type=F32))
        for kp in range(gw // LANE):
            mats = []
            for hh in range(2):
                hd = g * (SSD_HEADS // SSD_G) + 2 * kp + hh
                seg = cs[:, hd:hd + 1] - cs_t[hd:hd + 1, :]
                lm = jnp.exp(jnp.where(tri, seg, -jnp.inf))
                mats.append((cb * lm).astype(BF16))
            lhs = jnp.concatenate(mats, axis=1)
            col = g * gw + kp * LANE
            xp = xdb[:, col:col + LANE]
            rhs = jnp.concatenate([jnp.where(low, xp, zero), jnp.where(low, zero, xp)], axis=0)
            y_diag = jnp.dot(lhs, rhs, preferred_element_type=F32)
            outs.append(y_diag + y_off[:, kp * LANE:(kp + 1) * LANE])
    return jnp.concatenate(outs, axis=1), xs


def _chunk_rows(n_rows, reverse):
    starts = range(0, n_rows, SSD_CHUNK)
    return [slice(r, r + SSD_CHUNK) for r in (reversed(starts) if reverse else starts)]


def _ssd_fwd_kernel(xs_ref, bc_ref, dt_ref, bias_ref, alog_ref, y_ref, st_ref):
    @pl.when(pl.program_id(1) == 0)
    def _():
        st_ref[...] = jnp.zeros_like(st_ref)

    for rows in _chunk_rows(xs_ref.shape[0], reverse=False):
        y, _ = _ssd_chunk(xs_ref[rows, :], bc_ref[rows, :], dt_ref[rows, :], bias_ref[...], alog_ref[...],
                          st_ref, reverse=False)
        y_ref[rows, :] = y


def _ssd_bwd_kernel(xs_ref, bc_ref, dt_ref, bias_ref, alog_ref, yf_ref, z_ref, d_ref, nw_ref,
                    o_ref, st_ref):
    @pl.when(pl.program_id(1) == 0)
    def _():
        st_ref[...] = jnp.zeros_like(st_ref)

    gw = BRANCH_W // SSD_G
    for rows in _chunk_rows(xs_ref.shape[0], reverse=True):
        yb, xs = _ssd_chunk(xs_ref[rows, :], bc_ref[rows, :], dt_ref[rows, :], bias_ref[...], alog_ref[...],
                            st_ref, reverse=True)
        y = yf_ref[rows, :] + yb + xs * d_ref[...]
        y = y * _silu(z_ref[rows, :].astype(F32))
        for g in range(SSD_G):
            yg = y[:, g * gw:(g + 1) * gw]
            yn = yg * lax.rsqrt(jnp.mean(yg * yg, axis=-1, keepdims=True) + EPS)
            o_ref[rows, g * gw:(g + 1) * gw] = (yn * nw_ref[:, g * gw:(g + 1) * gw]).astype(BF16)


def _ssd_scan(xs_act, bc_act, dt3, proj3, dt_bias, a_log, d_skip, norm_w):
    b, s, _ = xs_act.shape
    tb = _tile(s, SSD_BLOCK)
    nc = s // tb
    wbc = SSD_CONV_DIM - BRANCH_W
    gw = BRANCH_W // SSD_G
    pad = lambda v: jnp.pad(v.astype(F32), (0, LANE - SSD_HEADS)).reshape(1, LANE)
    scratch = [pltpu.VMEM((SSD_G, SSD_N, gw), F32)]
    vec = lambda w: pl.BlockSpec((1, w), lambda bi, c: (0, 0))

    def specs(cmap, d):
        return [pl.BlockSpec((None, tb,BRANCH_W), lambda bi, c: (bi, cmap(c), 0)),
                pl.BlockSpec((None, tb,wbc), lambda bi, c: (bi, cmap(c), 0)),
                pl.BlockSpec((None, tb,LANE), lambda bi, c: (bi, cmap(c), d)),
                vec(LANE), vec(LANE)]

    y_f = pl.pallas_call(
        _ssd_fwd_kernel,
        grid=(b, nc),
        in_specs=specs(lambda c: c, 0),
        out_specs=pl.BlockSpec((None, tb,BRANCH_W), lambda bi, c: (bi, c, 0)),
        out_shape=jax.ShapeDtypeStruct((b, s, BRANCH_W), F32),
        scratch_shapes=scratch,
        compiler_params=_cparams(("parallel", "arbitrary")),
        name="ssd_fwd",
    )(xs_act, bc_act, dt3, pad(dt_bias[0]), pad(a_log[0]))

    rev = lambda c: nc - 1 - c
    return pl.pallas_call(
        _ssd_bwd_kernel,
        grid=(b, nc),
        in_specs=specs(rev, 1) + [
            pl.BlockSpec((None, tb,BRANCH_W), lambda bi, c: (bi, rev(c), 0)),
            pl.BlockSpec((None, tb,BRANCH_W), lambda bi, c: (bi, rev(c), OFF_Z // BRANCH_W)),
            vec(BRANCH_W), vec(BRANCH_W)],
        out_specs=pl.BlockSpec((None, tb,BRANCH_W), lambda bi, c: (bi, rev(c), 0)),
        out_shape=jax.ShapeDtypeStruct((b, s, BRANCH_W), BF16),
        scratch_shapes=scratch,
        compiler_params=_cparams(("parallel", "arbitrary")),
        name="ssd_bwd",
    )(xs_act, bc_act, dt3, pad(dt_bias[1]), pad(a_log[1]), y_f, proj3,
      jnp.repeat(d_skip.astype(F32), SSD_P).reshape(1, BRANCH_W), norm_w.reshape(1, BRANCH_W))


def _pool_kernel(u_ref, up_ref, un_ref, g_ref, w_ref, sc_ref, o_ref, ext_ref, *, tm, s):
    i = pl.program_id(1)
    cur = u_ref[...].astype(F32)
    ext_ref[0:HALO, :] = jnp.where(i > 0, up_ref[...].astype(F32), 0.0)
    ext_ref[HALO:HALO + tm, :] = cur
    ext_ref[HALO + tm:, :] = jnp.where(i < pl.num_programs(1) - 1, un_ref[...].astype(F32), 0.0)
    pos = i * tm + lax.broadcasted_iota(jnp.int32, (tm, 1), 0)
    for gi, w in enumerate(POOL_WINDOWS):
        lo = w // 2
        hi = w - 1 - lo
        sl = slice(gi * POOL_GROUP, (gi + 1) * POOL_GROUP)
        acc = ext_ref[HALO - lo:HALO - lo + tm, sl]
        for d in range(-lo + 1, hi + 1):
            acc = acc + ext_ref[HALO + d:HALO + d + tm, sl]
        cnt = (jnp.minimum(pos + hi + 1, s) - jnp.maximum(pos - lo, 0)).astype(F32)
        pooled = acc / cnt - cur[:, sl]
        mixed = jnp.dot(pooled.astype(BF16), w_ref[gi], preferred_element_type=F32)
        o_ref[:, sl] = (mixed * sc_ref[:, sl] * _silu(g_ref[:, sl].astype(F32))).astype(BF16)


def _pool(proj3, pool_w, pool_scale):
    b, s, _ = proj3.shape
    tm = _tile(s, 512)
    return pl.pallas_call(
        functools.partial(_pool_kernel, tm=tm, s=s),
        grid=(b, s // tm),
        in_specs=_halo_specs(tm, s, BRANCH_W, OFF_U // BRANCH_W) + [
            pl.BlockSpec((None, tm, BRANCH_W), lambda bi, i: (bi, i, OFF_GPOOL // BRANCH_W)),
            pl.BlockSpec((len(POOL_WINDOWS), POOL_GROUP, POOL_GROUP), lambda bi, i: (0, 0, 0)),
            pl.BlockSpec((1, BRANCH_W), lambda bi, i: (0, 0))],
        out_specs=pl.BlockSpec((None, tm, BRANCH_W), lambda bi, i: (bi, i, 0)),
        out_shape=jax.ShapeDtypeStruct((b, s, BRANCH_W), BF16),
        scratch_shapes=[pltpu.VMEM((tm + 2 * HALO, BRANCH_W), F32)],
        compiler_params=_cparams(("parallel", "parallel")),
        name="pool",
    )(proj3, proj3, proj3, proj3, pool_w, pool_scale.reshape(1, BRANCH_W))


def _merge_kernel(*refs):
    brs, ws, gs, o_ref = refs[0:4], refs[4:8], refs[8:12], refs[12]
    acc = None
    for br, w, g in zip(brs, ws, gs):
        term = jax.nn.sigmoid(g[...].astype(F32)) * jnp.dot(br[...], w[...], preferred_element_type=F32)
        acc = term if acc is None else acc + term
    o_ref[...] = acc.astype(BF16)


def _merge(branches, w_branch, proj):
    t = proj.shape[0]
    tm = _tile(t, 1024)
    tn = 512
    br_spec = pl.BlockSpec((tm, BRANCH_W), lambda i, j: (i, 0))
    w_specs = [pl.BlockSpec((None, BRANCH_W, tn), lambda i, j, k=k: (k, 0, j)) for k in range(N_BRANCH)]
    g_specs = [pl.BlockSpec((tm, tn), lambda i, j, k=k: (i, (OFF_MG + k * D_MODEL) // tn + j))
               for k in range(N_BRANCH)]
    return pl.pallas_call(
        _merge_kernel,
        grid=(t // tm, D_MODEL // tn),
        in_specs=[br_spec] * N_BRANCH + w_specs + g_specs,
        out_specs=pl.BlockSpec((tm, tn), lambda i, j: (i, j)),
        out_shape=jax.ShapeDtypeStruct((t, D_MODEL), BF16),
        compiler_params=_cparams(("parallel", "parallel")),
        name="merge",
    )(*branches, *([w_branch] * N_BRANCH), *([proj] * N_BRANCH))


def _out_kernel(m_ref, w_ref, x_ref, fn_ref, o_ref, *, final):
    y = x_ref[...] + jnp.dot(m_ref[...], w_ref[...], preferred_element_type=F32)
    if final:
        y = y * lax.rsqrt(jnp.mean(y * y, axis=-1, keepdims=True) + EPS) * fn_ref[...]
    o_ref[...] = y


def _out_proj(merged, w_out, x2, final_norm, final):
    t = x2.shape[0]
    tm = _tile(t, 512)
    return pl.pallas_call(
        functools.partial(_out_kernel, final=final),
        grid=(t // tm,),
        in_specs=[pl.BlockSpec((tm, D_MODEL), lambda i: (i, 0)),
                  pl.BlockSpec((D_MODEL, D_MODEL), lambda i: (0, 0)),
                  pl.BlockSpec((tm, D_MODEL), lambda i: (i, 0)),
                  pl.BlockSpec((1, D_MODEL), lambda i: (0, 0))],
        out_specs=pl.BlockSpec((tm, D_MODEL), lambda i: (i, 0)),
        out_shape=jax.ShapeDtypeStruct((t, D_MODEL), F32),
        compiler_params=_cparams(("parallel",)),
        name="out_proj",
    )(merged, w_out, x2, final_norm.reshape(1, D_MODEL))


PACK_ROWS = 128
PACK_COLS = 1024

_PACK_MOVES = (
    (IN_OFFS[3], OFF_GMLA, IN_OFFS[9] + BRANCH_W - IN_OFFS[3]),
    (IN_OFFS[11], OFF_U, IN_OFFS[14] - IN_OFFS[11]),
    (IN_OFFS[0], OFF_CQ, MLA_Q_LORA),
    (IN_OFFS[9] + BRANCH_W, OFF_BC, SSD_CONV_DIM - BRANCH_W),
    (IN_OFFS[1], OFF_CKV, MLA_KV_LORA),
)


def _pack_kernel(w_ref, o_ref, odt_ref):
    for src, dst, width in _PACK_MOVES:
        for c in range(0, width, PACK_COLS):
            n = min(PACK_COLS, width - c)
            lo = (src + c) // LANE * LANE
            hi = min(-(-(src + c + n) // LANE) * LANE, IN_OFFS[-1])
            win = w_ref[:, lo:hi]
            o_ref[:, dst + c:dst + c + n] = win[:, src + c - lo:src + c - lo + n].astype(BF16)
    lane = lax.broadcasted_iota(jnp.int32, (w_ref.shape[0], LANE), 1)
    half = MLA_ROPE // 2
    kr = w_ref[:, IN_OFFS[2]:IN_OFFS[2] + LANE]
    o_ref[:, OFF_KR:OFF_KR + LANE] = jnp.where(lane < MLA_ROPE, kr, 0.0).astype(BF16)
    part = jnp.where(lane < half, pltpu.roll(kr, LANE - half, 1),
                     jnp.where(lane < MLA_ROPE, pltpu.roll(kr, half, 1), 0.0))
    o_ref[:, OFF_KRP:OFF_KRP + LANE] = part.astype(BF16)
    dlo = IN_OFFS[10] // LANE * LANE
    dwin = w_ref[:, dlo:dlo + LANE]
    off = IN_OFFS[10] - dlo
    for d in range(2):
        shifted = pltpu.roll(dwin, (LANE - off - d * SSD_HEADS) % LANE, 1)
        odt_ref[:, d * LANE:(d + 1) * LANE] = jnp.where(lane < SSD_HEADS, shifted, 0.0).astype(BF16)


def _pack_w_in(w_in):
    depth, rows, in_dim = w_in.shape
    return pl.pallas_call(
        _pack_kernel,
        grid=(depth, rows // PACK_ROWS),
        in_specs=[pl.BlockSpec((None, PACK_ROWS, in_dim), lambda l, i: (l, i, 0))],
        out_specs=[pl.BlockSpec((None, PACK_ROWS, NP_MAIN), lambda l, i: (l, i, 0)),
                   pl.BlockSpec((None, PACK_ROWS, DT_W), lambda l, i: (l, i, 0))],
        out_shape=[jax.ShapeDtypeStruct((depth, rows, NP_MAIN), BF16),
                   jax.ShapeDtypeStruct((depth, rows, DT_W), BF16)],
        compiler_params=_cparams(("parallel", "parallel")),
        name="pack_w_in",
    )(w_in)


def _pack_mla(w_uq, w_ukv):
    half = MLA_ROPE // 2
    wq3 = w_uq.reshape(MLA_Q_LORA, MLA_HEADS, MLA_NOPE + MLA_ROPE)
    zq = jnp.zeros((MLA_Q_LORA, MLA_HEADS, LANE - MLA_ROPE), w_uq.dtype)
    wq = jnp.concatenate([wq3, zq], axis=2).reshape(MLA_Q_LORA, MLA_HEADS * MLA_QK).astype(BF16)
    wqp = jnp.concatenate([wq3[:, :, MLA_NOPE + half:], wq3[:, :, MLA_NOPE:MLA_NOPE + half], zq],
                          axis=2).reshape(MLA_Q_LORA, MLA_HEADS * LANE).astype(BF16)
    wkv3 = w_ukv.reshape(MLA_KV_LORA, MLA_HEADS, MLA_NOPE + MLA_V)
    wk = wkv3[:, :, :MLA_NOPE].reshape(MLA_KV_LORA, MLA_HEADS * MLA_NOPE).astype(BF16)
    wv = wkv3[:, :, MLA_NOPE:].reshape(MLA_KV_LORA, MLA_HEADS * MLA_V).astype(BF16)
    return wq, wqp, wk, wv


def _rope_tables(s):
    def angles(rot):
        half = rot // 2
        inv_freq = jnp.power(ROPE_THETA, -jnp.arange(half, dtype=F32) * 2.0 / rot)
        ang = jnp.arange(s, dtype=F32)[:, None] * inv_freq[None, :]
        return jnp.cos(ang), jnp.sin(ang)

    c, sn = angles(MLA_ROPE)
    z = jnp.zeros((s, LANE - MLA_ROPE), F32)
    mla_cos = jnp.concatenate([c, c, z], axis=1)
    mla_sin = jnp.concatenate([-sn, sn, z], axis=1)
    c, sn = angles(DIFF_ROT)
    zh = jnp.zeros_like(sn)
    rest = LANE - DIFF_ROT
    d_cos = jnp.concatenate([c, c, jnp.ones((s, rest), F32)], axis=1)
    d_sa = jnp.concatenate([zh, sn, jnp.zeros((s, rest), F32)], axis=1)
    d_sb = jnp.concatenate([-sn, zh, jnp.zeros((s, rest), F32)], axis=1)
    return mla_cos, mla_sin, d_cos, d_sa, d_sb


def _trunk(x, w_main, w_dt, packed, norm_w, mla_q_norm, mla_kv_norm, diff_lambda, diff_subln, ssd_conv_w, ssd_conv_b,
           ssd_dt_bias, ssd_a_log, ssd_d, ssd_norm, pool_w, pool_scale, w_branch, w_out, final_norm):
    b, s, _ = x.shape
    t = b * s
    mla_cos, mla_sin, d_cos, d_sa, d_sb = _rope_tables(s)
    x2 = x.reshape(t, D_MODEL)
    for l in range(DEPTH):
        wq, wqp, wk, wv = packed[l]
        lambda_init = 0.8 - 0.6 * math.exp(-0.3 * l)
        proj, dt = _inproj(x2, norm_w[l], w_main, w_dt, l)
        proj3 = proj.reshape(b, s, NP_MAIN)
        q, k, v = _mla_prep(proj, s, mla_cos, mla_sin, mla_q_norm[l], mla_kv_norm[l], wq, wqp, wk, wv)
        br_mla = _mla_attn(q.reshape(b, s, -1), k.reshape(b, s, -1), v.reshape(b, s, -1), proj3)
        dq, dk = _diff_prep(proj, s, d_cos, d_sa, d_sb)
        br_diff = _diff_attn(dq.reshape(b, s, -1), dk.reshape(b, s, -1), proj3, diff_lambda[l],
                             diff_subln[l], lambda_init)
        xs_act, bc_act = _ssd_conv(proj3, ssd_conv_w[l], ssd_conv_b[l])
        br_ssd = _ssd_scan(xs_act, bc_act, dt.reshape(b, s, DT_W), proj3, ssd_dt_bias[l], ssd_a_log[l],
                           ssd_d[l], ssd_norm[l])
        br_pool = _pool(proj3, pool_w[l].astype(BF16), pool_scale[l])
        branches = [a.reshape(t, BRANCH_W) for a in (br_mla, br_diff, br_ssd, br_pool)]
        merged = _merge(branches, w_branch[l].astype(BF16), proj)
        x2 = _out_proj(merged, w_out[l].astype(BF16), x2, final_norm, final=(l == DEPTH - 1))
    return x2.reshape(b, s, D_MODEL)


def kernel(x_prompt, x_sample, norm_w, w_in, mla_q_norm, mla_w_uq, mla_kv_norm, mla_w_ukv, diff_lambda,
           diff_subln, ssd_conv_w, ssd_conv_b, ssd_dt_bias, ssd_a_log, ssd_d, ssd_norm, pool_w, pool_scale,
           w_branch, w_out, final_norm):
    w_main, w_dt = _pack_w_in(w_in)
    packed = [_pack_mla(mla_w_uq[l], mla_w_ukv[l]) for l in range(DEPTH)]
    rest = (norm_w, mla_q_norm, mla_kv_norm, diff_lambda, diff_subln, ssd_conv_w, ssd_conv_b, ssd_dt_bias,
            ssd_a_log, ssd_d, ssd_norm, pool_w, pool_scale, w_branch, w_out, final_norm)
    return (_trunk(x_prompt, w_main, w_dt, packed, *rest), _trunk(x_sample, w_main, w_dt, packed, *rest))
```

```python
import functools
import math

import numpy as np
import jax
import jax.numpy as jnp
from jax import lax
from jax.experimental import pallas as pl
from jax.experimental.pallas import tpu as pltpu

F32 = jnp.float32
BF16 = jnp.bfloat16
HIGHEST = lax.Precision.HIGHEST

D_MODEL = 2048
DEPTH = 2
BRANCH_W = D_MODEL // 2
N_BRANCH = 4
ROPE_THETA = 500000.0
EPS = 1e-6
MLA_HEADS = 8
MLA_NOPE = 128
MLA_ROPE = 64
MLA_V = BRANCH_W // MLA_HEADS
MLA_Q_LORA = 512
MLA_KV_LORA = 256
DIFF_HEADS = 4
DIFF_HD = BRANCH_W // (2 * DIFF_HEADS)
DIFF_ROT = DIFF_HD // 4
SSD_P = 64
SSD_HEADS = BRANCH_W // SSD_P
SSD_N = 128
SSD_G = 2
SSD_CONV = 4
SSD_CHUNK = 128
SSD_BLOCK = 512
SSD_CONV_DIM = BRANCH_W + 2 * SSD_G * SSD_N
POOL_WINDOWS = (2, 4, 8, 16)
POOL_GROUP = BRANCH_W // 4
IN_SIZES = (MLA_Q_LORA, MLA_KV_LORA, MLA_ROPE, BRANCH_W, BRANCH_W, BRANCH_W, BRANCH_W, BRANCH_W,
            BRANCH_W, SSD_CONV_DIM, 2 * SSD_HEADS, BRANCH_W, BRANCH_W, N_BRANCH * D_MODEL)
IN_OFFS = tuple(int(v) for v in np.cumsum((0,) + IN_SIZES))

LANE = 128
HALO = 16
VMEM_LIMIT = 56 * 1024 * 1024

OFF_GMLA = 0
OFF_DQ = 1 * BRANCH_W
OFF_DK = 2 * BRANCH_W
OFF_DV = 3 * BRANCH_W
OFF_GDIFF = 4 * BRANCH_W
OFF_Z = 5 * BRANCH_W
OFF_XS = 6 * BRANCH_W
OFF_U = 7 * BRANCH_W
OFF_GPOOL = 8 * BRANCH_W
OFF_MG = 9 * BRANCH_W
OFF_CQ = OFF_MG + N_BRANCH * D_MODEL
OFF_BC = OFF_CQ + MLA_Q_LORA
OFF_CKV = OFF_BC + 2 * SSD_G * SSD_N
OFF_KR = OFF_CKV + MLA_KV_LORA
OFF_KRP = OFF_KR + LANE
NP_MAIN = OFF_KRP + LANE
DT_W = 2 * LANE

ATTN_TQ = 1024
ATTN_SUB = 256
MLA_QK = 2 * LANE
LOG2E = math.log2(math.e)


def _cparams(sem, vmem=VMEM_LIMIT):
    return pltpu.CompilerParams(dimension_semantics=sem, vmem_limit_bytes=vmem)


def _tile(n, pref):
    t = min(n, pref)
    while n % t:
        t //= 2
    return t


def _silu(x):
    return x * jax.nn.sigmoid(x)


def _softplus(x):
    return jnp.maximum(x, 0.0) + jnp.log1p(jnp.exp(-jnp.abs(x)))


_NT = (((1,), (1,)), ((), ()))
INPROJ_TN = 512

_N_TILE_A = OFF_U // INPROJ_TN
_N_TILE_B = OFF_CQ // INPROJ_TN
_N_TILE_MAIN = OFF_CKV // INPROJ_TN


def _inproj_src_row(j):
    return jnp.where(j < _N_TILE_A, IN_OFFS[3] + j * INPROJ_TN,
                     jnp.where(j < _N_TILE_B, IN_OFFS[11] + (j - _N_TILE_A) * INPROJ_TN,
                               jnp.where(j == _N_TILE_B + 1, IN_OFFS[9] + BRANCH_W, IN_OFFS[0])))


def _inproj_kernel(x_ref, nw_ref, w_ref, wtail_ref, wdt_ref, o_ref, dt_ref, h_ref):
    j = pl.program_id(1)

    @pl.when(j == 0)
    def _():
        x = x_ref[...]
        y = x * lax.rsqrt(jnp.mean(x * x, axis=-1, keepdims=True) + EPS)
        hb = (y * nw_ref[...]).astype(BF16)
        h_ref[...] = hb
        dt_ref[...] = lax.dot_general(hb, wdt_ref[...], _NT, preferred_element_type=F32)

    @pl.when(j < _N_TILE_MAIN)
    def _():
        o_ref[...] = lax.dot_general(h_ref[...], w_ref[...], _NT, preferred_element_type=F32).astype(BF16)

    @pl.when(j == _N_TILE_MAIN)
    def _():
        o_ref[...] = lax.dot_general(h_ref[...], wtail_ref[...], _NT, preferred_element_type=F32).astype(BF16)


def _inproj(x2, norm_w, wt, w_tail, w_dt, l):
    t = x2.shape[0]
    tm = _tile(t, 1024)
    tn = INPROJ_TN
    return pl.pallas_call(
        _inproj_kernel,
        grid=(t // tm, NP_MAIN // tn),
        in_specs=[
            pl.BlockSpec((tm, D_MODEL), lambda i, j: (i, 0)),
            pl.BlockSpec((1, D_MODEL), lambda i, j: (0, 0)),
            pl.BlockSpec((pl.Element(tn), pl.Element(D_MODEL)),
                         lambda i, j: (pl.multiple_of(l * IN_OFFS[-1] + _inproj_src_row(j), HALO), 0)),
            pl.BlockSpec((None, tn, D_MODEL), lambda i, j: (l, 0, 0)),
            pl.BlockSpec((None, DT_W, D_MODEL), lambda i, j: (l, 0, 0)),
        ],
        out_specs=[
            pl.BlockSpec((tm, tn), lambda i, j: (i, j)),
            pl.BlockSpec((tm, DT_W), lambda i, j: (i, 0)),
        ],
        out_shape=[jax.ShapeDtypeStruct((t, NP_MAIN), BF16), jax.ShapeDtypeStruct((t, DT_W), F32)],
        scratch_shapes=[pltpu.VMEM((tm, D_MODEL), BF16)],
        compiler_params=_cparams(("parallel", "arbitrary")),
        name="inproj",
    )(x2, norm_w.reshape(1, D_MODEL), wt.reshape(-1, D_MODEL), w_tail, w_dt)


def _mla_prep_kernel(cq_ref, ckv_ref, kr_ref, krp_ref, cos_ref, sin_ref, qn_ref, kvn_ref,
                     wq_ref, wqp_ref, wk_ref, wv_ref, q_ref, k_ref, v_ref):
    cos = cos_ref[...]
    sin = sin_ref[...]
    cq = cq_ref[...].astype(F32)
    ncq = (cq * lax.rsqrt(jnp.mean(cq * cq, axis=-1, keepdims=True) + EPS) * qn_ref[...]).astype(BF16)
    qm = jnp.dot(ncq, wq_ref[...], preferred_element_type=F32)
    qp = jnp.dot(ncq, wqp_ref[...], preferred_element_type=F32)
    ckv = ckv_ref[...].astype(F32)
    nkv = (ckv * lax.rsqrt(jnp.mean(ckv * ckv, axis=-1, keepdims=True) + EPS) * kvn_ref[...]).astype(BF16)
    kn = jnp.dot(nkv, wk_ref[...], preferred_element_type=F32)
    v_ref[...] = jnp.dot(nkv, wv_ref[...], preferred_element_type=F32).astype(BF16)
    kpe = (kr_ref[...].astype(F32) * cos + krp_ref[...].astype(F32) * sin).astype(BF16)
    qscale = (MLA_NOPE + MLA_ROPE) ** -0.5 * LOG2E
    for h in range(MLA_HEADS):
        lo = h * MLA_QK
        q_ref[:, lo:lo + LANE] = (qm[:, lo:lo + LANE] * qscale).astype(BF16)
        qr = qm[:, lo + LANE:lo + 2 * LANE] * cos + qp[:, h * LANE:(h + 1) * LANE] * sin
        q_ref[:, lo + LANE:lo + 2 * LANE] = (qr * qscale).astype(BF16)
        k_ref[:, lo:lo + LANE] = kn[:, h * LANE:(h + 1) * LANE].astype(BF16)
        k_ref[:, lo + LANE:lo + 2 * LANE] = kpe


def _mla_prep(proj, s, cos_t, sin_t, q_norm, kv_norm, wq, wqp, wk, wv):
    t = proj.shape[0]
    tm = _tile(s, 512)
    ns = s // tm
    hq = MLA_HEADS * MLA_QK
    const = lambda shape: pl.BlockSpec(shape, lambda i: (0, 0))
    return pl.pallas_call(
        _mla_prep_kernel,
        grid=(t // tm,),
        in_specs=[
            pl.BlockSpec((tm, MLA_Q_LORA), lambda i: (i, OFF_CQ // MLA_Q_LORA)),
            pl.BlockSpec((tm, MLA_KV_LORA), lambda i: (i, OFF_CKV // MLA_KV_LORA)),
            pl.BlockSpec((tm, LANE), lambda i: (i, OFF_KR // LANE)),
            pl.BlockSpec((tm, LANE), lambda i: (i, OFF_KRP // LANE)),
            pl.BlockSpec((tm, LANE), lambda i: (i % ns, 0)),
            pl.BlockSpec((tm, LANE), lambda i: (i % ns, 0)),
            const((1, MLA_Q_LORA)),
            const((1, MLA_KV_LORA)),
            const((MLA_Q_LORA, hq)),
            const((MLA_Q_LORA, MLA_HEADS * LANE)),
            const((MLA_KV_LORA, MLA_HEADS * MLA_NOPE)),
            const((MLA_KV_LORA, MLA_HEADS * MLA_V)),
        ],
        out_specs=[
            pl.BlockSpec((tm, hq), lambda i: (i, 0)),
            pl.BlockSpec((tm, hq), lambda i: (i, 0)),
            pl.BlockSpec((tm, MLA_HEADS * MLA_V), lambda i: (i, 0)),
        ],
        out_shape=[jax.ShapeDtypeStruct((t, hq), BF16), jax.ShapeDtypeStruct((t, hq), BF16),
                   jax.ShapeDtypeStruct((t, MLA_HEADS * MLA_V), BF16)],
        compiler_params=_cparams(("parallel",)),
        name="mla_prep",
    )(proj, proj, proj, proj, cos_t, sin_t, q_norm.reshape(1, -1), kv_norm.reshape(1, -1), wq, wqp, wk, wv)


def _softmax_pv(q, k_ref, v_ref):
    sc = lax.dot_general(q, k_ref[...], (((1,), (1,)), ((), ())), preferred_element_type=F32)
    m = jnp.max(sc, axis=-1, keepdims=True)
    p = jnp.exp2(sc - m)
    l = jnp.sum(p, axis=-1, keepdims=True)
    acc = jnp.dot(p.astype(BF16), v_ref[...], preferred_element_type=F32)
    return acc, l


def _mla_attn_kernel(q_ref, k_ref, v_ref, g_ref, o_ref, *, sub):
    for r in range(0, q_ref.shape[0], sub):
        acc, l = _softmax_pv(q_ref[r:r + sub, :], k_ref, v_ref)
        o_ref[r:r + sub, :] = ((acc / l) * _silu(g_ref[r:r + sub, :].astype(F32))).astype(BF16)


def _mla_attn(q, k, v, proj3):
    b, s, _ = q.shape
    tq = _tile(s, ATTN_TQ)
    return pl.pallas_call(
        functools.partial(_mla_attn_kernel, sub=min(tq, ATTN_SUB)),
        grid=(b, MLA_HEADS, s // tq),
        in_specs=[
            pl.BlockSpec((None, tq, MLA_QK), lambda bi, h, qi: (bi, qi, h)),
            pl.BlockSpec((None, s, MLA_QK), lambda bi, h, qi: (bi, 0, h)),
            pl.BlockSpec((None, s, MLA_V), lambda bi, h, qi: (bi, 0, h)),
            pl.BlockSpec((None, tq, MLA_V), lambda bi, h, qi: (bi, qi, OFF_GMLA // MLA_V + h)),
        ],
        out_specs=pl.BlockSpec((None, tq, MLA_V), lambda bi, h, qi: (bi, qi, h)),
        out_shape=jax.ShapeDtypeStruct((b, s, BRANCH_W), BF16),
        compiler_params=_cparams(("parallel", "parallel", "parallel")),
        name="mla_attn",
    )(q, k, v, proj3)


def _diff_prep_kernel(dq_ref, dk_ref, cos_ref, sa_ref, sb_ref, q_ref, k_ref):
    cos = cos_ref[...]
    sa = sa_ref[...]
    sb = sb_ref[...]
    half = DIFF_ROT // 2
    qscale = DIFF_HD ** -0.5 * LOG2E
    for h in range(2 * DIFF_HEADS):
        sl = slice(h * DIFF_HD, (h + 1) * DIFF_HD)
        xq = dq_ref[:, sl].astype(F32)
        xk = dk_ref[:, sl].astype(F32)
        rq = xq * cos + pltpu.roll(xq, half, 1) * sa + pltpu.roll(xq, DIFF_HD - half, 1) * sb
        rk = xk * cos + pltpu.roll(xk, half, 1) * sa + pltpu.roll(xk, DIFF_HD - half, 1) * sb
        q_ref[:, sl] = (rq * qscale).astype(BF16)
        k_ref[:, sl] = rk.astype(BF16)


def _diff_prep(proj, s, cos_t, sa_t, sb_t):
    t = proj.shape[0]
    tm = _tile(s, 512)
    ns = s // tm
    tab = pl.BlockSpec((tm, LANE), lambda i: (i % ns, 0))
    return pl.pallas_call(
        _diff_prep_kernel,
        grid=(t // tm,),
        in_specs=[
            pl.BlockSpec((tm, BRANCH_W), lambda i: (i, OFF_DQ // BRANCH_W)),
            pl.BlockSpec((tm, BRANCH_W), lambda i: (i, OFF_DK // BRANCH_W)),
            tab, tab, tab,
        ],
        out_specs=[pl.BlockSpec((tm, BRANCH_W), lambda i: (i, 0))] * 2,
        out_shape=[jax.ShapeDtypeStruct((t, BRANCH_W), BF16)] * 2,
        compiler_params=_cparams(("parallel",)),
        name="diff_prep",
    )(proj, proj, cos_t, sa_t, sb_t)


def _diff_attn_kernel(q1_ref, q2_ref, k1_ref, k2_ref, v_ref, g_ref, lam_ref, sub_ref, o_ref, *,
                      sub, lambda_init):
    lp = lam_ref[...]
    lam = (jnp.exp(jnp.sum(lp[0:1] * lp[1:2], axis=-1, keepdims=True))
           - jnp.exp(jnp.sum(lp[2:3] * lp[3:4], axis=-1, keepdims=True)) + lambda_init)
    for r in range(0, q1_ref.shape[0], sub):
        acc1, l1 = _softmax_pv(q1_ref[r:r + sub, :], k1_ref, v_ref)
        acc2, l2 = _softmax_pv(q2_ref[r:r + sub, :], k2_ref, v_ref)
        o = acc1 / l1 - lam * (acc2 / l2)
        o = o * lax.rsqrt(jnp.mean(o * o, axis=-1, keepdims=True) + EPS) * sub_ref[...]
        o = o * (1.0 - lambda_init)
        o_ref[r:r + sub, :] = (o * _silu(g_ref[r:r + sub, :].astype(F32))).astype(BF16)


def _diff_attn(q, k, proj3, lam_params, subln, lambda_init):
    b, s, _ = q.shape
    tq = _tile(s, ATTN_TQ)
    dv = 2 * DIFF_HD
    return pl.pallas_call(
        functools.partial(_diff_attn_kernel, sub=min(tq, ATTN_SUB), lambda_init=lambda_init),
        grid=(b, DIFF_HEADS, s // tq),
        in_specs=[
            pl.BlockSpec((None, tq, DIFF_HD), lambda bi, h, qi: (bi, qi, 2 * h)),
            pl.BlockSpec((None, tq, DIFF_HD), lambda bi, h, qi: (bi, qi, 2 * h + 1)),
            pl.BlockSpec((None, s, DIFF_HD), lambda bi, h, qi: (bi, 0, 2 * h)),
            pl.BlockSpec((None, s, DIFF_HD), lambda bi, h, qi: (bi, 0, 2 * h + 1)),
            pl.BlockSpec((None, s, dv), lambda bi, h, qi: (bi, 0, OFF_DV // dv + h)),
            pl.BlockSpec((None, tq, dv), lambda bi, h, qi: (bi, qi, OFF_GDIFF // dv + h)),
            pl.BlockSpec((4, DIFF_HD), lambda bi, h, qi: (0, 0)),
            pl.BlockSpec((1, dv), lambda bi, h, qi: (0, 0)),
        ],
        out_specs=pl.BlockSpec((None, tq, dv), lambda bi, h, qi: (bi, qi, h)),
        out_shape=jax.ShapeDtypeStruct((b, s, BRANCH_W), BF16),
        compiler_params=_cparams(("parallel", "parallel", "parallel")),
        name="diff_attn",
    )(q, q, k, k, proj3, proj3, lam_params, subln.reshape(1, dv))


def _ssd_conv_kernel(xs_ref, xsp_ref, xsn_ref, bc_ref, bcp_ref, bcn_ref, w_ref, b_ref,
                     oxs_ref, obc_ref, ext_ref, *, tm):
    i = pl.program_id(1)
    has_prev = i > 0
    has_next = i < pl.num_programs(1) - 1
    pad_l = SSD_CONV // 2

    def conv(cur_ref, prev_ref, next_ref, out_ref, c0, width):
        ext_ref[0:HALO, 0:width] = jnp.where(has_prev, prev_ref[...].astype(F32), 0.0)
        ext_ref[HALO:HALO + tm, 0:width] = cur_ref[...].astype(F32)
        ext_ref[HALO + tm:, 0:width] = jnp.where(has_next, next_ref[...].astype(F32), 0.0)
        acc = b_ref[:, c0:c0 + width]
        for j in range(SSD_CONV):
            off = HALO - pad_l + j
            acc = acc + ext_ref[off:off + tm, 0:width] * w_ref[j:j + 1, c0:c0 + width]
        out_ref[...] = _silu(acc).astype(BF16)

    conv(xs_ref, xsp_ref, xsn_ref, oxs_ref, 0, BRANCH_W)
    conv(bc_ref, bcp_ref, bcn_ref, obc_ref, BRANCH_W, SSD_CONV_DIM - BRANCH_W)


def _halo_specs(tm, s, width, col):
    r = tm // HALO
    last = s // HALO - 1
    cur = pl.BlockSpec((None, tm, width), lambda bi, i: (bi, i, col))
    prev = pl.BlockSpec((None, HALO, width), lambda bi, i: (bi, jnp.maximum(i * r - 1, 0), col))
    nxt = pl.BlockSpec((None, HALO, width), lambda bi, i: (bi, jnp.minimum((i + 1) * r, last), col))
    return [cur, prev, nxt]


def _ssd_conv(proj3, conv_w, conv_b):
    b, s, _ = proj3.shape
    tm = _tile(s, 512)
    wbc = SSD_CONV_DIM - BRANCH_W
    return pl.pallas_call(
        functools.partial(_ssd_conv_kernel, tm=tm),
        grid=(b, s // tm),
        in_specs=(_halo_specs(tm, s, BRANCH_W, OFF_XS // BRANCH_W) + _halo_specs(tm, s, wbc, OFF_BC // wbc)
                  + [pl.BlockSpec((SSD_CONV, SSD_CONV_DIM), lambda bi, i: (0, 0)),
                     pl.BlockSpec((1, SSD_CONV_DIM), lambda bi, i: (0, 0))]),
        out_specs=[pl.BlockSpec((None, tm, BRANCH_W), lambda bi, i: (bi, i, 0)),
                   pl.BlockSpec((None, tm, wbc), lambda bi, i: (bi, i, 0))],
        out_shape=[jax.ShapeDtypeStruct((b, s, BRANCH_W), BF16), jax.ShapeDtypeStruct((b, s, wbc), BF16)],
        scratch_shapes=[pltpu.VMEM((tm + 2 * HALO, BRANCH_W), F32)],
        compiler_params=_cparams(("parallel", "parallel")),
        name="ssd_conv",
    )(proj3, proj3, proj3, proj3, proj3, proj3, conv_w, conv_b.reshape(1, -1))


def _ssd_chunk(xs_b, bc, dt_raw, bias, alog, st_ref, reverse):
    ln = SSD_CHUNK
    gw = BRANCH_W // SSD_G
    xs = xs_b.astype(F32)
    dt = _softplus(dt_raw + bias)
    da = dt * (-jnp.exp(alog))
    ri = lax.broadcasted_iota(jnp.int32, (ln, ln), 0)
    ci = lax.broadcasted_iota(jnp.int32, (ln, ln), 1)
    tri = (ci >= ri) if reverse else (ri >= ci)
    cs = jnp.dot(tri.astype(F32), da, precision=HIGHEST, preferred_element_type=F32)
    cs_end = cs[0:1, :] if reverse else cs[ln - 1:ln, :]
    er = lax.broadcasted_iota(jnp.int32, (2 * LANE, BRANCH_W), 0)
    ec = lax.broadcasted_iota(jnp.int32, (2 * LANE, BRANCH_W), 1)
    expand = jnp.where(ec // SSD_P == er % LANE, 1.0, 0.0).astype(BF16)
    cols = jnp.concatenate([dt, jnp.exp(cs), jnp.exp(cs_end - cs)], axis=0)
    hi = cols.astype(BF16)
    lo = (cols - hi.astype(F32)).astype(BF16)
    wide = jnp.dot(jnp.concatenate([hi, lo], axis=1), expand, preferred_element_type=F32)
    dt_e = wide[0:ln]
    dec_out_e = wide[ln:2 * ln]
    dec_st_e = wide[2 * ln:3 * ln]
    tot_e = dec_out_e[0:1, :] if reverse else dec_out_e[ln - 1:ln, :]
    cs_t = cs.T
    xd = xs * dt_e
    xdb = xd.astype(BF16)
    xst = (xd * dec_st_e).astype(BF16)
    low = lax.broadcasted_iota(jnp.int32, (ln, LANE), 1) < SSD_P
    zero = jnp.zeros((ln, LANE), BF16)
    outs = []
    for g in range(SSD_G):
        bg = bc[:, g * SSD_N:(g + 1) * SSD_N]
        cg = bc[:, (SSD_G + g) * SSD_N:(SSD_G + g + 1) * SSD_N]
        cb = lax.dot_general(cg, bg, (((1,), (1,)), ((), ())), preferred_element_type=F32)
        st = st_ref[g]
        y_off = jnp.dot(cg, st.astype(BF16), preferred_element_type=F32) * dec_out_e[:, g * gw:(g + 1) * gw]
        bt = bg.astype(F32).T.astype(BF16)
        st_ref[g] = (st * tot_e[:, g * gw:(g + 1) * gw]
                     + jnp.dot(bt, xst[:, g * gw:(g + 1) * gw], preferred_element_type=F32))
        for kp in range(gw // LANE):
            mats = []
            for hh in range(2):
                hd = g * (SSD_HEADS // SSD_G) + 2 * kp + hh
                seg = cs[:, hd:hd + 1] - cs_t[hd:hd + 1, :]
                lm = jnp.exp(jnp.where(tri, seg, -jnp.inf))
                mats.append((cb * lm).astype(BF16))
            lhs = jnp.concatenate(mats, axis=1)
            col = g * gw + kp * LANE
            xp = xdb[:, col:col + LANE]
            rhs = jnp.concatenate([jnp.where(low, xp, zero), jnp.where(low, zero, xp)], axis=0)
            y_diag = jnp.dot(lhs, rhs, preferred_element_type=F32)
            outs.append(y_diag + y_off[:, kp * LANE:(kp + 1) * LANE])
    return jnp.concatenate(outs, axis=1), xs


def _chunk_rows(n_rows, reverse):
    starts = range(0, n_rows, SSD_CHUNK)
    return [slice(r, r + SSD_CHUNK) for r in (reversed(starts) if reverse else starts)]


def _ssd_fwd_kernel(xs_ref, bc_ref, dt_ref, bias_ref, alog_ref, y_ref, st_ref):
    @pl.when(pl.program_id(1) == 0)
    def _():
        st_ref[...] = jnp.zeros_like(st_ref)

    for rows in _chunk_rows(xs_ref.shape[0], reverse=False):
        y, _ = _ssd_chunk(xs_ref[rows, :], bc_ref[rows, :], dt_ref[rows, :], bias_ref[...], alog_ref[...],
                          st_ref, reverse=False)
        y_ref[rows, :] = y


def _ssd_bwd_kernel(xs_ref, bc_ref, dt_ref, bias_ref, alog_ref, yf_ref, z_ref, d_ref, nw_ref,
                    o_ref, st_ref):
    @pl.when(pl.program_id(1) == 0)
    def _():
        st_ref[...] = jnp.zeros_like(st_ref)

    gw = BRANCH_W // SSD_G
    for rows in _chunk_rows(xs_ref.shape[0], reverse=True):
        yb, xs = _ssd_chunk(xs_ref[rows, :], bc_ref[rows, :], dt_ref[rows, :], bias_ref[...], alog_ref[...],
                            st_ref, reverse=True)
        y = yf_ref[rows, :] + yb + xs * d_ref[...]
        y = y * _silu(z_ref[rows, :].astype(F32))
        for g in range(SSD_G):
            yg = y[:, g * gw:(g + 1) * gw]
            yn = yg * lax.rsqrt(jnp.mean(yg * yg, axis=-1, keepdims=True) + EPS)
            o_ref[rows, g * gw:(g + 1) * gw] = (yn * nw_ref[:, g * gw:(g + 1) * gw]).astype(BF16)


def _ssd_scan(xs_act, bc_act, dt3, proj3, dt_bias, a_log, d_skip, norm_w):
    b, s, _ = xs_act.shape
    tb = _tile(s, SSD_BLOCK)
    nc = s // tb
    wbc = SSD_CONV_DIM - BRANCH_W
    gw = BRANCH_W // SSD_G
    pad = lambda v: jnp.pad(v.astype(F32), (0, LANE - SSD_HEADS)).reshape(1, LANE)
    scratch = [pltpu.VMEM((SSD_G, SSD_N, gw), F32)]
    vec = lambda w: pl.BlockSpec((1, w), lambda bi, c: (0, 0))

    def specs(cmap, d):
        return [pl.BlockSpec((None, tb,BRANCH_W), lambda bi, c: (bi, cmap(c), 0)),
                pl.BlockSpec((None, tb,wbc), lambda bi, c: (bi, cmap(c), 0)),
                pl.BlockSpec((None, tb,LANE), lambda bi, c: (bi, cmap(c), d)),
                vec(LANE), vec(LANE)]

    y_f = pl.pallas_call(
        _ssd_fwd_kernel,
        grid=(b, nc),
        in_specs=specs(lambda c: c, 0),
        out_specs=pl.BlockSpec((None, tb,BRANCH_W), lambda bi, c: (bi, c, 0)),
        out_shape=jax.ShapeDtypeStruct((b, s, BRANCH_W), F32),
        scratch_shapes=scratch,
        compiler_params=_cparams(("parallel", "arbitrary")),
        name="ssd_fwd",
    )(xs_act, bc_act, dt3, pad(dt_bias[0]), pad(a_log[0]))

    rev = lambda c: nc - 1 - c
    return pl.pallas_call(
        _ssd_bwd_kernel,
        grid=(b, nc),
        in_specs=specs(rev, 1) + [
            pl.BlockSpec((None, tb,BRANCH_W), lambda bi, c: (bi, rev(c), 0)),
            pl.BlockSpec((None, tb,BRANCH_W), lambda bi, c: (bi, rev(c), OFF_Z // BRANCH_W)),
            vec(BRANCH_W), vec(BRANCH_W)],
        out_specs=pl.BlockSpec((None, tb,BRANCH_W), lambda bi, c: (bi, rev(c), 0)),
        out_shape=jax.ShapeDtypeStruct((b, s, BRANCH_W), BF16),
        scratch_shapes=scratch,
        compiler_params=_cparams(("parallel", "arbitrary")),
        name="ssd_bwd",
    )(xs_act, bc_act, dt3, pad(dt_bias[1]), pad(a_log[1]), y_f, proj3,
      jnp.repeat(d_skip.astype(F32), SSD_P).reshape(1, BRANCH_W), norm_w.reshape(1, BRANCH_W))


def _pool_kernel(u_ref, up_ref, un_ref, g_ref, w_ref, sc_ref, o_ref, ext_ref, *, tm, s):
    i = pl.program_id(1)
    cur = u_ref[...].astype(F32)
    ext_ref[0:HALO, :] = jnp.where(i > 0, up_ref[...].astype(F32), 0.0)
    ext_ref[HALO:HALO + tm, :] = cur
    ext_ref[HALO + tm:, :] = jnp.where(i < pl.num_programs(1) - 1, un_ref[...].astype(F32), 0.0)
    pos = i * tm + lax.broadcasted_iota(jnp.int32, (tm, 1), 0)
    for gi, w in enumerate(POOL_WINDOWS):
        lo = w // 2
        hi = w - 1 - lo
        sl = slice(gi * POOL_GROUP, (gi + 1) * POOL_GROUP)
        acc = ext_ref[HALO - lo:HALO - lo + tm, sl]
        for d in range(-lo + 1, hi + 1):
            acc = acc + ext_ref[HALO + d:HALO + d + tm, sl]
        cnt = (jnp.minimum(pos + hi + 1, s) - jnp.maximum(pos - lo, 0)).astype(F32)
        pooled = acc / cnt - cur[:, sl]
        mixed = jnp.dot(pooled.astype(BF16), w_ref[gi], preferred_element_type=F32)
        o_ref[:, sl] = (mixed * sc_ref[:, sl] * _silu(g_ref[:, sl].astype(F32))).astype(BF16)


def _pool(proj3, pool_w, pool_scale):
    b, s, _ = proj3.shape
    tm = _tile(s, 512)
    return pl.pallas_call(
        functools.partial(_pool_kernel, tm=tm, s=s),
        grid=(b, s // tm),
        in_specs=_halo_specs(tm, s, BRANCH_W, OFF_U // BRANCH_W) + [
            pl.BlockSpec((None, tm, BRANCH_W), lambda bi, i: (bi, i, OFF_GPOOL // BRANCH_W)),
            pl.BlockSpec((len(POOL_WINDOWS), POOL_GROUP, POOL_GROUP), lambda bi, i: (0, 0, 0)),
            pl.BlockSpec((1, BRANCH_W), lambda bi, i: (0, 0))],
        out_specs=pl.BlockSpec((None, tm, BRANCH_W), lambda bi, i: (bi, i, 0)),
        out_shape=jax.ShapeDtypeStruct((b, s, BRANCH_W), BF16),
        scratch_shapes=[pltpu.VMEM((tm + 2 * HALO, BRANCH_W), F32)],
        compiler_params=_cparams(("parallel", "parallel")),
        name="pool",
    )(proj3, proj3, proj3, proj3, pool_w, pool_scale.reshape(1, BRANCH_W))


def _merge_kernel(*refs):
    brs, ws, gs, o_ref = refs[0:4], refs[4:8], refs[8:12], refs[12]
    acc = None
    for br, w, g in zip(brs, ws, gs):
        term = jax.nn.sigmoid(g[...].astype(F32)) * jnp.dot(br[...], w[...], preferred_element_type=F32)
        acc = term if acc is None else acc + term
    o_ref[...] = acc.astype(BF16)


def _merge(branches, w_branch, proj):
    t = proj.shape[0]
    tm = _tile(t, 1024)
    tn = 512
    br_spec = pl.BlockSpec((tm, BRANCH_W), lambda i, j: (i, 0))
    w_specs = [pl.BlockSpec((None, BRANCH_W, tn), lambda i, j, k=k: (k, 0, j)) for k in range(N_BRANCH)]
    g_specs = [pl.BlockSpec((tm, tn), lambda i, j, k=k: (i, (OFF_MG + k * D_MODEL) // tn + j))
               for k in range(N_BRANCH)]
    return pl.pallas_call(
        _merge_kernel,
        grid=(t // tm, D_MODEL // tn),
        in_specs=[br_spec] * N_BRANCH + w_specs + g_specs,
        out_specs=pl.BlockSpec((tm, tn), lambda i, j: (i, j)),
        out_shape=jax.ShapeDtypeStruct((t, D_MODEL), BF16),
        compiler_params=_cparams(("parallel", "parallel")),
        name="merge",
    )(*branches, *([w_branch] * N_BRANCH), *([proj] * N_BRANCH))


def _out_kernel(m_ref, w_ref, x_ref, fn_ref, o_ref, *, final):
    y = x_ref[...] + jnp.dot(m_ref[...], w_ref[...], preferred_element_type=F32)
    if final:
        y = y * lax.rsqrt(jnp.mean(y * y, axis=-1, keepdims=True) + EPS) * fn_ref[...]
    o_ref[...] = y


def _out_proj(merged, w_out, x2, final_norm, final):
    t = x2.shape[0]
    tm = _tile(t, 512)
    return pl.pallas_call(
        functools.partial(_out_kernel, final=final),
        grid=(t // tm,),
        in_specs=[pl.BlockSpec((tm, D_MODEL), lambda i: (i, 0)),
                  pl.BlockSpec((D_MODEL, D_MODEL), lambda i: (0, 0)),
                  pl.BlockSpec((tm, D_MODEL), lambda i: (i, 0)),
                  pl.BlockSpec((1, D_MODEL), lambda i: (0, 0))],
        out_specs=pl.BlockSpec((tm, D_MODEL), lambda i: (i, 0)),
        out_shape=jax.ShapeDtypeStruct((t, D_MODEL), F32),
        compiler_params=_cparams(("parallel",)),
        name="out_proj",
    )(merged, w_out, x2, final_norm.reshape(1, D_MODEL))


def _pack_w_in(w_in):
    wt = jnp.swapaxes(w_in, 1, 2).astype(BF16)
    depth = wt.shape[0]
    zeros = lambda n: jnp.zeros((depth, n, D_MODEL), BF16)
    half = MLA_ROPE // 2
    kr0 = IN_OFFS[2]
    w_tail = jnp.concatenate([
        wt[:, IN_OFFS[1]:IN_OFFS[2]],
        wt[:, kr0:kr0 + MLA_ROPE], zeros(LANE - MLA_ROPE),
        wt[:, kr0 + half:kr0 + MLA_ROPE], wt[:, kr0:kr0 + half], zeros(LANE - MLA_ROPE),
    ], axis=1)
    dt0 = IN_OFFS[10]
    w_dt = jnp.concatenate([wt[:, dt0:dt0 + SSD_HEADS], zeros(LANE - SSD_HEADS),
                            wt[:, dt0 + SSD_HEADS:dt0 + 2 * SSD_HEADS], zeros(LANE - SSD_HEADS)], axis=1)
    return wt, w_tail, w_dt


def _pack_mla(w_uq, w_ukv):
    half = MLA_ROPE // 2
    wq3 = w_uq.reshape(MLA_Q_LORA, MLA_HEADS, MLA_NOPE + MLA_ROPE)
    zq = jnp.zeros((MLA_Q_LORA, MLA_HEADS, LANE - MLA_ROPE), w_uq.dtype)
    wq = jnp.concatenate([wq3, zq], axis=2).reshape(MLA_Q_LORA, MLA_HEADS * MLA_QK).astype(BF16)
    wqp = jnp.concatenate([wq3[:, :, MLA_NOPE + half:], wq3[:, :, MLA_NOPE:MLA_NOPE + half], zq],
                          axis=2).reshape(MLA_Q_LORA, MLA_HEADS * LANE).astype(BF16)
    wkv3 = w_ukv.reshape(MLA_KV_LORA, MLA_HEADS, MLA_NOPE + MLA_V)
    wk = wkv3[:, :, :MLA_NOPE].reshape(MLA_KV_LORA, MLA_HEADS * MLA_NOPE).astype(BF16)
    wv = wkv3[:, :, MLA_NOPE:].reshape(MLA_KV_LORA, MLA_HEADS * MLA_V).astype(BF16)
    return wq, wqp, wk, wv


def _rope_tables(s):
    def angles(rot):
        half = rot // 2
        inv_freq = jnp.power(ROPE_THETA, -jnp.arange(half, dtype=F32) * 2.0 / rot)
        ang = jnp.arange(s, dtype=F32)[:, None] * inv_freq[None, :]
        return jnp.cos(ang), jnp.sin(ang)

    c, sn = angles(MLA_ROPE)
    z = jnp.zeros((s, LANE - MLA_ROPE), F32)
    mla_cos = jnp.concatenate([c, c, z], axis=1)
    mla_sin = jnp.concatenate([-sn, sn, z], axis=1)
    c, sn = angles(DIFF_ROT)
    zh = jnp.zeros_like(sn)
    rest = LANE - DIFF_ROT
    d_cos = jnp.concatenate([c, c, jnp.ones((s, rest), F32)], axis=1)
    d_sa = jnp.concatenate([zh, sn, jnp.zeros((s, rest), F32)], axis=1)
    d_sb = jnp.concatenate([-sn, zh, jnp.zeros((s, rest), F32)], axis=1)
    return mla_cos, mla_sin, d_cos, d_sa, d_sb


def _trunk(x, w_inproj, packed, norm_w, mla_q_norm, mla_kv_norm, diff_lambda, diff_subln, ssd_conv_w, ssd_conv_b,
           ssd_dt_bias, ssd_a_log, ssd_d, ssd_norm, pool_w, pool_scale, w_branch, w_out, final_norm):
    b, s, _ = x.shape
    t = b * s
    mla_cos, mla_sin, d_cos, d_sa, d_sb = _rope_tables(s)
    x2 = x.reshape(t, D_MODEL)
    for l in range(DEPTH):
        wq, wqp, wk, wv = packed[l]
        lambda_init = 0.8 - 0.6 * math.exp(-0.3 * l)
        proj, dt = _inproj(x2, norm_w[l], *w_inproj, l)
        proj3 = proj.reshape(b, s, NP_MAIN)
        q, k, v = _mla_prep(proj, s, mla_cos, mla_sin, mla_q_norm[l], mla_kv_norm[l], wq, wqp, wk, wv)
        br_mla = _mla_attn(q.reshape(b, s, -1), k.reshape(b, s, -1), v.reshape(b, s, -1), proj3)
        dq, dk = _diff_prep(proj, s, d_cos, d_sa, d_sb)
        br_diff = _diff_attn(dq.reshape(b, s, -1), dk.reshape(b, s, -1), proj3, diff_lambda[l],
                             diff_subln[l], lambda_init)
        xs_act, bc_act = _ssd_conv(proj3, ssd_conv_w[l], ssd_conv_b[l])
        br_ssd = _ssd_scan(xs_act, bc_act, dt.reshape(b, s, DT_W), proj3, ssd_dt_bias[l], ssd_a_log[l],
                           ssd_d[l], ssd_norm[l])
        br_pool = _pool(proj3, pool_w[l].astype(BF16), pool_scale[l])
        branches = [a.reshape(t, BRANCH_W) for a in (br_mla, br_diff, br_ssd, br_pool)]
        merged = _merge(branches, w_branch[l].astype(BF16), proj)
        x2 = _out_proj(merged, w_out[l].astype(BF16), x2, final_norm, final=(l == DEPTH - 1))
    return x2.reshape(b, s, D_MODEL)


def kernel(x_prompt, x_sample, norm_w, w_in, mla_q_norm, mla_w_uq, mla_kv_norm, mla_w_ukv, diff_lambda,
           diff_subln, ssd_conv_w, ssd_conv_b, ssd_dt_bias, ssd_a_log, ssd_d, ssd_norm, pool_w, pool_scale,
           w_branch, w_out, final_norm):
    w_inproj = _pack_w_in(w_in)
    packed = [_pack_mla(mla_w_uq[l], mla_w_ukv[l]) for l in range(DEPTH)]
    rest = (norm_w, mla_q_norm, mla_kv_norm, diff_lambda, diff_subln, ssd_conv_w, ssd_conv_b, ssd_dt_bias,
            ssd_a_log, ssd_d, ssd_norm, pool_w, pool_scale, w_branch, w_out, final_norm)
    return (_trunk(x_prompt, w_inproj, packed, *rest), _trunk(x_sample, w_inproj, packed, *rest))
```

```python
import functools
import math

import numpy as np
import jax
import jax.numpy as jnp
from jax import lax
from jax.experimental import pallas as pl
from jax.experimental.pallas import tpu as pltpu

F32 = jnp.float32
BF16 = jnp.bfloat16
HIGHEST = lax.Precision.HIGHEST

D_MODEL = 2048
DEPTH = 2
BRANCH_W = D_MODEL // 2
N_BRANCH = 4
ROPE_THETA = 500000.0
EPS = 1e-6
MLA_HEADS = 8
MLA_NOPE = 128
MLA_ROPE = 64
MLA_V = BRANCH_W // MLA_HEADS
MLA_Q_LORA = 512
MLA_KV_LORA = 256
DIFF_HEADS = 4
DIFF_HD = BRANCH_W // (2 * DIFF_HEADS)
DIFF_ROT = DIFF_HD // 4
SSD_P = 64
SSD_HEADS = BRANCH_W // SSD_P
SSD_N = 128
SSD_G = 2
SSD_CONV = 4
SSD_CHUNK = 128
SSD_BLOCK = 1024
SEQ_TILE = 1024
SSD_CONV_DIM = BRANCH_W + 2 * SSD_G * SSD_N
POOL_WINDOWS = (2, 4, 8, 16)
POOL_GROUP = BRANCH_W // 4
IN_SIZES = (MLA_Q_LORA, MLA_KV_LORA, MLA_ROPE, BRANCH_W, BRANCH_W, BRANCH_W, BRANCH_W, BRANCH_W,
            BRANCH_W, SSD_CONV_DIM, 2 * SSD_HEADS, BRANCH_W, BRANCH_W, N_BRANCH * D_MODEL)
IN_OFFS = tuple(int(v) for v in np.cumsum((0,) + IN_SIZES))

LANE = 128
HALO = 16
VMEM_LIMIT = 56 * 1024 * 1024

OFF_GMLA = 0
OFF_DQ = 1 * BRANCH_W
OFF_DK = 2 * BRANCH_W
OFF_DV = 3 * BRANCH_W
OFF_GDIFF = 4 * BRANCH_W
OFF_Z = 5 * BRANCH_W
OFF_XS = 6 * BRANCH_W
OFF_U = 7 * BRANCH_W
OFF_GPOOL = 8 * BRANCH_W
OFF_MG = 9 * BRANCH_W
OFF_CQ = OFF_MG + N_BRANCH * D_MODEL
OFF_BC = OFF_CQ + MLA_Q_LORA
OFF_CKV = OFF_BC + 2 * SSD_G * SSD_N
OFF_KR = OFF_CKV + MLA_KV_LORA
OFF_KRP = OFF_KR + LANE
NP_MAIN = OFF_KRP + LANE
DT_W = 2 * LANE

ATTN_TQ = 1024
ATTN_SUB = 256
MLA_QK = 2 * LANE
LOG2E = math.log2(math.e)


def _cparams(sem, vmem=VMEM_LIMIT):
    return pltpu.CompilerParams(dimension_semantics=sem, vmem_limit_bytes=vmem)


def _tile(n, pref):
    t = min(n, pref)
    while n % t:
        t //= 2
    return t


def _silu(x):
    return x * jax.nn.sigmoid(x)


def _softplus(x):
    return jnp.maximum(x, 0.0) + jnp.log1p(jnp.exp(-jnp.abs(x)))


_NT = (((1,), (1,)), ((), ()))
INPROJ_TN = 1024
TAIL_W = NP_MAIN - OFF_CQ

_N_TILE_A = OFF_U // INPROJ_TN
_N_TILE_MAIN = OFF_CQ // INPROJ_TN
_N_TILE = _N_TILE_MAIN + 2
NP_OUT = _N_TILE * INPROJ_TN


def _inproj_src_row(j):
    return jnp.where(j < _N_TILE_A, IN_OFFS[3] + j * INPROJ_TN,
                     jnp.where(j < _N_TILE_MAIN, IN_OFFS[11] + (j - _N_TILE_A) * INPROJ_TN, IN_OFFS[0]))


def _inproj_kernel(x_ref, nw_ref, w_ref, wtail_ref, wdt_ref, o_ref, dt_ref, h_ref):
    j = pl.program_id(1)
    tn = INPROJ_TN

    @pl.when(j == 0)
    def _():
        x = x_ref[...]
        y = x * lax.rsqrt(jnp.mean(x * x, axis=-1, keepdims=True) + EPS)
        hb = (y * nw_ref[...]).astype(BF16)
        h_ref[...] = hb
        dt_ref[...] = lax.dot_general(hb, wdt_ref[...], _NT, preferred_element_type=F32)

    @pl.when(j < _N_TILE_MAIN)
    def _():
        o_ref[...] = lax.dot_general(h_ref[...], w_ref[...], _NT, preferred_element_type=F32).astype(BF16)

    @pl.when(j == _N_TILE_MAIN)
    def _():
        o_ref[...] = lax.dot_general(h_ref[...], wtail_ref[0:tn, :], _NT,
                                     preferred_element_type=F32).astype(BF16)

    @pl.when(j == _N_TILE_MAIN + 1)
    def _():
        rest = TAIL_W - tn
        o_ref[:, 0:rest] = lax.dot_general(h_ref[...], wtail_ref[tn:TAIL_W, :], _NT,
                                           preferred_element_type=F32).astype(BF16)
        o_ref[:, rest:] = jnp.zeros((o_ref.shape[0], tn - rest), BF16)


def _inproj(x2, norm_w, wt, w_tail, w_dt, l):
    t = x2.shape[0]
    tm = _tile(t, 1024)
    tn = INPROJ_TN
    return pl.pallas_call(
        _inproj_kernel,
        grid=(t // tm, _N_TILE),
        in_specs=[
            pl.BlockSpec((tm, D_MODEL), lambda i, j: (i, 0)),
            pl.BlockSpec((1, D_MODEL), lambda i, j: (0, 0)),
            pl.BlockSpec((pl.Element(tn), pl.Element(D_MODEL)),
                         lambda i, j: (pl.multiple_of(l * IN_OFFS[-1] + _inproj_src_row(j), HALO), 0)),
            pl.BlockSpec((None, TAIL_W, D_MODEL), lambda i, j: (l, 0, 0)),
            pl.BlockSpec((None, DT_W, D_MODEL), lambda i, j: (l, 0, 0)),
        ],
        out_specs=[
            pl.BlockSpec((tm, tn), lambda i, j: (i, j)),
            pl.BlockSpec((tm, DT_W), lambda i, j: (i, 0)),
        ],
        out_shape=[jax.ShapeDtypeStruct((t, NP_OUT), BF16), jax.ShapeDtypeStruct((t, DT_W), F32)],
        scratch_shapes=[pltpu.VMEM((tm, D_MODEL), BF16)],
        compiler_params=_cparams(("parallel", "arbitrary")),
        name="inproj",
    )(x2, norm_w.reshape(1, D_MODEL), wt.reshape(-1, D_MODEL), w_tail, w_dt)


def _mla_prep_kernel(cq_ref, ckv_ref, kr_ref, krp_ref, cos_ref, sin_ref, qn_ref, kvn_ref,
                     wq_ref, wqp_ref, wk_ref, wv_ref, q_ref, k_ref, v_ref):
    cos = cos_ref[...]
    sin = sin_ref[...]
    cq = cq_ref[...].astype(F32)
    ncq = (cq * lax.rsqrt(jnp.mean(cq * cq, axis=-1, keepdims=True) + EPS) * qn_ref[...]).astype(BF16)
    qm = jnp.dot(ncq, wq_ref[...], preferred_element_type=F32)
    qp = jnp.dot(ncq, wqp_ref[...], preferred_element_type=F32)
    ckv = ckv_ref[...].astype(F32)
    nkv = (ckv * lax.rsqrt(jnp.mean(ckv * ckv, axis=-1, keepdims=True) + EPS) * kvn_ref[...]).astype(BF16)
    kn = jnp.dot(nkv, wk_ref[...], preferred_element_type=F32)
    v_ref[...] = jnp.dot(nkv, wv_ref[...], preferred_element_type=F32).astype(BF16)
    kpe = (kr_ref[...].astype(F32) * cos + krp_ref[...].astype(F32) * sin).astype(BF16)
    qscale = (MLA_NOPE + MLA_ROPE) ** -0.5 * LOG2E
    for h in range(MLA_HEADS):
        lo = h * MLA_QK
        q_ref[:, lo:lo + LANE] = (qm[:, lo:lo + LANE] * qscale).astype(BF16)
        qr = qm[:, lo + LANE:lo + 2 * LANE] * cos + qp[:, h * LANE:(h + 1) * LANE] * sin
        q_ref[:, lo + LANE:lo + 2 * LANE] = (qr * qscale).astype(BF16)
        k_ref[:, lo:lo + LANE] = kn[:, h * LANE:(h + 1) * LANE].astype(BF16)
        k_ref[:, lo + LANE:lo + 2 * LANE] = kpe


def _mla_prep(proj, s, cos_t, sin_t, q_norm, kv_norm, wq, wqp, wk, wv):
    t = proj.shape[0]
    tm = _tile(s, SEQ_TILE)
    ns = s // tm
    hq = MLA_HEADS * MLA_QK
    const = lambda shape: pl.BlockSpec(shape, lambda i: (0, 0))
    return pl.pallas_call(
        _mla_prep_kernel,
        grid=(t // tm,),
        in_specs=[
            pl.BlockSpec((tm, MLA_Q_LORA), lambda i: (i, OFF_CQ // MLA_Q_LORA)),
            pl.BlockSpec((tm, MLA_KV_LORA), lambda i: (i, OFF_CKV // MLA_KV_LORA)),
            pl.BlockSpec((tm, LANE), lambda i: (i, OFF_KR // LANE)),
            pl.BlockSpec((tm, LANE), lambda i: (i, OFF_KRP // LANE)),
            pl.BlockSpec((tm, LANE), lambda i: (i % ns, 0)),
            pl.BlockSpec((tm, LANE), lambda i: (i % ns, 0)),
            const((1, MLA_Q_LORA)),
            const((1, MLA_KV_LORA)),
            const((MLA_Q_LORA, hq)),
            const((MLA_Q_LORA, MLA_HEADS * LANE)),
            const((MLA_KV_LORA, MLA_HEADS * MLA_NOPE)),
            const((MLA_KV_LORA, MLA_HEADS * MLA_V)),
        ],
        out_specs=[
            pl.BlockSpec((tm, hq), lambda i: (i, 0)),
            pl.BlockSpec((tm, hq), lambda i: (i, 0)),
            pl.BlockSpec((tm, MLA_HEADS * MLA_V), lambda i: (i, 0)),
        ],
        out_shape=[jax.ShapeDtypeStruct((t, hq), BF16), jax.ShapeDtypeStruct((t, hq), BF16),
                   jax.ShapeDtypeStruct((t, MLA_HEADS * MLA_V), BF16)],
        compiler_params=_cparams(("parallel",)),
        name="mla_prep",
    )(proj, proj, proj, proj, cos_t, sin_t, q_norm.reshape(1, -1), kv_norm.reshape(1, -1), wq, wqp, wk, wv)


def _softmax_pv(q, k_ref, v_ref):
    sc = lax.dot_general(q, k_ref[...], (((1,), (1,)), ((), ())), preferred_element_type=F32)
    m = jnp.max(sc, axis=-1, keepdims=True)
    p = jnp.exp2(sc - m)
    l = jnp.sum(p, axis=-1, keepdims=True)
    acc = jnp.dot(p.astype(BF16), v_ref[...], preferred_element_type=F32)
    return acc, l


def _mla_attn_kernel(q_ref, k_ref, v_ref, g_ref, o_ref, *, sub):
    for r in range(0, q_ref.shape[0], sub):
        acc, l = _softmax_pv(q_ref[r:r + sub, :], k_ref, v_ref)
        o_ref[r:r + sub, :] = ((acc / l) * _silu(g_ref[r:r + sub, :].astype(F32))).astype(BF16)


def _mla_attn(q, k, v, proj3):
    b, s, _ = q.shape
    tq = _tile(s, ATTN_TQ)
    return pl.pallas_call(
        functools.partial(_mla_attn_kernel, sub=min(tq, ATTN_SUB)),
        grid=(b, MLA_HEADS, s // tq),
        in_specs=[
            pl.BlockSpec((None, tq, MLA_QK), lambda bi, h, qi: (bi, qi, h)),
            pl.BlockSpec((None, s, MLA_QK), lambda bi, h, qi: (bi, 0, h)),
            pl.BlockSpec((None, s, MLA_V), lambda bi, h, qi: (bi, 0, h)),
            pl.BlockSpec((None, tq, MLA_V), lambda bi, h, qi: (bi, qi, OFF_GMLA // MLA_V + h)),
        ],
        out_specs=pl.BlockSpec((None, tq, MLA_V), lambda bi, h, qi: (bi, qi, h)),
        out_shape=jax.ShapeDtypeStruct((b, s, BRANCH_W), BF16),
        compiler_params=_cparams(("parallel", "parallel", "parallel")),
        name="mla_attn",
    )(q, k, v, proj3)


def _diff_prep_kernel(dq_ref, dk_ref, cos_ref, sa_ref, sb_ref, q_ref, k_ref):
    cos = cos_ref[...]
    sa = sa_ref[...]
    sb = sb_ref[...]
    half = DIFF_ROT // 2
    qscale = DIFF_HD ** -0.5 * LOG2E
    for h in range(2 * DIFF_HEADS):
        sl = slice(h * DIFF_HD, (h + 1) * DIFF_HD)
        xq = dq_ref[:, sl].astype(F32)
        xk = dk_ref[:, sl].astype(F32)
        rq = xq * cos + pltpu.roll(xq, half, 1) * sa + pltpu.roll(xq, DIFF_HD - half, 1) * sb
        rk = xk * cos + pltpu.roll(xk, half, 1) * sa + pltpu.roll(xk, DIFF_HD - half, 1) * sb
        q_ref[:, sl] = (rq * qscale).astype(BF16)
        k_ref[:, sl] = rk.astype(BF16)


def _diff_prep(proj, s, cos_t, sa_t, sb_t):
    t = proj.shape[0]
    tm = _tile(s, SEQ_TILE)
    ns = s // tm
    tab = pl.BlockSpec((tm, LANE), lambda i: (i % ns, 0))
    return pl.pallas_call(
        _diff_prep_kernel,
        grid=(t // tm,),
        in_specs=[
            pl.BlockSpec((tm, BRANCH_W), lambda i: (i, OFF_DQ // BRANCH_W)),
            pl.BlockSpec((tm, BRANCH_W), lambda i: (i, OFF_DK // BRANCH_W)),
            tab, tab, tab,
        ],
        out_specs=[pl.BlockSpec((tm, BRANCH_W), lambda i: (i, 0))] * 2,
        out_shape=[jax.ShapeDtypeStruct((t, BRANCH_W), BF16)] * 2,
        compiler_params=_cparams(("parallel",)),
        name="diff_prep",
    )(proj, proj, cos_t, sa_t, sb_t)


def _diff_attn_kernel(q1_ref, q2_ref, k1_ref, k2_ref, v_ref, g_ref, lam_ref, sub_ref, o_ref, *,
                      sub, lambda_init):
    lp = lam_ref[...]
    lam = (jnp.exp(jnp.sum(lp[0:1] * lp[1:2], axis=-1, keepdims=True))
           - jnp.exp(jnp.sum(lp[2:3] * lp[3:4], axis=-1, keepdims=True)) + lambda_init)
    for r in range(0, q1_ref.shape[0], sub):
        acc1, l1 = _softmax_pv(q1_ref[r:r + sub, :], k1_ref, v_ref)
        acc2, l2 = _softmax_pv(q2_ref[r:r + sub, :], k2_ref, v_ref)
        o = acc1 / l1 - lam * (acc2 / l2)
        o = o * lax.rsqrt(jnp.mean(o * o, axis=-1, keepdims=True) + EPS) * sub_ref[...]
        o = o * (1.0 - lambda_init)
        o_ref[r:r + sub, :] = (o * _silu(g_ref[r:r + sub, :].astype(F32))).astype(BF16)


def _diff_attn(q, k, proj3, lam_params, subln, lambda_init):
    b, s, _ = q.shape
    tq = _tile(s, ATTN_TQ)
    dv = 2 * DIFF_HD
    return pl.pallas_call(
        functools.partial(_diff_attn_kernel, sub=min(tq, ATTN_SUB), lambda_init=lambda_init),
        grid=(b, DIFF_HEADS, s // tq),
        in_specs=[
            pl.BlockSpec((None, tq, DIFF_HD), lambda bi, h, qi: (bi, qi, 2 * h)),
            pl.BlockSpec((None, tq, DIFF_HD), lambda bi, h, qi: (bi, qi, 2 * h + 1)),
            pl.BlockSpec((None, s, DIFF_HD), lambda bi, h, qi: (bi, 0, 2 * h)),
            pl.BlockSpec((None, s, DIFF_HD), lambda bi, h, qi: (bi, 0, 2 * h + 1)),
            pl.BlockSpec((None, s, dv), lambda bi, h, qi: (bi, 0, OFF_DV // dv + h)),
            pl.BlockSpec((None, tq, dv), lambda bi, h, qi: (bi, qi, OFF_GDIFF // dv + h)),
            pl.BlockSpec((4, DIFF_HD), lambda bi, h, qi: (0, 0)),
            pl.BlockSpec((1, dv), lambda bi, h, qi: (0, 0)),
        ],
        out_specs=pl.BlockSpec((None, tq, dv), lambda bi, h, qi: (bi, qi, h)),
        out_shape=jax.ShapeDtypeStruct((b, s, BRANCH_W), BF16),
        compiler_params=_cparams(("parallel", "parallel", "parallel")),
        name="diff_attn",
    )(q, q, k, k, proj3, proj3, lam_params, subln.reshape(1, dv))


def _ssd_conv_kernel(xs_ref, xsp_ref, xsn_ref, bc_ref, bcp_ref, bcn_ref, w_ref, b_ref,
                     oxs_ref, obc_ref, ext_ref, *, tm):
    i = pl.program_id(1)
    has_prev = i > 0
    has_next = i < pl.num_programs(1) - 1
    pad_l = SSD_CONV // 2

    def conv(cur_ref, prev_ref, next_ref, out_ref, c0, width):
        ext_ref[0:HALO, 0:width] = jnp.where(has_prev, prev_ref[...].astype(F32), 0.0)
        ext_ref[HALO:HALO + tm, 0:width] = cur_ref[...].astype(F32)
        ext_ref[HALO + tm:, 0:width] = jnp.where(has_next, next_ref[...].astype(F32), 0.0)
        acc = b_ref[:, c0:c0 + width]
        for j in range(SSD_CONV):
            off = HALO - pad_l + j
            acc = acc + ext_ref[off:off + tm, 0:width] * w_ref[j:j + 1, c0:c0 + width]
        out_ref[...] = _silu(acc).astype(BF16)

    conv(xs_ref, xsp_ref, xsn_ref, oxs_ref, 0, BRANCH_W)
    conv(bc_ref, bcp_ref, bcn_ref, obc_ref, BRANCH_W, SSD_CONV_DIM - BRANCH_W)


def _halo_specs(tm, s, width, col):
    r = tm // HALO
    last = s // HALO - 1
    cur = pl.BlockSpec((None, tm, width), lambda bi, i: (bi, i, col))
    prev = pl.BlockSpec((None, HALO, width), lambda bi, i: (bi, jnp.maximum(i * r - 1, 0), col))
    nxt = pl.BlockSpec((None, HALO, width), lambda bi, i: (bi, jnp.minimum((i + 1) * r, last), col))
    return [cur, prev, nxt]


def _ssd_conv(proj3, conv_w, conv_b):
    b, s, _ = proj3.shape
    tm = _tile(s, SEQ_TILE)
    wbc = SSD_CONV_DIM - BRANCH_W
    return pl.pallas_call(
        functools.partial(_ssd_conv_kernel, tm=tm),
        grid=(b, s // tm),
        in_specs=(_halo_specs(tm, s, BRANCH_W, OFF_XS // BRANCH_W) + _halo_specs(tm, s, wbc, OFF_BC // wbc)
                  + [pl.BlockSpec((SSD_CONV, SSD_CONV_DIM), lambda bi, i: (0, 0)),
                     pl.BlockSpec((1, SSD_CONV_DIM), lambda bi, i: (0, 0))]),
        out_specs=[pl.BlockSpec((None, tm, BRANCH_W), lambda bi, i: (bi, i, 0)),
                   pl.BlockSpec((None, tm, wbc), lambda bi, i: (bi, i, 0))],
        out_shape=[jax.ShapeDtypeStruct((b, s, BRANCH_W), BF16), jax.ShapeDtypeStruct((b, s, wbc), BF16)],
        scratch_shapes=[pltpu.VMEM((tm + 2 * HALO, BRANCH_W), F32)],
        compiler_params=_cparams(("parallel", "parallel")),
        name="ssd_conv",
    )(proj3, proj3, proj3, proj3, proj3, proj3, conv_w, conv_b.reshape(1, -1))


def _ssd_chunk(xs_b, bc, dt_raw, bias, alog, st_ref, reverse):
    ln = SSD_CHUNK
    gw = BRANCH_W // SSD_G
    xs = xs_b.astype(F32)
    dt = _softplus(dt_raw + bias)
    da = dt * (-jnp.exp(alog))
    ri = lax.broadcasted_iota(jnp.int32, (ln, ln), 0)
    ci = lax.broadcasted_iota(jnp.int32, (ln, ln), 1)
    tri = (ci >= ri) if reverse else (ri >= ci)
    cs = jnp.dot(tri.astype(F32), da, precision=HIGHEST, preferred_element_type=F32)
    cs_end = cs[0:1, :] if reverse else cs[ln - 1:ln, :]
    er = lax.broadcasted_iota(jnp.int32, (2 * LANE, BRANCH_W), 0)
    ec = lax.broadcasted_iota(jnp.int32, (2 * LANE, BRANCH_W), 1)
    expand = jnp.where(ec // SSD_P == er % LANE, 1.0, 0.0).astype(BF16)
    cols = jnp.concatenate([dt, jnp.exp(cs), jnp.exp(cs_end - cs)], axis=0)
    hi = cols.astype(BF16)
    lo = (cols - hi.astype(F32)).astype(BF16)
    wide = jnp.dot(jnp.concatenate([hi, lo], axis=1), expand, preferred_element_type=F32)
    dt_e = wide[0:ln]
    dec_out_e = wide[ln:2 * ln]
    dec_st_e = wide[2 * ln:3 * ln]
    tot_e = dec_out_e[0:1, :] if reverse else dec_out_e[ln - 1:ln, :]
    cs_t = cs.T
    xd = xs * dt_e
    xdb = xd.astype(BF16)
    xst = (xd * dec_st_e).astype(BF16)
    low = lax.broadcasted_iota(jnp.int32, (ln, LANE), 1) < SSD_P
    zero = jnp.zeros((ln, LANE), BF16)
    outs = []
    for g in range(SSD_G):
        bg = bc[:, g * SSD_N:(g + 1) * SSD_N]
        cg = bc[:, (SSD_G + g) * SSD_N:(SSD_G + g + 1) * SSD_N]
        cb = lax.dot_general(cg, bg, (((1,), (1,)), ((), ())), preferred_element_type=F32)
        st = st_ref[g]
        y_off = jnp.dot(cg, st.astype(BF16), preferred_element_type=F32) * dec_out_e[:, g * gw:(g + 1) * gw]
        bt = bg.astype(F32).T.astype(BF16)
        st_ref[g] = (st * tot_e[:, g * gw:(g + 1) * gw]
                     + jnp.dot(bt, xst[:, g * gw:(g + 1) * gw], preferred_element_type=F32))
        for kp in range(gw // LANE):
            mats = []
            for hh in range(2):
                hd = g * (SSD_HEADS // SSD_G) + 2 * kp + hh
                seg = cs[:, hd:hd + 1] - cs_t[hd:hd + 1, :]
                lm = jnp.exp(jnp.where(tri, seg, -jnp.inf))
                mats.append((cb * lm).astype(BF16))
            lhs = jnp.concatenate(mats, axis=1)
            col = g * gw + kp * LANE
            xp = xdb[:, col:col + LANE]
            rhs = jnp.concatenate([jnp.where(low, xp, zero), jnp.where(low, zero, xp)], axis=0)
            y_diag = jnp.dot(lhs, rhs, preferred_element_type=F32)
            outs.append(y_diag + y_off[:, kp * LANE:(kp + 1) * LANE])
    return jnp.concatenate(outs, axis=1), xs


def _chunk_rows(n_rows, reverse):
    starts = range(0, n_rows, SSD_CHUNK)
    return [slice(r, r + SSD_CHUNK) for r in (reversed(starts) if reverse else starts)]


def _ssd_fwd_kernel(xs_ref, bc_ref, dt_ref, bias_ref, alog_ref, y_ref, st_ref):
    @pl.when(pl.program_id(1) == 0)
    def _():
        st_ref[...] = jnp.zeros_like(st_ref)

    for rows in _chunk_rows(xs_ref.shape[0], reverse=False):
        y, _ = _ssd_chunk(xs_ref[rows, :], bc_ref[rows, :], dt_ref[rows, :], bias_ref[...], alog_ref[...],
                          st_ref, reverse=False)
        y_ref[rows, :] = y


def _ssd_bwd_kernel(xs_ref, bc_ref, dt_ref, bias_ref, alog_ref, yf_ref, z_ref, d_ref, nw_ref,
                    o_ref, st_ref):
    @pl.when(pl.program_id(1) == 0)
    def _():
        st_ref[...] = jnp.zeros_like(st_ref)

    gw = BRANCH_W // SSD_G
    for rows in _chunk_rows(xs_ref.shape[0], reverse=True):
        yb, xs = _ssd_chunk(xs_ref[rows, :], bc_ref[rows, :], dt_ref[rows, :], bias_ref[...], alog_ref[...],
                            st_ref, reverse=True)
        y = yf_ref[rows, :] + yb + xs * d_ref[...]
        y = y * _silu(z_ref[rows, :].astype(F32))
        for g in range(SSD_G):
            yg = y[:, g * gw:(g + 1) * gw]
            yn = yg * lax.rsqrt(jnp.mean(yg * yg, axis=-1, keepdims=True) + EPS)
            o_ref[rows, g * gw:(g + 1) * gw] = (yn * nw_ref[:, g * gw:(g + 1) * gw]).astype(BF16)


def _ssd_scan(xs_act, bc_act, dt3, proj3, dt_bias, a_log, d_skip, norm_w):
    b, s, _ = xs_act.shape
    tb = _tile(s, SSD_BLOCK)
    nc = s // tb
    wbc = SSD_CONV_DIM - BRANCH_W
    gw = BRANCH_W // SSD_G
    pad = lambda v: jnp.pad(v.astype(F32), (0, LANE - SSD_HEADS)).reshape(1, LANE)
    scratch = [pltpu.VMEM((SSD_G, SSD_N, gw), F32)]
    vec = lambda w: pl.BlockSpec((1, w), lambda bi, c: (0, 0))

    def specs(cmap, d):
        return [pl.BlockSpec((None, tb,BRANCH_W), lambda bi, c: (bi, cmap(c), 0)),
                pl.BlockSpec((None, tb,wbc), lambda bi, c: (bi, cmap(c), 0)),
                pl.BlockSpec((None, tb,LANE), lambda bi, c: (bi, cmap(c), d)),
                vec(LANE), vec(LANE)]

    y_f = pl.pallas_call(
        _ssd_fwd_kernel,
        grid=(b, nc),
        in_specs=specs(lambda c: c, 0),
        out_specs=pl.BlockSpec((None, tb,BRANCH_W), lambda bi, c: (bi, c, 0)),
        out_shape=jax.ShapeDtypeStruct((b, s, BRANCH_W), F32),
        scratch_shapes=scratch,
        compiler_params=_cparams(("parallel", "arbitrary")),
        name="ssd_fwd",
    )(xs_act, bc_act, dt3, pad(dt_bias[0]), pad(a_log[0]))

    rev = lambda c: nc - 1 - c
    return pl.pallas_call(
        _ssd_bwd_kernel,
        grid=(b, nc),
        in_specs=specs(rev, 1) + [
            pl.BlockSpec((None, tb,BRANCH_W), lambda bi, c: (bi, rev(c), 0)),
            pl.BlockSpec((None, tb,BRANCH_W), lambda bi, c: (bi, rev(c), OFF_Z // BRANCH_W)),
            vec(BRANCH_W), vec(BRANCH_W)],
        out_specs=pl.BlockSpec((None, tb,BRANCH_W), lambda bi, c: (bi, rev(c), 0)),
        out_shape=jax.ShapeDtypeStruct((b, s, BRANCH_W), BF16),
        scratch_shapes=scratch,
        compiler_params=_cparams(("parallel", "arbitrary")),
        name="ssd_bwd",
    )(xs_act, bc_act, dt3, pad(dt_bias[1]), pad(a_log[1]), y_f, proj3,
      jnp.repeat(d_skip.astype(F32), SSD_P).reshape(1, BRANCH_W), norm_w.reshape(1, BRANCH_W))


def _pool_kernel(u_ref, up_ref, un_ref, g_ref, w_ref, sc_ref, o_ref, ext_ref, *, tm, s):
    i = pl.program_id(1)
    cur = u_ref[...].astype(F32)
    ext_ref[0:HALO, :] = jnp.where(i > 0, up_ref[...].astype(F32), 0.0)
    ext_ref[HALO:HALO + tm, :] = cur
    ext_ref[HALO + tm:, :] = jnp.where(i < pl.num_programs(1) - 1, un_ref[...].astype(F32), 0.0)
    pos = i * tm + lax.broadcasted_iota(jnp.int32, (tm, 1), 0)
    for gi, w in enumerate(POOL_WINDOWS):
        lo = w // 2
        hi = w - 1 - lo
        sl = slice(gi * POOL_GROUP, (gi + 1) * POOL_GROUP)
        acc = ext_ref[HALO - lo:HALO - lo + tm, sl]
        for d in range(-lo + 1, hi + 1):
            acc = acc + ext_ref[HALO + d:HALO + d + tm, sl]
        cnt = (jnp.minimum(pos + hi + 1, s) - jnp.maximum(pos - lo, 0)).astype(F32)
        pooled = acc / cnt - cur[:, sl]
        mixed = jnp.dot(pooled.astype(BF16), w_ref[gi], preferred_element_type=F32)
        o_ref[:, sl] = (mixed * sc_ref[:, sl] * _silu(g_ref[:, sl].astype(F32))).astype(BF16)


def _pool(proj3, pool_w, pool_scale):
    b, s, _ = proj3.shape
    tm = _tile(s, SEQ_TILE)
    return pl.pallas_call(
        functools.partial(_pool_kernel, tm=tm, s=s),
        grid=(b, s // tm),
        in_specs=_halo_specs(tm, s, BRANCH_W, OFF_U // BRANCH_W) + [
            pl.BlockSpec((None, tm, BRANCH_W), lambda bi, i: (bi, i, OFF_GPOOL // BRANCH_W)),
            pl.BlockSpec((len(POOL_WINDOWS), POOL_GROUP, POOL_GROUP), lambda bi, i: (0, 0, 0)),
            pl.BlockSpec((1, BRANCH_W), lambda bi, i: (0, 0))],
        out_specs=pl.BlockSpec((None, tm, BRANCH_W), lambda bi, i: (bi, i, 0)),
        out_shape=jax.ShapeDtypeStruct((b, s, BRANCH_W), BF16),
        scratch_shapes=[pltpu.VMEM((tm + 2 * HALO, BRANCH_W), F32)],
        compiler_params=_cparams(("parallel", "parallel")),
        name="pool",
    )(proj3, proj3, proj3, proj3, pool_w, pool_scale.reshape(1, BRANCH_W))


def _merge_kernel(*refs):
    brs, ws, gs, o_ref = refs[0:4], refs[4:8], refs[8:12], refs[12]
    acc = None
    for br, w, g in zip(brs, ws, gs):
        term = jax.nn.sigmoid(g[...].astype(F32)) * jnp.dot(br[...], w[...], preferred_element_type=F32)
        acc = term if acc is None else acc + term
    o_ref[...] = acc.astype(BF16)


def _merge(branches, w_branch, proj):
    t = proj.shape[0]
    tm = _tile(t, 1024)
    tn = 512
    br_spec = pl.BlockSpec((tm, BRANCH_W), lambda i, j: (i, 0))
    w_specs = [pl.BlockSpec((None, BRANCH_W, tn), lambda i, j, k=k: (k, 0, j)) for k in range(N_BRANCH)]
    g_specs = [pl.BlockSpec((tm, tn), lambda i, j, k=k: (i, (OFF_MG + k * D_MODEL) // tn + j))
               for k in range(N_BRANCH)]
    return pl.pallas_call(
        _merge_kernel,
        grid=(t // tm, D_MODEL // tn),
        in_specs=[br_spec] * N_BRANCH + w_specs + g_specs,
        out_specs=pl.BlockSpec((tm, tn), lambda i, j: (i, j)),
        out_shape=jax.ShapeDtypeStruct((t, D_MODEL), BF16),
        compiler_params=_cparams(("parallel", "parallel")),
        name="merge",
    )(*branches, *([w_branch] * N_BRANCH), *([proj] * N_BRANCH))


def _out_kernel(m_ref, w_ref, x_ref, fn_ref, o_ref, *, final):
    y = x_ref[...] + jnp.dot(m_ref[...], w_ref[...], preferred_element_type=F32)
    if final:
        y = y * lax.rsqrt(jnp.mean(y * y, axis=-1, keepdims=True) + EPS) * fn_ref[...]
    o_ref[...] = y


def _out_proj(merged, w_out, x2, final_norm, final):
    t = x2.shape[0]
    tm = _tile(t, 512)
    return pl.pallas_call(
        functools.partial(_out_kernel, final=final),
        grid=(t // tm,),
        in_specs=[pl.BlockSpec((tm, D_MODEL), lambda i: (i, 0)),
                  pl.BlockSpec((D_MODEL, D_MODEL), lambda i: (0, 0)),
                  pl.BlockSpec((tm, D_MODEL), lambda i: (i, 0)),
                  pl.BlockSpec((1, D_MODEL), lambda i: (0, 0))],
        out_specs=pl.BlockSpec((tm, D_MODEL), lambda i: (i, 0)),
        out_shape=jax.ShapeDtypeStruct((t, D_MODEL), F32),
        compiler_params=_cparams(("parallel",)),
        name="out_proj",
    )(merged, w_out, x2, final_norm.reshape(1, D_MODEL))


def _pack_w_in(w_in):
    wt = jnp.swapaxes(w_in, 1, 2)
    depth = wt.shape[0]
    zeros = lambda n: jnp.zeros((depth, n, D_MODEL), w_in.dtype)
    half = MLA_ROPE // 2
    kr0 = IN_OFFS[2]
    bc0 = IN_OFFS[9] + BRANCH_W
    w_tail = jnp.concatenate([
        wt[:, IN_OFFS[0]:IN_OFFS[1]],
        wt[:, bc0:IN_OFFS[10]],
        wt[:, IN_OFFS[1]:IN_OFFS[2]],
        wt[:, kr0:kr0 + MLA_ROPE], zeros(LANE - MLA_ROPE),
        wt[:, kr0 + half:kr0 + MLA_ROPE], wt[:, kr0:kr0 + half], zeros(LANE - MLA_ROPE),
    ], axis=1)
    dt0 = IN_OFFS[10]
    w_dt = jnp.concatenate([wt[:, dt0:dt0 + SSD_HEADS], zeros(LANE - SSD_HEADS),
                            wt[:, dt0 + SSD_HEADS:dt0 + 2 * SSD_HEADS], zeros(LANE - SSD_HEADS)], axis=1)
    return wt.astype(BF16), w_tail.astype(BF16), w_dt.astype(BF16)


def _pack_mla(w_uq, w_ukv):
    half = MLA_ROPE // 2
    wq3 = w_uq.reshape(MLA_Q_LORA, MLA_HEADS, MLA_NOPE + MLA_ROPE)
    zq = jnp.zeros((MLA_Q_LORA, MLA_HEADS, LANE - MLA_ROPE), w_uq.dtype)
    wq = jnp.concatenate([wq3, zq], axis=2).reshape(MLA_Q_LORA, MLA_HEADS * MLA_QK).astype(BF16)
    wqp = jnp.concatenate([wq3[:, :, MLA_NOPE + half:], wq3[:, :, MLA_NOPE:MLA_NOPE + half], zq],
                          axis=2).reshape(MLA_Q_LORA, MLA_HEADS * LANE).astype(BF16)
    wkv3 = w_ukv.reshape(MLA_KV_LORA, MLA_HEADS, MLA_NOPE + MLA_V)
    wk = wkv3[:, :, :MLA_NOPE].reshape(MLA_KV_LORA, MLA_HEADS * MLA_NOPE).astype(BF16)
    wv = wkv3[:, :, MLA_NOPE:].reshape(MLA_KV_LORA, MLA_HEADS * MLA_V).astype(BF16)
    return wq, wqp, wk, wv


def _rope_tables(s):
    def angles(rot):
        half = rot // 2
        inv_freq = jnp.power(ROPE_THETA, -jnp.arange(half, dtype=F32) * 2.0 / rot)
        ang = jnp.arange(s, dtype=F32)[:, None] * inv_freq[None, :]
        return jnp.cos(ang), jnp.sin(ang)

    c, sn = angles(MLA_ROPE)
    z = jnp.zeros((s, LANE - MLA_ROPE), F32)
    mla_cos = jnp.concatenate([c, c, z], axis=1)
    mla_sin = jnp.concatenate([-sn, sn, z], axis=1)
    c, sn = angles(DIFF_ROT)
    zh = jnp.zeros_like(sn)
    rest = LANE - DIFF_ROT
    d_cos = jnp.concatenate([c, c, jnp.ones((s, rest), F32)], axis=1)
    d_sa = jnp.concatenate([zh, sn, jnp.zeros((s, rest), F32)], axis=1)
    d_sb = jnp.concatenate([-sn, zh, jnp.zeros((s, rest), F32)], axis=1)
    return mla_cos, mla_sin, d_cos, d_sa, d_sb


def _trunk(x, w_inproj, packed, norm_w, mla_q_norm, mla_kv_norm, diff_lambda, diff_subln, ssd_conv_w, ssd_conv_b,
           ssd_dt_bias, ssd_a_log, ssd_d, ssd_norm, pool_w, pool_scale, w_branch, w_out, final_norm):
    b, s, _ = x.shape
    t = b * s
    mla_cos, mla_sin, d_cos, d_sa, d_sb = _rope_tables(s)
    x2 = x.reshape(t, D_MODEL)
    for l in range(DEPTH):
        wq, wqp, wk, wv = packed[l]
        lambda_init = 0.8 - 0.6 * math.exp(-0.3 * l)
        proj, dt = _inproj(x2, norm_w[l], *w_inproj, l)
        proj3 = proj.reshape(b, s, NP_OUT)
        q, k, v = _mla_prep(proj, s, mla_cos, mla_sin, mla_q_norm[l], mla_kv_norm[l], wq, wqp, wk, wv)
        br_mla = _mla_attn(q.reshape(b, s, -1), k.reshape(b, s, -1), v.reshape(b, s, -1), proj3)
        dq, dk = _diff_prep(proj, s, d_cos, d_sa, d_sb)
        br_diff = _diff_attn(dq.reshape(b, s, -1), dk.reshape(b, s, -1), proj3, diff_lambda[l],
                             diff_subln[l], lambda_init)
        xs_act, bc_act = _ssd_conv(proj3, ssd_conv_w[l], ssd_conv_b[l])
        br_ssd = _ssd_scan(xs_act, bc_act, dt.reshape(b, s, DT_W), proj3, ssd_dt_bias[l], ssd_a_log[l],
                           ssd_d[l], ssd_norm[l])
        br_pool = _pool(proj3, pool_w[l].astype(BF16), pool_scale[l])
        branches = [a.reshape(t, BRANCH_W) for a in (br_mla, br_diff, br_ssd, br_pool)]
        merged = _merge(branches, w_branch[l].astype(BF16), proj)
        x2 = _out_proj(merged, w_out[l].astype(BF16), x2, final_norm, final=(l == DEPTH - 1))
    return x2.reshape(b, s, D_MODEL)


def kernel(x_prompt, x_sample, norm_w, w_in, mla_q_norm, mla_w_uq, mla_kv_norm, mla_w_ukv, diff_lambda,
           diff_subln, ssd_conv_w, ssd_conv_b, ssd_dt_bias, ssd_a_log, ssd_d, ssd_norm, pool_w, pool_scale,
           w_branch, w_out, final_norm):
    w_inproj = _pack_w_in(w_in)
    packed = [_pack_mla(mla_w_uq[l], mla_w_ukv[l]) for l in range(DEPTH)]
    rest = (norm_w, mla_q_norm, mla_kv_norm, diff_lambda, diff_subln, ssd_conv_w, ssd_conv_b, ssd_dt_bias,
            ssd_a_log, ssd_d, ssd_norm, pool_w, pool_scale, w_branch, w_out, final_norm)
    return (_trunk(x_prompt, w_inproj, packed, *rest), _trunk(x_sample, w_inproj, packed, *rest))
```

```python
import functools
import math

import numpy as np
import jax
import jax.numpy as jnp
from jax import lax
from jax.experimental import pallas as pl
from jax.experimental.pallas import tpu as pltpu

F32 = jnp.float32
BF16 = jnp.bfloat16
HIGHEST = lax.Precision.HIGHEST

D_MODEL = 2048
DEPTH = 2
BRANCH_W = D_MODEL // 2
N_BRANCH = 4
ROPE_THETA = 500000.0
EPS = 1e-6
MLA_HEADS = 8
MLA_NOPE = 128
MLA_ROPE = 64
MLA_V = BRANCH_W // MLA_HEADS
MLA_Q_LORA = 512
MLA_KV_LORA = 256
DIFF_HEADS = 4
DIFF_HD = BRANCH_W // (2 * DIFF_HEADS)
DIFF_ROT = DIFF_HD // 4
SSD_P = 64
SSD_HEADS = BRANCH_W // SSD_P
SSD_N = 128
SSD_G = 2
SSD_CONV = 4
SSD_CHUNK = 128
SSD_BLOCK = 1024
SEQ_TILE = 1024
SSD_CONV_DIM = BRANCH_W + 2 * SSD_G * SSD_N
POOL_WINDOWS = (2, 4, 8, 16)
POOL_GROUP = BRANCH_W // 4
IN_SIZES = (MLA_Q_LORA, MLA_KV_LORA, MLA_ROPE, BRANCH_W, BRANCH_W, BRANCH_W, BRANCH_W, BRANCH_W,
            BRANCH_W, SSD_CONV_DIM, 2 * SSD_HEADS, BRANCH_W, BRANCH_W, N_BRANCH * D_MODEL)
IN_OFFS = tuple(int(v) for v in np.cumsum((0,) + IN_SIZES))

LANE = 128
HALO = 16
VMEM_LIMIT = 56 * 1024 * 1024

OFF_GMLA = 0
OFF_DQ = 1 * BRANCH_W
OFF_DK = 2 * BRANCH_W
OFF_DV = 3 * BRANCH_W
OFF_GDIFF = 4 * BRANCH_W
OFF_Z = 5 * BRANCH_W
OFF_XS = 6 * BRANCH_W
OFF_U = 7 * BRANCH_W
OFF_GPOOL = 8 * BRANCH_W
OFF_MG = 9 * BRANCH_W
OFF_CQ = OFF_MG + N_BRANCH * D_MODEL
OFF_BC = OFF_CQ + MLA_Q_LORA
OFF_CKV = OFF_BC + 2 * SSD_G * SSD_N
OFF_KR = OFF_CKV + MLA_KV_LORA
OFF_KRP = OFF_KR + LANE
NP_MAIN = OFF_KRP + LANE
DT_W = 2 * LANE

ATTN_TQ = 2048
ATTN_SUB = 256
MLA_QK = 2 * LANE
LOG2E = math.log2(math.e)


def _cparams(sem, vmem=VMEM_LIMIT):
    return pltpu.CompilerParams(dimension_semantics=sem, vmem_limit_bytes=vmem)


def _tile(n, pref):
    t = min(n, pref)
    while n % t:
        t //= 2
    return t


def _silu(x):
    return x * jax.nn.sigmoid(x)


def _softplus(x):
    return jnp.maximum(x, 0.0) + jnp.log1p(jnp.exp(-jnp.abs(x)))


_NT = (((1,), (1,)), ((), ()))
INPROJ_TN = 1024
TAIL_W = NP_MAIN - OFF_CQ

_N_TILE_A = OFF_U // INPROJ_TN
_N_TILE_MAIN = OFF_CQ // INPROJ_TN
_N_TILE = _N_TILE_MAIN + 2
NP_OUT = _N_TILE * INPROJ_TN


def _inproj_src_row(j):
    return jnp.where(j < _N_TILE_A, IN_OFFS[3] + j * INPROJ_TN,
                     jnp.where(j < _N_TILE_MAIN, IN_OFFS[11] + (j - _N_TILE_A) * INPROJ_TN, IN_OFFS[0]))


def _inproj_kernel(x_ref, nw_ref, w_ref, wtail_ref, wdt_ref, o_ref, dt_ref, h_ref):
    j = pl.program_id(1)
    tn = INPROJ_TN

    @pl.when(j == 0)
    def _():
        x = x_ref[...]
        y = x * lax.rsqrt(jnp.mean(x * x, axis=-1, keepdims=True) + EPS)
        hb = (y * nw_ref[...]).astype(BF16)
        h_ref[...] = hb
        dt_ref[...] = lax.dot_general(hb, wdt_ref[...], _NT, preferred_element_type=F32)

    @pl.when(j < _N_TILE_MAIN)
    def _():
        o_ref[...] = lax.dot_general(h_ref[...], w_ref[...], _NT, preferred_element_type=F32).astype(BF16)

    @pl.when(j == _N_TILE_MAIN)
    def _():
        o_ref[...] = lax.dot_general(h_ref[...], wtail_ref[0:tn, :], _NT,
                                     preferred_element_type=F32).astype(BF16)

    @pl.when(j == _N_TILE_MAIN + 1)
    def _():
        rest = TAIL_W - tn
        o_ref[:, 0:rest] = lax.dot_general(h_ref[...], wtail_ref[tn:TAIL_W, :], _NT,
                                           preferred_element_type=F32).astype(BF16)
        o_ref[:, rest:] = jnp.zeros((o_ref.shape[0], tn - rest), BF16)


def _inproj(x2, norm_w, wt, w_tail, w_dt, l):
    t = x2.shape[0]
    tm = _tile(t, 1024)
    tn = INPROJ_TN
    return pl.pallas_call(
        _inproj_kernel,
        grid=(t // tm, _N_TILE),
        in_specs=[
            pl.BlockSpec((tm, D_MODEL), lambda i, j: (i, 0)),
            pl.BlockSpec((1, D_MODEL), lambda i, j: (0, 0)),
            pl.BlockSpec((pl.Element(tn), pl.Element(D_MODEL)),
                         lambda i, j: (pl.multiple_of(l * IN_OFFS[-1] + _inproj_src_row(j), HALO), 0)),
            pl.BlockSpec((None, TAIL_W, D_MODEL), lambda i, j: (l, 0, 0)),
            pl.BlockSpec((None, DT_W, D_MODEL), lambda i, j: (l, 0, 0)),
        ],
        out_specs=[
            pl.BlockSpec((tm, tn), lambda i, j: (i, j)),
            pl.BlockSpec((tm, DT_W), lambda i, j: (i, 0)),
        ],
        out_shape=[jax.ShapeDtypeStruct((t, NP_OUT), BF16), jax.ShapeDtypeStruct((t, DT_W), F32)],
        scratch_shapes=[pltpu.VMEM((tm, D_MODEL), BF16)],
        compiler_params=_cparams(("parallel", "arbitrary")),
        name="inproj",
    )(x2, norm_w.reshape(1, D_MODEL), wt.reshape(-1, D_MODEL), w_tail, w_dt)


def _mla_prep_kernel(cq_ref, ckv_ref, kr_ref, krp_ref, cos_ref, sin_ref, qn_ref, kvn_ref,
                     wq_ref, wqp_ref, wk_ref, wv_ref, q_ref, k_ref, v_ref):
    cos = cos_ref[...]
    sin = sin_ref[...]
    cq = cq_ref[...].astype(F32)
    ncq = (cq * lax.rsqrt(jnp.mean(cq * cq, axis=-1, keepdims=True) + EPS) * qn_ref[...]).astype(BF16)
    qm = jnp.dot(ncq, wq_ref[...], preferred_element_type=F32)
    qp = jnp.dot(ncq, wqp_ref[...], preferred_element_type=F32)
    ckv = ckv_ref[...].astype(F32)
    nkv = (ckv * lax.rsqrt(jnp.mean(ckv * ckv, axis=-1, keepdims=True) + EPS) * kvn_ref[...]).astype(BF16)
    kn = jnp.dot(nkv, wk_ref[...], preferred_element_type=F32)
    v_ref[...] = jnp.dot(nkv, wv_ref[...], preferred_element_type=F32).astype(BF16)
    kpe = (kr_ref[...].astype(F32) * cos + krp_ref[...].astype(F32) * sin).astype(BF16)
    qscale = (MLA_NOPE + MLA_ROPE) ** -0.5 * LOG2E
    for h in range(MLA_HEADS):
        lo = h * MLA_QK
        q_ref[:, lo:lo + LANE] = (qm[:, lo:lo + LANE] * qscale).astype(BF16)
        qr = qm[:, lo + LANE:lo + 2 * LANE] * cos + qp[:, h * LANE:(h + 1) * LANE] * sin
        q_ref[:, lo + LANE:lo + 2 * LANE] = (qr * qscale).astype(BF16)
        k_ref[:, lo:lo + LANE] = kn[:, h * LANE:(h + 1) * LANE].astype(BF16)
        k_ref[:, lo + LANE:lo + 2 * LANE] = kpe


def _mla_prep(proj, s, cos_t, sin_t, q_norm, kv_norm, wq, wqp, wk, wv):
    t = proj.shape[0]
    tm = _tile(s, SEQ_TILE)
    ns = s // tm
    hq = MLA_HEADS * MLA_QK
    const = lambda shape: pl.BlockSpec(shape, lambda i: (0, 0))
    return pl.pallas_call(
        _mla_prep_kernel,
        grid=(t // tm,),
        in_specs=[
            pl.BlockSpec((tm, MLA_Q_LORA), lambda i: (i, OFF_CQ // MLA_Q_LORA)),
            pl.BlockSpec((tm, MLA_KV_LORA), lambda i: (i, OFF_CKV // MLA_KV_LORA)),
            pl.BlockSpec((tm, LANE), lambda i: (i, OFF_KR // LANE)),
            pl.BlockSpec((tm, LANE), lambda i: (i, OFF_KRP // LANE)),
            pl.BlockSpec((tm, LANE), lambda i: (i % ns, 0)),
            pl.BlockSpec((tm, LANE), lambda i: (i % ns, 0)),
            const((1, MLA_Q_LORA)),
            const((1, MLA_KV_LORA)),
            const((MLA_Q_LORA, hq)),
            const((MLA_Q_LORA, MLA_HEADS * LANE)),
            const((MLA_KV_LORA, MLA_HEADS * MLA_NOPE)),
            const((MLA_KV_LORA, MLA_HEADS * MLA_V)),
        ],
        out_specs=[
            pl.BlockSpec((tm, hq), lambda i: (i, 0)),
            pl.BlockSpec((tm, hq), lambda i: (i, 0)),
            pl.BlockSpec((tm, MLA_HEADS * MLA_V), lambda i: (i, 0)),
        ],
        out_shape=[jax.ShapeDtypeStruct((t, hq), BF16), jax.ShapeDtypeStruct((t, hq), BF16),
                   jax.ShapeDtypeStruct((t, MLA_HEADS * MLA_V), BF16)],
        compiler_params=_cparams(("parallel",)),
        name="mla_prep",
    )(proj, proj, proj, proj, cos_t, sin_t, q_norm.reshape(1, -1), kv_norm.reshape(1, -1), wq, wqp, wk, wv)


def _scores(q, k_ref):
    return lax.dot_general(q, k_ref[...], (((1,), (1,)), ((), ())), preferred_element_type=F32)


def _softmax_pv_from(sc, v_ref):
    m = jnp.max(sc, axis=-1, keepdims=True)
    p = jnp.exp2(sc - m)
    l = jnp.sum(p, axis=-1, keepdims=True)
    return jnp.dot(p.astype(BF16), v_ref[...], preferred_element_type=F32), l


def _mla_attn_kernel(q_ref, k_ref, v_ref, g_ref, o_ref, sx_ref, sy_ref, *, sub):
    bufs = (sx_ref, sy_ref)
    starts = list(range(0, q_ref.shape[0], sub))
    bufs[0][...] = _scores(q_ref[0:sub, :], k_ref)
    for i, r in enumerate(starts):
        if i + 1 < len(starts):
            r2 = starts[i + 1]
            bufs[(i + 1) % 2][...] = _scores(q_ref[r2:r2 + sub, :], k_ref)
        acc, l = _softmax_pv_from(bufs[i % 2][...], v_ref)
        o_ref[r:r + sub, :] = ((acc / l) * _silu(g_ref[r:r + sub, :].astype(F32))).astype(BF16)


def _mla_attn(q, k, v, proj3):
    b, s, _ = q.shape
    tq = _tile(s, ATTN_TQ)
    return pl.pallas_call(
        functools.partial(_mla_attn_kernel, sub=min(tq, ATTN_SUB)),
        grid=(b, MLA_HEADS, s // tq),
        in_specs=[
            pl.BlockSpec((None, tq, MLA_QK), lambda bi, h, qi: (bi, qi, h)),
            pl.BlockSpec((None, s, MLA_QK), lambda bi, h, qi: (bi, 0, h)),
            pl.BlockSpec((None, s, MLA_V), lambda bi, h, qi: (bi, 0, h)),
            pl.BlockSpec((None, tq, MLA_V), lambda bi, h, qi: (bi, qi, OFF_GMLA // MLA_V + h)),
        ],
        out_specs=pl.BlockSpec((None, tq, MLA_V), lambda bi, h, qi: (bi, qi, h)),
        out_shape=jax.ShapeDtypeStruct((b, s, BRANCH_W), BF16),
        scratch_shapes=[pltpu.VMEM((min(tq, ATTN_SUB), s), F32)] * 2,
        compiler_params=_cparams(("parallel", "parallel", "parallel")),
        name="mla_attn",
    )(q, k, v, proj3)


def _diff_prep_kernel(dq_ref, dk_ref, cos_ref, sa_ref, sb_ref, q_ref, k_ref):
    cos = cos_ref[...]
    sa = sa_ref[...]
    sb = sb_ref[...]
    half = DIFF_ROT // 2
    qscale = DIFF_HD ** -0.5 * LOG2E
    for h in range(2 * DIFF_HEADS):
        sl = slice(h * DIFF_HD, (h + 1) * DIFF_HD)
        xq = dq_ref[:, sl].astype(F32)
        xk = dk_ref[:, sl].astype(F32)
        rq = xq * cos + pltpu.roll(xq, half, 1) * sa + pltpu.roll(xq, DIFF_HD - half, 1) * sb
        rk = xk * cos + pltpu.roll(xk, half, 1) * sa + pltpu.roll(xk, DIFF_HD - half, 1) * sb
        q_ref[:, sl] = (rq * qscale).astype(BF16)
        k_ref[:, sl] = rk.astype(BF16)


def _diff_prep(proj, s, cos_t, sa_t, sb_t):
    t = proj.shape[0]
    tm = _tile(s, SEQ_TILE)
    ns = s // tm
    tab = pl.BlockSpec((tm, LANE), lambda i: (i % ns, 0))
    return pl.pallas_call(
        _diff_prep_kernel,
        grid=(t // tm,),
        in_specs=[
            pl.BlockSpec((tm, BRANCH_W), lambda i: (i, OFF_DQ // BRANCH_W)),
            pl.BlockSpec((tm, BRANCH_W), lambda i: (i, OFF_DK // BRANCH_W)),
            tab, tab, tab,
        ],
        out_specs=[pl.BlockSpec((tm, BRANCH_W), lambda i: (i, 0))] * 2,
        out_shape=[jax.ShapeDtypeStruct((t, BRANCH_W), BF16)] * 2,
        compiler_params=_cparams(("parallel",)),
        name="diff_prep",
    )(proj, proj, cos_t, sa_t, sb_t)


def _diff_attn_kernel(q1_ref, q2_ref, k1_ref, k2_ref, v_ref, g_ref, lam_ref, sub_ref, o_ref, sx_ref, sy_ref, *,
                      sub, lambda_init):
    lp = lam_ref[...]
    lam = (jnp.exp(jnp.sum(lp[0:1] * lp[1:2], axis=-1, keepdims=True))
           - jnp.exp(jnp.sum(lp[2:3] * lp[3:4], axis=-1, keepdims=True)) + lambda_init)
    bufs = (sx_ref, sy_ref)
    jobs = [(qr, kr, r) for r in range(0, q1_ref.shape[0], sub) for qr, kr in ((q1_ref, k1_ref), (q2_ref, k2_ref))]
    scores = lambda n: _scores(jobs[n][0][jobs[n][2]:jobs[n][2] + sub, :], jobs[n][1])
    bufs[0][...] = scores(0)
    parts = []
    for n, (_, _, r) in enumerate(jobs):
        if n + 1 < len(jobs):
            bufs[(n + 1) % 2][...] = scores(n + 1)
        acc, l = _softmax_pv_from(bufs[n % 2][...], v_ref)
        parts.append(acc / l)
        if n % 2 == 0:
            continue
        o = parts[n - 1] - lam * parts[n]
        o = o * lax.rsqrt(jnp.mean(o * o, axis=-1, keepdims=True) + EPS) * sub_ref[...]
        o = o * (1.0 - lambda_init)
        o_ref[r:r + sub, :] = (o * _silu(g_ref[r:r + sub, :].astype(F32))).astype(BF16)


def _diff_attn(q, k, proj3, lam_params, subln, lambda_init):
    b, s, _ = q.shape
    tq = _tile(s, ATTN_TQ)
    dv = 2 * DIFF_HD
    return pl.pallas_call(
        functools.partial(_diff_attn_kernel, sub=min(tq, ATTN_SUB), lambda_init=lambda_init),
        grid=(b, DIFF_HEADS, s // tq),
        in_specs=[
            pl.BlockSpec((None, tq, DIFF_HD), lambda bi, h, qi: (bi, qi, 2 * h)),
            pl.BlockSpec((None, tq, DIFF_HD), lambda bi, h, qi: (bi, qi, 2 * h + 1)),
            pl.BlockSpec((None, s, DIFF_HD), lambda bi, h, qi: (bi, 0, 2 * h)),
            pl.BlockSpec((None, s, DIFF_HD), lambda bi, h, qi: (bi, 0, 2 * h + 1)),
            pl.BlockSpec((None, s, dv), lambda bi, h, qi: (bi, 0, OFF_DV // dv + h)),
            pl.BlockSpec((None, tq, dv), lambda bi, h, qi: (bi, qi, OFF_GDIFF // dv + h)),
            pl.BlockSpec((4, DIFF_HD), lambda bi, h, qi: (0, 0)),
            pl.BlockSpec((1, dv), lambda bi, h, qi: (0, 0)),
        ],
        out_specs=pl.BlockSpec((None, tq, dv), lambda bi, h, qi: (bi, qi, h)),
        out_shape=jax.ShapeDtypeStruct((b, s, BRANCH_W), BF16),
        scratch_shapes=[pltpu.VMEM((min(tq, ATTN_SUB), s), F32)] * 2,
        compiler_params=_cparams(("parallel", "parallel", "parallel")),
        name="diff_attn",
    )(q, q, k, k, proj3, proj3, lam_params, subln.reshape(1, dv))


def _ssd_conv_kernel(xs_ref, xsp_ref, xsn_ref, bc_ref, bcp_ref, bcn_ref, w_ref, b_ref,
                     oxs_ref, obc_ref, ext_ref, *, tm):
    i = pl.program_id(1)
    has_prev = i > 0
    has_next = i < pl.num_programs(1) - 1
    pad_l = SSD_CONV // 2

    def conv(cur_ref, prev_ref, next_ref, out_ref, c0, width):
        ext_ref[0:HALO, 0:width] = jnp.where(has_prev, prev_ref[...].astype(F32), 0.0)
        ext_ref[HALO:HALO + tm, 0:width] = cur_ref[...].astype(F32)
        ext_ref[HALO + tm:, 0:width] = jnp.where(has_next, next_ref[...].astype(F32), 0.0)
        acc = b_ref[:, c0:c0 + width]
        for j in range(SSD_CONV):
            off = HALO - pad_l + j
            acc = acc + ext_ref[off:off + tm, 0:width] * w_ref[j:j + 1, c0:c0 + width]
        out_ref[...] = _silu(acc).astype(BF16)

    conv(xs_ref, xsp_ref, xsn_ref, oxs_ref, 0, BRANCH_W)
    conv(bc_ref, bcp_ref, bcn_ref, obc_ref, BRANCH_W, SSD_CONV_DIM - BRANCH_W)


def _halo_specs(tm, s, width, col):
    r = tm // HALO
    last = s // HALO - 1
    cur = pl.BlockSpec((None, tm, width), lambda bi, i: (bi, i, col))
    prev = pl.BlockSpec((None, HALO, width), lambda bi, i: (bi, jnp.maximum(i * r - 1, 0), col))
    nxt = pl.BlockSpec((None, HALO, width), lambda bi, i: (bi, jnp.minimum((i + 1) * r, last), col))
    return [cur, prev, nxt]


def _ssd_conv(proj3, conv_w, conv_b):
    b, s, _ = proj3.shape
    tm = _tile(s, SEQ_TILE)
    wbc = SSD_CONV_DIM - BRANCH_W
    return pl.pallas_call(
        functools.partial(_ssd_conv_kernel, tm=tm),
        grid=(b, s // tm),
        in_specs=(_halo_specs(tm, s, BRANCH_W, OFF_XS // BRANCH_W) + _halo_specs(tm, s, wbc, OFF_BC // wbc)
                  + [pl.BlockSpec((SSD_CONV, SSD_CONV_DIM), lambda bi, i: (0, 0)),
                     pl.BlockSpec((1, SSD_CONV_DIM), lambda bi, i: (0, 0))]),
        out_specs=[pl.BlockSpec((None, tm, BRANCH_W), lambda bi, i: (bi, i, 0)),
                   pl.BlockSpec((None, tm, wbc), lambda bi, i: (bi, i, 0))],
        out_shape=[jax.ShapeDtypeStruct((b, s, BRANCH_W), BF16), jax.ShapeDtypeStruct((b, s, wbc), BF16)],
        scratch_shapes=[pltpu.VMEM((tm + 2 * HALO, BRANCH_W), F32)],
        compiler_params=_cparams(("parallel", "parallel")),
        name="ssd_conv",
    )(proj3, proj3, proj3, proj3, proj3, proj3, conv_w, conv_b.reshape(1, -1))


def _ssd_chunk(xs_b, bc, dt_raw, bias, alog, st_ref, reverse):
    ln = SSD_CHUNK
    gw = BRANCH_W // SSD_G
    xs = xs_b.astype(F32)
    dt = _softplus(dt_raw + bias)
    da = dt * (-jnp.exp(alog))
    ri = lax.broadcasted_iota(jnp.int32, (ln, ln), 0)
    ci = lax.broadcasted_iota(jnp.int32, (ln, ln), 1)
    tri = (ci >= ri) if reverse else (ri >= ci)
    cs = jnp.dot(tri.astype(F32), da, precision=HIGHEST, preferred_element_type=F32)
    cs_end = cs[0:1, :] if reverse else cs[ln - 1:ln, :]
    er = lax.broadcasted_iota(jnp.int32, (2 * LANE, BRANCH_W), 0)
    ec = lax.broadcasted_iota(jnp.int32, (2 * LANE, BRANCH_W), 1)
    expand = jnp.where(ec // SSD_P == er % LANE, 1.0, 0.0).astype(BF16)
    cols = jnp.concatenate([dt, jnp.exp(cs), jnp.exp(cs_end - cs)], axis=0)
    hi = cols.astype(BF16)
    lo = (cols - hi.astype(F32)).astype(BF16)
    wide = jnp.dot(jnp.concatenate([hi, lo], axis=1), expand, preferred_element_type=F32)
    dt_e = wide[0:ln]
    dec_out_e = wide[ln:2 * ln]
    dec_st_e = wide[2 * ln:3 * ln]
    tot_e = dec_out_e[0:1, :] if reverse else dec_out_e[ln - 1:ln, :]
    cs_t = cs.T
    xd = xs * dt_e
    xdb = xd.astype(BF16)
    xst = (xd * dec_st_e).astype(BF16)
    low = lax.broadcasted_iota(jnp.int32, (ln, LANE), 1) < SSD_P
    zero = jnp.zeros((ln, LANE), BF16)
    outs = []
    for g in range(SSD_G):
        bg = bc[:, g * SSD_N:(g + 1) * SSD_N]
        cg = bc[:, (SSD_G + g) * SSD_N:(SSD_G + g + 1) * SSD_N]
        cb = lax.dot_general(cg, bg, (((1,), (1,)), ((), ())), preferred_element_type=F32)
        st = st_ref[g]
        y_off = jnp.dot(cg, st.astype(BF16), preferred_element_type=F32) * dec_out_e[:, g * gw:(g + 1) * gw]
        bt = bg.astype(F32).T.astype(BF16)
        st_ref[g] = (st * tot_e[:, g * gw:(g + 1) * gw]
                     + jnp.dot(bt, xst[:, g * gw:(g + 1) * gw], preferred_element_type=F32))
        for kp in range(gw // LANE):
            mats = []
            for hh in range(2):
                hd = g * (SSD_HEADS // SSD_G) + 2 * kp + hh
                seg = cs[:, hd:hd + 1] - cs_t[hd:hd + 1, :]
                lm = jnp.exp(jnp.where(tri, seg, -jnp.inf))
                mats.append((cb * lm).astype(BF16))
            lhs = jnp.concatenate(mats, axis=1)
            col = g * gw + kp * LANE
            xp = xdb[:, col:col + LANE]
            rhs = jnp.concatenate([jnp.where(low, xp, zero), jnp.where(low, zero, xp)], axis=0)
            y_diag = jnp.dot(lhs, rhs, preferred_element_type=F32)
            outs.append(y_diag + y_off[:, kp * LANE:(kp + 1) * LANE])
    return jnp.concatenate(outs, axis=1), xs


def _chunk_rows(n_rows, reverse):
    starts = range(0, n_rows, SSD_CHUNK)
    return [slice(r, r + SSD_CHUNK) for r in (reversed(starts) if reverse else starts)]


def _ssd_fwd_kernel(xs_ref, bc_ref, dt_ref, bias_ref, alog_ref, y_ref, st_ref):
    @pl.when(pl.program_id(1) == 0)
    def _():
        st_ref[...] = jnp.zeros_like(st_ref)

    for rows in _chunk_rows(xs_ref.shape[0], reverse=False):
        y, _ = _ssd_chunk(xs_ref[rows, :], bc_ref[rows, :], dt_ref[rows, :], bias_ref[...], alog_ref[...],
                          st_ref, reverse=False)
        y_ref[rows, :] = y


def _ssd_bwd_kernel(xs_ref, bc_ref, dt_ref, bias_ref, alog_ref, yf_ref, z_ref, d_ref, nw_ref,
                    o_ref, st_ref):
    @pl.when(pl.program_id(1) == 0)
    def _():
        st_ref[...] = jnp.zeros_like(st_ref)

    gw = BRANCH_W // SSD_G
    for rows in _chunk_rows(xs_ref.shape[0], reverse=True):
        yb, xs = _ssd_chunk(xs_ref[rows, :], bc_ref[rows, :], dt_ref[rows, :], bias_ref[...], alog_ref[...],
                            st_ref, reverse=True)
        y = yf_ref[rows, :] + yb + xs * d_ref[...]
        y = y * _silu(z_ref[rows, :].astype(F32))
        for g in range(SSD_G):
            yg = y[:, g * gw:(g + 1) * gw]
            yn = yg * lax.rsqrt(jnp.mean(yg * yg, axis=-1, keepdims=True) + EPS)
            o_ref[rows, g * gw:(g + 1) * gw] = (yn * nw_ref[:, g * gw:(g + 1) * gw]).astype(BF16)


def _ssd_scan(xs_act, bc_act, dt3, proj3, dt_bias, a_log, d_skip, norm_w):
    b, s, _ = xs_act.shape
    tb = _tile(s, SSD_BLOCK)
    nc = s // tb
    wbc = SSD_CONV_DIM - BRANCH_W
    gw = BRANCH_W // SSD_G
    pad = lambda v: jnp.pad(v.astype(F32), (0, LANE - SSD_HEADS)).reshape(1, LANE)
    scratch = [pltpu.VMEM((SSD_G, SSD_N, gw), F32)]
    vec = lambda w: pl.BlockSpec((1, w), lambda bi, c: (0, 0))

    def specs(cmap, d):
        return [pl.BlockSpec((None, tb,BRANCH_W), lambda bi, c: (bi, cmap(c), 0)),
                pl.BlockSpec((None, tb,wbc), lambda bi, c: (bi, cmap(c), 0)),
                pl.BlockSpec((None, tb,LANE), lambda bi, c: (bi, cmap(c), d)),
                vec(LANE), vec(LANE)]

    y_f = pl.pallas_call(
        _ssd_fwd_kernel,
        grid=(b, nc),
        in_specs=specs(lambda c: c, 0),
        out_specs=pl.BlockSpec((None, tb,BRANCH_W), lambda bi, c: (bi, c, 0)),
        out_shape=jax.ShapeDtypeStruct((b, s, BRANCH_W), F32),
        scratch_shapes=scratch,
        compiler_params=_cparams(("parallel", "arbitrary")),
        name="ssd_fwd",
    )(xs_act, bc_act, dt3, pad(dt_bias[0]), pad(a_log[0]))

    rev = lambda c: nc - 1 - c
    return pl.pallas_call(
        _ssd_bwd_kernel,
        grid=(b, nc),
        in_specs=specs(rev, 1) + [
            pl.BlockSpec((None, tb,BRANCH_W), lambda bi, c: (bi, rev(c), 0)),
            pl.BlockSpec((None, tb,BRANCH_W), lambda bi, c: (bi, rev(c), OFF_Z // BRANCH_W)),
            vec(BRANCH_W), vec(BRANCH_W)],
        out_specs=pl.BlockSpec((None, tb,BRANCH_W), lambda bi, c: (bi, rev(c), 0)),
        out_shape=jax.ShapeDtypeStruct((b, s, BRANCH_W), BF16),
        scratch_shapes=scratch,
        compiler_params=_cparams(("parallel", "arbitrary")),
        name="ssd_bwd",
    )(xs_act, bc_act, dt3, pad(dt_bias[1]), pad(a_log[1]), y_f, proj3,
      jnp.repeat(d_skip.astype(F32), SSD_P).reshape(1, BRANCH_W), norm_w.reshape(1, BRANCH_W))


def _pool_kernel(u_ref, up_ref, un_ref, g_ref, w_ref, sc_ref, o_ref, ext_ref, *, tm, s):
    i = pl.program_id(1)
    cur = u_ref[...].astype(F32)
    ext_ref[0:HALO, :] = jnp.where(i > 0, up_ref[...].astype(F32), 0.0)
    ext_ref[HALO:HALO + tm, :] = cur
    ext_ref[HALO + tm:, :] = jnp.where(i < pl.num_programs(1) - 1, un_ref[...].astype(F32), 0.0)
    pos = i * tm + lax.broadcasted_iota(jnp.int32, (tm, 1), 0)
    for gi, w in enumerate(POOL_WINDOWS):
        lo = w // 2
        hi = w - 1 - lo
        sl = slice(gi * POOL_GROUP, (gi + 1) * POOL_GROUP)
        acc = ext_ref[HALO - lo:HALO - lo + tm, sl]
        for d in range(-lo + 1, hi + 1):
            acc = acc + ext_ref[HALO + d:HALO + d + tm, sl]
        cnt = (jnp.minimum(pos + hi + 1, s) - jnp.maximum(pos - lo, 0)).astype(F32)
        pooled = acc / cnt - cur[:, sl]
        mixed = jnp.dot(pooled.astype(BF16), w_ref[gi], preferred_element_type=F32)
        o_ref[:, sl] = (mixed * sc_ref[:, sl] * _silu(g_ref[:, sl].astype(F32))).astype(BF16)


def _pool(proj3, pool_w, pool_scale):
    b, s, _ = proj3.shape
    tm = _tile(s, SEQ_TILE)
    return pl.pallas_call(
        functools.partial(_pool_kernel, tm=tm, s=s),
        grid=(b, s // tm),
        in_specs=_halo_specs(tm, s, BRANCH_W, OFF_U // BRANCH_W) + [
            pl.BlockSpec((None, tm, BRANCH_W), lambda bi, i: (bi, i, OFF_GPOOL // BRANCH_W)),
            pl.BlockSpec((len(POOL_WINDOWS), POOL_GROUP, POOL_GROUP), lambda bi, i: (0, 0, 0)),
            pl.BlockSpec((1, BRANCH_W), lambda bi, i: (0, 0))],
        out_specs=pl.BlockSpec((None, tm, BRANCH_W), lambda bi, i: (bi, i, 0)),
        out_shape=jax.ShapeDtypeStruct((b, s, BRANCH_W), BF16),
        scratch_shapes=[pltpu.VMEM((tm + 2 * HALO, BRANCH_W), F32)],
        compiler_params=_cparams(("parallel", "parallel")),
        name="pool",
    )(proj3, proj3, proj3, proj3, pool_w, pool_scale.reshape(1, BRANCH_W))


def _merge_kernel(*refs):
    brs, ws, gs, o_ref = refs[0:4], refs[4:8], refs[8:12], refs[12]
    acc = None
    for br, w, g in zip(brs, ws, gs):
        term = jax.nn.sigmoid(g[...].astype(F32)) * jnp.dot(br[...], w[...], preferred_element_type=F32)
        acc = term if acc is None else acc + term
    o_ref[...] = acc.astype(BF16)


def _merge(branches, w_branch, proj):
    t = proj.shape[0]
    tm = _tile(t, 1024)
    tn = 512
    br_spec = pl.BlockSpec((tm, BRANCH_W), lambda i, j: (i, 0))
    w_specs = [pl.BlockSpec((None, BRANCH_W, tn), lambda i, j, k=k: (k, 0, j)) for k in range(N_BRANCH)]
    g_specs = [pl.BlockSpec((tm, tn), lambda i, j, k=k: (i, (OFF_MG + k * D_MODEL) // tn + j))
               for k in range(N_BRANCH)]
    return pl.pallas_call(
        _merge_kernel,
        grid=(t // tm, D_MODEL // tn),
        in_specs=[br_spec] * N_BRANCH + w_specs + g_specs,
        out_specs=pl.BlockSpec((tm, tn), lambda i, j: (i, j)),
        out_shape=jax.ShapeDtypeStruct((t, D_MODEL), BF16),
        compiler_params=_cparams(("parallel", "parallel")),
        name="merge",
    )(*branches, *([w_branch] * N_BRANCH), *([proj] * N_BRANCH))


def _out_kernel(m_ref, w_ref, x_ref, fn_ref, o_ref, *, final):
    y = x_ref[...] + jnp.dot(m_ref[...], w_ref[...], preferred_element_type=F32)
    if final:
        y = y * lax.rsqrt(jnp.mean(y * y, axis=-1, keepdims=True) + EPS) * fn_ref[...]
    o_ref[...] = y


def _out_proj(merged, w_out, x2, final_norm, final):
    t = x2.shape[0]
    tm = _tile(t, 512)
    return pl.pallas_call(
        functools.partial(_out_kernel, final=final),
        grid=(t // tm,),
        in_specs=[pl.BlockSpec((tm, D_MODEL), lambda i: (i, 0)),
                  pl.BlockSpec((D_MODEL, D_MODEL), lambda i: (0, 0)),
                  pl.BlockSpec((tm, D_MODEL), lambda i: (i, 0)),
                  pl.BlockSpec((1, D_MODEL), lambda i: (0, 0))],
        out_specs=pl.BlockSpec((tm, D_MODEL), lambda i: (i, 0)),
        out_shape=jax.ShapeDtypeStruct((t, D_MODEL), F32),
        compiler_params=_cparams(("parallel",)),
        name="out_proj",
    )(merged, w_out, x2, final_norm.reshape(1, D_MODEL))


def _pack_w_in(w_in):
    wt = lax.optimization_barrier(jnp.swapaxes(w_in, 1, 2).astype(BF16))
    depth = wt.shape[0]
    zeros = lambda n: jnp.zeros((depth, n, D_MODEL), BF16)
    half = MLA_ROPE // 2
    kr0 = IN_OFFS[2]
    bc0 = IN_OFFS[9] + BRANCH_W
    w_tail = jnp.concatenate([
        wt[:, IN_OFFS[0]:IN_OFFS[1]],
        wt[:, bc0:IN_OFFS[10]],
        wt[:, IN_OFFS[1]:IN_OFFS[2]],
        wt[:, kr0:kr0 + MLA_ROPE], zeros(LANE - MLA_ROPE),
        wt[:, kr0 + half:kr0 + MLA_ROPE], wt[:, kr0:kr0 + half], zeros(LANE - MLA_ROPE),
    ], axis=1)
    dt0 = IN_OFFS[10]
    w_dt = jnp.concatenate([wt[:, dt0:dt0 + SSD_HEADS], zeros(LANE - SSD_HEADS),
                            wt[:, dt0 + SSD_HEADS:dt0 + 2 * SSD_HEADS], zeros(LANE - SSD_HEADS)], axis=1)
    return wt, w_tail, w_dt


def _pack_mla(w_uq, w_ukv):
    half = MLA_ROPE // 2
    wq3 = w_uq.reshape(MLA_Q_LORA, MLA_HEADS, MLA_NOPE + MLA_ROPE)
    zq = jnp.zeros((MLA_Q_LORA, MLA_HEADS, LANE - MLA_ROPE), w_uq.dtype)
    wq = jnp.concatenate([wq3, zq], axis=2).reshape(MLA_Q_LORA, MLA_HEADS * MLA_QK).astype(BF16)
    wqp = jnp.concatenate([wq3[:, :, MLA_NOPE + half:], wq3[:, :, MLA_NOPE:MLA_NOPE + half], zq],
                          axis=2).reshape(MLA_Q_LORA, MLA_HEADS * LANE).astype(BF16)
    wkv3 = w_ukv.reshape(MLA_KV_LORA, MLA_HEADS, MLA_NOPE + MLA_V)
    wk = wkv3[:, :, :MLA_NOPE].reshape(MLA_KV_LORA, MLA_HEADS * MLA_NOPE).astype(BF16)
    wv = wkv3[:, :, MLA_NOPE:].reshape(MLA_KV_LORA, MLA_HEADS * MLA_V).astype(BF16)
    return wq, wqp, wk, wv


def _rope_tables(s):
    def angles(rot):
        half = rot // 2
        inv_freq = jnp.power(ROPE_THETA, -jnp.arange(half, dtype=F32) * 2.0 / rot)
        ang = jnp.arange(s, dtype=F32)[:, None] * inv_freq[None, :]
        return jnp.cos(ang), jnp.sin(ang)

    c, sn = angles(MLA_ROPE)
    z = jnp.zeros((s, LANE - MLA_ROPE), F32)
    mla_cos = jnp.concatenate([c, c, z], axis=1)
    mla_sin = jnp.concatenate([-sn, sn, z], axis=1)
    c, sn = angles(DIFF_ROT)
    zh = jnp.zeros_like(sn)
    rest = LANE - DIFF_ROT
    d_cos = jnp.concatenate([c, c, jnp.ones((s, rest), F32)], axis=1)
    d_sa = jnp.concatenate([zh, sn, jnp.zeros((s, rest), F32)], axis=1)
    d_sb = jnp.concatenate([-sn, zh, jnp.zeros((s, rest), F32)], axis=1)
    return mla_cos, mla_sin, d_cos, d_sa, d_sb


def _trunk(x, w_inproj, packed, norm_w, mla_q_norm, mla_kv_norm, diff_lambda, diff_subln, ssd_conv_w, ssd_conv_b,
           ssd_dt_bias, ssd_a_log, ssd_d, ssd_norm, pool_w, pool_scale, w_branch, w_out, final_norm):
    b, s, _ = x.shape
    t = b * s
    mla_cos, mla_sin, d_cos, d_sa, d_sb = _rope_tables(s)
    x2 = x.reshape(t, D_MODEL)
    for l in range(DEPTH):
        wq, wqp, wk, wv = packed[l]
        lambda_init = 0.8 - 0.6 * math.exp(-0.3 * l)
        proj, dt = _inproj(x2, norm_w[l], *w_inproj, l)
        proj3 = proj.reshape(b, s, NP_OUT)
        q, k, v = _mla_prep(proj, s, mla_cos, mla_sin, mla_q_norm[l], mla_kv_norm[l], wq, wqp, wk, wv)
        br_mla = _mla_attn(q.reshape(b, s, -1), k.reshape(b, s, -1), v.reshape(b, s, -1), proj3)
        dq, dk = _diff_prep(proj, s, d_cos, d_sa, d_sb)
        br_diff = _diff_attn(dq.reshape(b, s, -1), dk.reshape(b, s, -1), proj3, diff_lambda[l],
                             diff_subln[l], lambda_init)
        xs_act, bc_act = _ssd_conv(proj3, ssd_conv_w[l], ssd_conv_b[l])
        br_ssd = _ssd_scan(xs_act, bc_act, dt.reshape(b, s, DT_W), proj3, ssd_dt_bias[l], ssd_a_log[l],
                           ssd_d[l], ssd_norm[l])
        br_pool = _pool(proj3, pool_w[l].astype(BF16), pool_scale[l])
        branches = [a.reshape(t, BRANCH_W) for a in (br_mla, br_diff, br_ssd, br_pool)]
        merged = _merge(branches, w_branch[l].astype(BF16), proj)
        x2 = _out_proj(merged, w_out[l].astype(BF16), x2, final_norm, final=(l == DEPTH - 1))
    return x2.reshape(b, s, D_MODEL)


def kernel(x_prompt, x_sample, norm_w, w_in, mla_q_norm, mla_w_uq, mla_kv_norm, mla_w_ukv, diff_lambda,
           diff_subln, ssd_conv_w, ssd_conv_b, ssd_dt_bias, ssd_a_log, ssd_d, ssd_norm, pool_w, pool_scale,
           w_branch, w_out, final_norm):
    w_inproj = _pack_w_in(w_in)
    packed = [_pack_mla(mla_w_uq[l], mla_w_ukv[l]) for l in range(DEPTH)]
    rest = (norm_w, mla_q_norm, mla_kv_norm, diff_lambda, diff_subln, ssd_conv_w, ssd_conv_b, ssd_dt_bias,
            ssd_a_log, ssd_d, ssd_norm, pool_w, pool_scale, w_branch, w_out, final_norm)
    return (_trunk(x_prompt, w_inproj, packed, *rest), _trunk(x_sample, w_inproj, packed, *rest))
```

```python
import functools
import math

import numpy as np
import jax
import jax.numpy as jnp
from jax import lax
from jax.experimental import pallas as pl
from jax.experimental.pallas import tpu as pltpu

F32 = jnp.float32
BF16 = jnp.bfloat16
HIGHEST = lax.Precision.HIGHEST

D_MODEL = 2048
DEPTH = 2
BRANCH_W = D_MODEL // 2
N_BRANCH = 4
ROPE_THETA = 500000.0
EPS = 1e-6
MLA_HEADS = 8
MLA_NOPE = 128
MLA_ROPE = 64
MLA_V = BRANCH_W // MLA_HEADS
MLA_Q_LORA = 512
MLA_KV_LORA = 256
DIFF_HEADS = 4
DIFF_HD = BRANCH_W // (2 * DIFF_HEADS)
DIFF_ROT = DIFF_HD // 4
SSD_P = 64
SSD_HEADS = BRANCH_W // SSD_P
SSD_N = 128
SSD_G = 2
SSD_CONV = 4
SSD_CHUNK = 128
SSD_BLOCK = 1024
SEQ_TILE = 1024
SSD_CONV_DIM = BRANCH_W + 2 * SSD_G * SSD_N
POOL_WINDOWS = (2, 4, 8, 16)
POOL_GROUP = BRANCH_W // 4
IN_SIZES = (MLA_Q_LORA, MLA_KV_LORA, MLA_ROPE, BRANCH_W, BRANCH_W, BRANCH_W, BRANCH_W, BRANCH_W,
            BRANCH_W, SSD_CONV_DIM, 2 * SSD_HEADS, BRANCH_W, BRANCH_W, N_BRANCH * D_MODEL)
IN_OFFS = tuple(int(v) for v in np.cumsum((0,) + IN_SIZES))

LANE = 128
HALO = 16
VMEM_LIMIT = 56 * 1024 * 1024

OFF_GMLA = 0
OFF_DQ = 1 * BRANCH_W
OFF_DK = 2 * BRANCH_W
OFF_DV = 3 * BRANCH_W
OFF_GDIFF = 4 * BRANCH_W
OFF_Z = 5 * BRANCH_W
OFF_XS = 6 * BRANCH_W
OFF_U = 7 * BRANCH_W
OFF_GPOOL = 8 * BRANCH_W
OFF_MG = 9 * BRANCH_W
OFF_CQ = OFF_MG + N_BRANCH * D_MODEL
OFF_BC = OFF_CQ + MLA_Q_LORA
OFF_CKV = OFF_BC + 2 * SSD_G * SSD_N
OFF_KR = OFF_CKV + MLA_KV_LORA
OFF_KRP = OFF_KR + LANE
NP_MAIN = OFF_KRP + LANE
DT_W = 2 * LANE

ATTN_TQ = 2048
ATTN_BUFS = 2
ATTN_SUB = 256
MLA_QK = 2 * LANE
LOG2E = math.log2(math.e)


def _cparams(sem, vmem=VMEM_LIMIT):
    return pltpu.CompilerParams(dimension_semantics=sem, vmem_limit_bytes=vmem)


def _tile(n, pref):
    t = min(n, pref)
    while n % t:
        t //= 2
    return t


def _silu(x):
    return x * jax.nn.sigmoid(x)


def _softplus(x):
    return jnp.maximum(x, 0.0) + jnp.log1p(jnp.exp(-jnp.abs(x)))


_NT = (((1,), (1,)), ((), ()))
INPROJ_TN = 1024
TAIL_W = NP_MAIN - OFF_CQ

_N_TILE_A = OFF_U // INPROJ_TN
_N_TILE_MAIN = OFF_CQ // INPROJ_TN
_N_TILE = _N_TILE_MAIN + 2
NP_OUT = _N_TILE * INPROJ_TN


def _inproj_src_row(j):
    return jnp.where(j < _N_TILE_A, IN_OFFS[3] + j * INPROJ_TN,
                     jnp.where(j < _N_TILE_MAIN, IN_OFFS[11] + (j - _N_TILE_A) * INPROJ_TN, IN_OFFS[0]))


def _inproj_kernel(x_ref, nw_ref, w_ref, wtail_ref, wdt_ref, o_ref, dt_ref, h_ref):
    j = pl.program_id(1)
    tn = INPROJ_TN

    @pl.when(j == 0)
    def _():
        x = x_ref[...]
        y = x * lax.rsqrt(jnp.mean(x * x, axis=-1, keepdims=True) + EPS)
        hb = (y * nw_ref[...]).astype(BF16)
        h_ref[...] = hb
        dt_ref[...] = lax.dot_general(hb, wdt_ref[...], _NT, preferred_element_type=F32)

    @pl.when(j < _N_TILE_MAIN)
    def _():
        o_ref[...] = lax.dot_general(h_ref[...], w_ref[...].astype(BF16), _NT,
                                     preferred_element_type=F32).astype(BF16)

    @pl.when(j == _N_TILE_MAIN)
    def _():
        o_ref[...] = lax.dot_general(h_ref[...], wtail_ref[0:tn, :], _NT,
                                     preferred_element_type=F32).astype(BF16)

    @pl.when(j == _N_TILE_MAIN + 1)
    def _():
        rest = TAIL_W - tn
        o_ref[:, 0:rest] = lax.dot_general(h_ref[...], wtail_ref[tn:TAIL_W, :], _NT,
                                           preferred_element_type=F32).astype(BF16)
        o_ref[:, rest:] = jnp.zeros((o_ref.shape[0], tn - rest), BF16)


def _inproj(x2, norm_w, wt, w_tail, w_dt, l):
    t = x2.shape[0]
    tm = _tile(t, 1024)
    tn = INPROJ_TN
    return pl.pallas_call(
        _inproj_kernel,
        grid=(t // tm, _N_TILE),
        in_specs=[
            pl.BlockSpec((tm, D_MODEL), lambda i, j: (i, 0)),
            pl.BlockSpec((1, D_MODEL), lambda i, j: (0, 0)),
            pl.BlockSpec((pl.Element(tn), pl.Element(D_MODEL)),
                         lambda i, j: (pl.multiple_of(l * IN_OFFS[-1] + _inproj_src_row(j), HALO), 0)),
            pl.BlockSpec((None, TAIL_W, D_MODEL), lambda i, j: (l, 0, 0)),
            pl.BlockSpec((None, DT_W, D_MODEL), lambda i, j: (l, 0, 0)),
        ],
        out_specs=[
            pl.BlockSpec((tm, tn), lambda i, j: (i, j)),
            pl.BlockSpec((tm, DT_W), lambda i, j: (i, 0)),
        ],
        out_shape=[jax.ShapeDtypeStruct((t, NP_OUT), BF16), jax.ShapeDtypeStruct((t, DT_W), F32)],
        scratch_shapes=[pltpu.VMEM((tm, D_MODEL), BF16)],
        compiler_params=_cparams(("parallel", "arbitrary")),
        name="inproj",
    )(x2, norm_w.reshape(1, D_MODEL), wt.reshape(-1, D_MODEL), w_tail, w_dt)


def _mla_prep_kernel(cq_ref, ckv_ref, kr_ref, krp_ref, cos_ref, sin_ref, qn_ref, kvn_ref,
                     wq_ref, wqp_ref, wk_ref, wv_ref, q_ref, k_ref, v_ref):
    cos = cos_ref[...]
    sin = sin_ref[...]
    cq = cq_ref[...].astype(F32)
    ncq = (cq * lax.rsqrt(jnp.mean(cq * cq, axis=-1, keepdims=True) + EPS) * qn_ref[...]).astype(BF16)
    qm = jnp.dot(ncq, wq_ref[...], preferred_element_type=F32)
    qp = jnp.dot(ncq, wqp_ref[...], preferred_element_type=F32)
    ckv = ckv_ref[...].astype(F32)
    nkv = (ckv * lax.rsqrt(jnp.mean(ckv * ckv, axis=-1, keepdims=True) + EPS) * kvn_ref[...]).astype(BF16)
    kn = jnp.dot(nkv, wk_ref[...], preferred_element_type=F32)
    v_ref[...] = jnp.dot(nkv, wv_ref[...], preferred_element_type=F32).astype(BF16)
    kpe = (kr_ref[...].astype(F32) * cos + krp_ref[...].astype(F32) * sin).astype(BF16)
    qscale = (MLA_NOPE + MLA_ROPE) ** -0.5 * LOG2E
    for h in range(MLA_HEADS):
        lo = h * MLA_QK
        q_ref[:, lo:lo + LANE] = (qm[:, lo:lo + LANE] * qscale).astype(BF16)
        qr = qm[:, lo + LANE:lo + 2 * LANE] * cos + qp[:, h * LANE:(h + 1) * LANE] * sin
        q_ref[:, lo + LANE:lo + 2 * LANE] = (qr * qscale).astype(BF16)
        k_ref[:, lo:lo + LANE] = kn[:, h * LANE:(h + 1) * LANE].astype(BF16)
        k_ref[:, lo + LANE:lo + 2 * LANE] = kpe


def _mla_prep(proj, s, cos_t, sin_t, q_norm, kv_norm, wq, wqp, wk, wv):
    t = proj.shape[0]
    tm = _tile(s, SEQ_TILE)
    ns = s // tm
    hq = MLA_HEADS * MLA_QK
    const = lambda shape: pl.BlockSpec(shape, lambda i: (0, 0))
    return pl.pallas_call(
        _mla_prep_kernel,
        grid=(t // tm,),
        in_specs=[
            pl.BlockSpec((tm, MLA_Q_LORA), lambda i: (i, OFF_CQ // MLA_Q_LORA)),
            pl.BlockSpec((tm, MLA_KV_LORA), lambda i: (i, OFF_CKV // MLA_KV_LORA)),
            pl.BlockSpec((tm, LANE), lambda i: (i, OFF_KR // LANE)),
            pl.BlockSpec((tm, LANE), lambda i: (i, OFF_KRP // LANE)),
            pl.BlockSpec((tm, LANE), lambda i: (i % ns, 0)),
            pl.BlockSpec((tm, LANE), lambda i: (i % ns, 0)),
            const((1, MLA_Q_LORA)),
            const((1, MLA_KV_LORA)),
            const((MLA_Q_LORA, hq)),
            const((MLA_Q_LORA, MLA_HEADS * LANE)),
            const((MLA_KV_LORA, MLA_HEADS * MLA_NOPE)),
            const((MLA_KV_LORA, MLA_HEADS * MLA_V)),
        ],
        out_specs=[
            pl.BlockSpec((tm, hq), lambda i: (i, 0)),
            pl.BlockSpec((tm, hq), lambda i: (i, 0)),
            pl.BlockSpec((tm, MLA_HEADS * MLA_V), lambda i: (i, 0)),
        ],
        out_shape=[jax.ShapeDtypeStruct((t, hq), BF16), jax.ShapeDtypeStruct((t, hq), BF16),
                   jax.ShapeDtypeStruct((t, MLA_HEADS * MLA_V), BF16)],
        compiler_params=_cparams(("parallel",)),
        name="mla_prep",
    )(proj, proj, proj, proj, cos_t, sin_t, q_norm.reshape(1, -1), kv_norm.reshape(1, -1), wq, wqp, wk, wv)


def _scores(q, k_ref):
    return lax.dot_general(q, k_ref[...], (((1,), (1,)), ((), ())), preferred_element_type=F32)


def _softmax_pv_from(sc, v_ref):
    m = jnp.max(sc, axis=-1, keepdims=True)
    p = jnp.exp2(sc - m)
    l = jnp.sum(p, axis=-1, keepdims=True)
    return jnp.dot(p.astype(BF16), v_ref[...], preferred_element_type=F32), l


def _staged(n_jobs, bufs, score_fn, consume_fn):
    ahead = len(bufs) - 1
    for n in range(min(ahead, n_jobs)):
        bufs[n % len(bufs)][...] = score_fn(n)
    for n in range(n_jobs):
        if n + ahead < n_jobs:
            bufs[(n + ahead) % len(bufs)][...] = score_fn(n + ahead)
        consume_fn(n, bufs[n % len(bufs)][...])


def _mla_attn_kernel(q_ref, k_ref, v_ref, g_ref, o_ref, *bufs, sub):
    def consume(n, sc):
        r = n * sub
        acc, l = _softmax_pv_from(sc, v_ref)
        o_ref[r:r + sub, :] = ((acc / l) * _silu(g_ref[r:r + sub, :].astype(F32))).astype(BF16)

    _staged(q_ref.shape[0] // sub, bufs, lambda n: _scores(q_ref[n * sub:(n + 1) * sub, :], k_ref), consume)


def _mla_attn(q, k, v, proj3):
    b, s, _ = q.shape
    tq = _tile(s, ATTN_TQ)
    return pl.pallas_call(
        functools.partial(_mla_attn_kernel, sub=min(tq, ATTN_SUB)),
        grid=(b, MLA_HEADS, s // tq),
        in_specs=[
            pl.BlockSpec((None, tq, MLA_QK), lambda bi, h, qi: (bi, qi, h)),
            pl.BlockSpec((None, s, MLA_QK), lambda bi, h, qi: (bi, 0, h)),
            pl.BlockSpec((None, s, MLA_V), lambda bi, h, qi: (bi, 0, h)),
            pl.BlockSpec((None, tq, MLA_V), lambda bi, h, qi: (bi, qi, OFF_GMLA // MLA_V + h)),
        ],
        out_specs=pl.BlockSpec((None, tq, MLA_V), lambda bi, h, qi: (bi, qi, h)),
        out_shape=jax.ShapeDtypeStruct((b, s, BRANCH_W), BF16),
        scratch_shapes=[pltpu.VMEM((min(tq, ATTN_SUB), s), F32)] * ATTN_BUFS,
        compiler_params=_cparams(("parallel", "parallel", "parallel")),
        name="mla_attn",
    )(q, k, v, proj3)


def _diff_prep_kernel(dq_ref, dk_ref, cos_ref, sa_ref, sb_ref, q_ref, k_ref):
    cos = cos_ref[...]
    sa = sa_ref[...]
    sb = sb_ref[...]
    half = DIFF_ROT // 2
    qscale = DIFF_HD ** -0.5 * LOG2E
    for h in range(2 * DIFF_HEADS):
        sl = slice(h * DIFF_HD, (h + 1) * DIFF_HD)
        xq = dq_ref[:, sl].astype(F32)
        xk = dk_ref[:, sl].astype(F32)
        rq = xq * cos + pltpu.roll(xq, half, 1) * sa + pltpu.roll(xq, DIFF_HD - half, 1) * sb
        rk = xk * cos + pltpu.roll(xk, half, 1) * sa + pltpu.roll(xk, DIFF_HD - half, 1) * sb
        q_ref[:, sl] = (rq * qscale).astype(BF16)
        k_ref[:, sl] = rk.astype(BF16)


def _diff_prep(proj, s, cos_t, sa_t, sb_t):
    t = proj.shape[0]
    tm = _tile(s, SEQ_TILE)
    ns = s // tm
    tab = pl.BlockSpec((tm, LANE), lambda i: (i % ns, 0))
    return pl.pallas_call(
        _diff_prep_kernel,
        grid=(t // tm,),
        in_specs=[
            pl.BlockSpec((tm, BRANCH_W), lambda i: (i, OFF_DQ // BRANCH_W)),
            pl.BlockSpec((tm, BRANCH_W), lambda i: (i, OFF_DK // BRANCH_W)),
            tab, tab, tab,
        ],
        out_specs=[pl.BlockSpec((tm, BRANCH_W), lambda i: (i, 0))] * 2,
        out_shape=[jax.ShapeDtypeStruct((t, BRANCH_W), BF16)] * 2,
        compiler_params=_cparams(("parallel",)),
        name="diff_prep",
    )(proj, proj, cos_t, sa_t, sb_t)


def _diff_attn_kernel(q1_ref, q2_ref, k1_ref, k2_ref, v_ref, g_ref, lam_ref, sub_ref, o_ref, *bufs,
                      sub, lambda_init):
    lp = lam_ref[...]
    lam = (jnp.exp(jnp.sum(lp[0:1] * lp[1:2], axis=-1, keepdims=True))
           - jnp.exp(jnp.sum(lp[2:3] * lp[3:4], axis=-1, keepdims=True)) + lambda_init)
    qk = ((q1_ref, k1_ref), (q2_ref, k2_ref))
    first = {}

    def scores(n):
        q_ref, k_ref = qk[n % 2]
        return _scores(q_ref[(n // 2) * sub:(n // 2 + 1) * sub, :], k_ref)

    def consume(n, sc):
        acc, l = _softmax_pv_from(sc, v_ref)
        if n % 2 == 0:
            first[n // 2] = acc / l
            return
        r = (n // 2) * sub
        o = first.pop(n // 2) - lam * (acc / l)
        o = o * lax.rsqrt(jnp.mean(o * o, axis=-1, keepdims=True) + EPS) * sub_ref[...]
        o = o * (1.0 - lambda_init)
        o_ref[r:r + sub, :] = (o * _silu(g_ref[r:r + sub, :].astype(F32))).astype(BF16)

    _staged(2 * (q1_ref.shape[0] // sub), bufs, scores, consume)


def _diff_attn(q, k, proj3, lam_params, subln, lambda_init):
    b, s, _ = q.shape
    tq = _tile(s, ATTN_TQ)
    dv = 2 * DIFF_HD
    return pl.pallas_call(
        functools.partial(_diff_attn_kernel, sub=min(tq, ATTN_SUB), lambda_init=lambda_init),
        grid=(b, DIFF_HEADS, s // tq),
        in_specs=[
            pl.BlockSpec((None, tq, DIFF_HD), lambda bi, h, qi: (bi, qi, 2 * h)),
            pl.BlockSpec((None, tq, DIFF_HD), lambda bi, h, qi: (bi, qi, 2 * h + 1)),
            pl.BlockSpec((None, s, DIFF_HD), lambda bi, h, qi: (bi, 0, 2 * h)),
            pl.BlockSpec((None, s, DIFF_HD), lambda bi, h, qi: (bi, 0, 2 * h + 1)),
            pl.BlockSpec((None, s, dv), lambda bi, h, qi: (bi, 0, OFF_DV // dv + h)),
            pl.BlockSpec((None, tq, dv), lambda bi, h, qi: (bi, qi, OFF_GDIFF // dv + h)),
            pl.BlockSpec((4, DIFF_HD), lambda bi, h, qi: (0, 0)),
            pl.BlockSpec((1, dv), lambda bi, h, qi: (0, 0)),
        ],
        out_specs=pl.BlockSpec((None, tq, dv), lambda bi, h, qi: (bi, qi, h)),
        out_shape=jax.ShapeDtypeStruct((b, s, BRANCH_W), BF16),
        scratch_shapes=[pltpu.VMEM((min(tq, ATTN_SUB), s), F32)] * ATTN_BUFS,
        compiler_params=_cparams(("parallel", "parallel", "parallel")),
        name="diff_attn",
    )(q, q, k, k, proj3, proj3, lam_params, subln.reshape(1, dv))


def _ssd_conv_kernel(xs_ref, xsp_ref, xsn_ref, bc_ref, bcp_ref, bcn_ref, w_ref, b_ref,
                     oxs_ref, obc_ref, ext_ref, *, tm):
    i = pl.program_id(1)
    has_prev = i > 0
    has_next = i < pl.num_programs(1) - 1
    pad_l = SSD_CONV // 2

    def conv(cur_ref, prev_ref, next_ref, out_ref, c0, width):
        ext_ref[0:HALO, 0:width] = jnp.where(has_prev, prev_ref[...].astype(F32), 0.0)
        ext_ref[HALO:HALO + tm, 0:width] = cur_ref[...].astype(F32)
        ext_ref[HALO + tm:, 0:width] = jnp.where(has_next, next_ref[...].astype(F32), 0.0)
        acc = b_ref[:, c0:c0 + width]
        for j in range(SSD_CONV):
            off = HALO - pad_l + j
            acc = acc + ext_ref[off:off + tm, 0:width] * w_ref[j:j + 1, c0:c0 + width]
        out_ref[...] = _silu(acc).astype(BF16)

    conv(xs_ref, xsp_ref, xsn_ref, oxs_ref, 0, BRANCH_W)
    conv(bc_ref, bcp_ref, bcn_ref, obc_ref, BRANCH_W, SSD_CONV_DIM - BRANCH_W)


def _halo_specs(tm, s, width, col):
    r = tm // HALO
    last = s // HALO - 1
    cur = pl.BlockSpec((None, tm, width), lambda bi, i: (bi, i, col))
    prev = pl.BlockSpec((None, HALO, width), lambda bi, i: (bi, jnp.maximum(i * r - 1, 0), col))
    nxt = pl.BlockSpec((None, HALO, width), lambda bi, i: (bi, jnp.minimum((i + 1) * r, last), col))
    return [cur, prev, nxt]


def _ssd_conv(proj3, conv_w, conv_b):
    b, s, _ = proj3.shape
    tm = _tile(s, SEQ_TILE)
    wbc = SSD_CONV_DIM - BRANCH_W
    return pl.pallas_call(
        functools.partial(_ssd_conv_kernel, tm=tm),
        grid=(b, s // tm),
        in_specs=(_halo_specs(tm, s, BRANCH_W, OFF_XS // BRANCH_W) + _halo_specs(tm, s, wbc, OFF_BC // wbc)
                  + [pl.BlockSpec((SSD_CONV, SSD_CONV_DIM), lambda bi, i: (0, 0)),
                     pl.BlockSpec((1, SSD_CONV_DIM), lambda bi, i: (0, 0))]),
        out_specs=[pl.BlockSpec((None, tm, BRANCH_W), lambda bi, i: (bi, i, 0)),
                   pl.BlockSpec((None, tm, wbc), lambda bi, i: (bi, i, 0))],
        out_shape=[jax.ShapeDtypeStruct((b, s, BRANCH_W), BF16), jax.ShapeDtypeStruct((b, s, wbc), BF16)],
        scratch_shapes=[pltpu.VMEM((tm + 2 * HALO, BRANCH_W), F32)],
        compiler_params=_cparams(("parallel", "parallel")),
        name="ssd_conv",
    )(proj3, proj3, proj3, proj3, proj3, proj3, conv_w, conv_b.reshape(1, -1))


def _ssd_chunk(xs_b, bc, dt_raw, bias, alog, st_ref, reverse):
    ln = SSD_CHUNK
    gw = BRANCH_W // SSD_G
    xs = xs_b.astype(F32)
    dt = _softplus(dt_raw + bias)
    da = dt * (-jnp.exp(alog))
    ri = lax.broadcasted_iota(jnp.int32, (ln, ln), 0)
    ci = lax.broadcasted_iota(jnp.int32, (ln, ln), 1)
    tri = (ci >= ri) if reverse else (ri >= ci)
    cs = jnp.dot(tri.astype(F32), da, precision=HIGHEST, preferred_element_type=F32)
    cs_end = cs[0:1, :] if reverse else cs[ln - 1:ln, :]
    er = lax.broadcasted_iota(jnp.int32, (2 * LANE, BRANCH_W), 0)
    ec = lax.broadcasted_iota(jnp.int32, (2 * LANE, BRANCH_W), 1)
    expand = jnp.where(ec // SSD_P == er % LANE, 1.0, 0.0).astype(BF16)
    cols = jnp.concatenate([dt, jnp.exp(cs), jnp.exp(cs_end - cs)], axis=0)
    hi = cols.astype(BF16)
    lo = (cols - hi.astype(F32)).astype(BF16)
    wide = jnp.dot(jnp.concatenate([hi, lo], axis=1), expand, preferred_element_type=F32)
    dt_e = wide[0:ln]
    dec_out_e = wide[ln:2 * ln]
    dec_st_e = wide[2 * ln:3 * ln]
    tot_e = dec_out_e[0:1, :] if reverse else dec_out_e[ln - 1:ln, :]
    cs_t = cs.T
    xd = xs * dt_e
    xdb = xd.astype(BF16)
    xst = (xd * dec_st_e).astype(BF16)
    low = lax.broadcasted_iota(jnp.int32, (ln, LANE), 1) < SSD_P
    zero = jnp.zeros((ln, LANE), BF16)
    outs = []
    for g in range(SSD_G):
        bg = bc[:, g * SSD_N:(g + 1) * SSD_N]
        cg = bc[:, (SSD_G + g) * SSD_N:(SSD_G + g + 1) * SSD_N]
        cb = lax.dot_general(cg, bg, (((1,), (1,)), ((), ())), preferred_element_type=F32)
        st = st_ref[g]
        y_off = jnp.dot(cg, st.astype(BF16), preferred_element_type=F32) * dec_out_e[:, g * gw:(g + 1) * gw]
        bt = bg.astype(F32).T.astype(BF16)
        st_ref[g] = (st * tot_e[:, g * gw:(g + 1) * gw]
                     + jnp.dot(bt, xst[:, g * gw:(g + 1) * gw], preferred_element_type=F32))
        for kp in range(gw // LANE):
            mats = []
            for hh in range(2):
                hd = g * (SSD_HEADS // SSD_G) + 2 * kp + hh
                seg = cs[:, hd:hd + 1] - cs_t[hd:hd + 1, :]
                lm = jnp.exp(jnp.where(tri, seg, -jnp.inf))
                mats.append((cb * lm).astype(BF16))
            lhs = jnp.concatenate(mats, axis=1)
            col = g * gw + kp * LANE
            xp = xdb[:, col:col + LANE]
            rhs = jnp.concatenate([jnp.where(low, xp, zero), jnp.where(low, zero, xp)], axis=0)
            y_diag = jnp.dot(lhs, rhs, preferred_element_type=F32)
            outs.append(y_diag + y_off[:, kp * LANE:(kp + 1) * LANE])
    return jnp.concatenate(outs, axis=1), xs


def _chunk_rows(n_rows, reverse):
    starts = range(0, n_rows, SSD_CHUNK)
    return [slice(r, r + SSD_CHUNK) for r in (reversed(starts) if reverse else starts)]


def _ssd_fwd_kernel(xs_ref, bc_ref, dt_ref, bias_ref, alog_ref, y_ref, st_ref):
    @pl.when(pl.program_id(1) == 0)
    def _():
        st_ref[...] = jnp.zeros_like(st_ref)

    for rows in _chunk_rows(xs_ref.shape[0], reverse=False):
        y, _ = _ssd_chunk(xs_ref[rows, :], bc_ref[rows, :], dt_ref[rows, :], bias_ref[...], alog_ref[...],
                          st_ref, reverse=False)
        y_ref[rows, :] = y


def _ssd_bwd_kernel(xs_ref, bc_ref, dt_ref, bias_ref, alog_ref, yf_ref, z_ref, d_ref, nw_ref,
                    o_ref, st_ref):
    @pl.when(pl.program_id(1) == 0)
    def _():
        st_ref[...] = jnp.zeros_like(st_ref)

    gw = BRANCH_W // SSD_G
    for rows in _chunk_rows(xs_ref.shape[0], reverse=True):
        yb, xs = _ssd_chunk(xs_ref[rows, :], bc_ref[rows, :], dt_ref[rows, :], bias_ref[...], alog_ref[...],
                            st_ref, reverse=True)
        y = yf_ref[rows, :] + yb + xs * d_ref[...]
        y = y * _silu(z_ref[rows, :].astype(F32))
        for g in range(SSD_G):
            yg = y[:, g * gw:(g + 1) * gw]
            yn = yg * lax.rsqrt(jnp.mean(yg * yg, axis=-1, keepdims=True) + EPS)
            o_ref[rows, g * gw:(g + 1) * gw] = (yn * nw_ref[:, g * gw:(g + 1) * gw]).astype(BF16)


def _ssd_scan(xs_act, bc_act, dt3, proj3, dt_bias, a_log, d_skip, norm_w):
    b, s, _ = xs_act.shape
    tb = _tile(s, SSD_BLOCK)
    nc = s // tb
    wbc = SSD_CONV_DIM - BRANCH_W
    gw = BRANCH_W // SSD_G
    pad = lambda v: jnp.pad(v.astype(F32), (0, LANE - SSD_HEADS)).reshape(1, LANE)
    scratch = [pltpu.VMEM((SSD_G, SSD_N, gw), F32)]
    vec = lambda w: pl.BlockSpec((1, w), lambda bi, c: (0, 0))

    def specs(cmap, d):
        return [pl.BlockSpec((None, tb,BRANCH_W), lambda bi, c: (bi, cmap(c), 0)),
                pl.BlockSpec((None, tb,wbc), lambda bi, c: (bi, cmap(c), 0)),
                pl.BlockSpec((None, tb,LANE), lambda bi, c: (bi, cmap(c), d)),
                vec(LANE), vec(LANE)]

    y_f = pl.pallas_call(
        _ssd_fwd_kernel,
        grid=(b, nc),
        in_specs=specs(lambda c: c, 0),
        out_specs=pl.BlockSpec((None, tb,BRANCH_W), lambda bi, c: (bi, c, 0)),
        out_shape=jax.ShapeDtypeStruct((b, s, BRANCH_W), F32),
        scratch_shapes=scratch,
        compiler_params=_cparams(("parallel", "arbitrary")),
        name="ssd_fwd",
    )(xs_act, bc_act, dt3, pad(dt_bias[0]), pad(a_log[0]))

    rev = lambda c: nc - 1 - c
    return pl.pallas_call(
        _ssd_bwd_kernel,
        grid=(b, nc),
        in_specs=specs(rev, 1) + [
            pl.BlockSpec((None, tb,BRANCH_W), lambda bi, c: (bi, rev(c), 0)),
            pl.BlockSpec((None, tb,BRANCH_W), lambda bi, c: (bi, rev(c), OFF_Z // BRANCH_W)),
            vec(BRANCH_W), vec(BRANCH_W)],
        out_specs=pl.BlockSpec((None, tb,BRANCH_W), lambda bi, c: (bi, rev(c), 0)),
        out_shape=jax.ShapeDtypeStruct((b, s, BRANCH_W), BF16),
        scratch_shapes=scratch,
        compiler_params=_cparams(("parallel", "arbitrary")),
        name="ssd_bwd",
    )(xs_act, bc_act, dt3, pad(dt_bias[1]), pad(a_log[1]), y_f, proj3,
      jnp.repeat(d_skip.astype(F32), SSD_P).reshape(1, BRANCH_W), norm_w.reshape(1, BRANCH_W))


def _pool_kernel(u_ref, up_ref, un_ref, g_ref, w_ref, sc_ref, o_ref, ext_ref, *, tm, s):
    i = pl.program_id(1)
    cur = u_ref[...].astype(F32)
    ext_ref[0:HALO, :] = jnp.where(i > 0, up_ref[...].astype(F32), 0.0)
    ext_ref[HALO:HALO + tm, :] = cur
    ext_ref[HALO + tm:, :] = jnp.where(i < pl.num_programs(1) - 1, un_ref[...].astype(F32), 0.0)
    pos = i * tm + lax.broadcasted_iota(jnp.int32, (tm, 1), 0)
    for gi, w in enumerate(POOL_WINDOWS):
        lo = w // 2
        hi = w - 1 - lo
        sl = slice(gi * POOL_GROUP, (gi + 1) * POOL_GROUP)
        acc = ext_ref[HALO - lo:HALO - lo + tm, sl]
        for d in range(-lo + 1, hi + 1):
            acc = acc + ext_ref[HALO + d:HALO + d + tm, sl]
        cnt = (jnp.minimum(pos + hi + 1, s) - jnp.maximum(pos - lo, 0)).astype(F32)
        pooled = acc / cnt - cur[:, sl]
        mixed = jnp.dot(pooled.astype(BF16), w_ref[gi], preferred_element_type=F32)
        o_ref[:, sl] = (mixed * sc_ref[:, sl] * _silu(g_ref[:, sl].astype(F32))).astype(BF16)


def _pool(proj3, pool_w, pool_scale):
    b, s, _ = proj3.shape
    tm = _tile(s, SEQ_TILE)
    return pl.pallas_call(
        functools.partial(_pool_kernel, tm=tm, s=s),
        grid=(b, s // tm),
        in_specs=_halo_specs(tm, s, BRANCH_W, OFF_U // BRANCH_W) + [
            pl.BlockSpec((None, tm, BRANCH_W), lambda bi, i: (bi, i, OFF_GPOOL // BRANCH_W)),
            pl.BlockSpec((len(POOL_WINDOWS), POOL_GROUP, POOL_GROUP), lambda bi, i: (0, 0, 0)),
            pl.BlockSpec((1, BRANCH_W), lambda bi, i: (0, 0))],
        out_specs=pl.BlockSpec((None, tm, BRANCH_W), lambda bi, i: (bi, i, 0)),
        out_shape=jax.ShapeDtypeStruct((b, s, BRANCH_W), BF16),
        scratch_shapes=[pltpu.VMEM((tm + 2 * HALO, BRANCH_W), F32)],
        compiler_params=_cparams(("parallel", "parallel")),
        name="pool",
    )(proj3, proj3, proj3, proj3, pool_w, pool_scale.reshape(1, BRANCH_W))


def _merge_kernel(*refs):
    brs, ws, gs, o_ref = refs[0:4], refs[4:8], refs[8:12], refs[12]
    acc = None
    for br, w, g in zip(brs, ws, gs):
        term = jax.nn.sigmoid(g[...].astype(F32)) * jnp.dot(br[...], w[...], preferred_element_type=F32)
        acc = term if acc is None else acc + term
    o_ref[...] = acc.astype(BF16)


def _merge(branches, w_branch, proj):
    t = proj.shape[0]
    tm = _tile(t, 1024)
    tn = 512
    br_spec = pl.BlockSpec((tm, BRANCH_W), lambda i, j: (i, 0))
    w_specs = [pl.BlockSpec((None, BRANCH_W, tn), lambda i, j, k=k: (k, 0, j)) for k in range(N_BRANCH)]
    g_specs = [pl.BlockSpec((tm, tn), lambda i, j, k=k: (i, (OFF_MG + k * D_MODEL) // tn + j))
               for k in range(N_BRANCH)]
    return pl.pallas_call(
        _merge_kernel,
        grid=(t // tm, D_MODEL // tn),
        in_specs=[br_spec] * N_BRANCH + w_specs + g_specs,
        out_specs=pl.BlockSpec((tm, tn), lambda i, j: (i, j)),
        out_shape=jax.ShapeDtypeStruct((t, D_MODEL), BF16),
        compiler_params=_cparams(("parallel", "parallel")),
        name="merge",
    )(*branches, *([w_branch] * N_BRANCH), *([proj] * N_BRANCH))


def _out_kernel(m_ref, w_ref, x_ref, fn_ref, o_ref, *, final):
    y = x_ref[...] + jnp.dot(m_ref[...], w_ref[...], preferred_element_type=F32)
    if final:
        y = y * lax.rsqrt(jnp.mean(y * y, axis=-1, keepdims=True) + EPS) * fn_ref[...]
    o_ref[...] = y


def _out_proj(merged, w_out, x2, final_norm, final):
    t = x2.shape[0]
    tm = _tile(t, 512)
    return pl.pallas_call(
        functools.partial(_out_kernel, final=final),
        grid=(t // tm,),
        in_specs=[pl.BlockSpec((tm, D_MODEL), lambda i: (i, 0)),
                  pl.BlockSpec((D_MODEL, D_MODEL), lambda i: (0, 0)),
                  pl.BlockSpec((tm, D_MODEL), lambda i: (i, 0)),
                  pl.BlockSpec((1, D_MODEL), lambda i: (0, 0))],
        out_specs=pl.BlockSpec((tm, D_MODEL), lambda i: (i, 0)),
        out_shape=jax.ShapeDtypeStruct((t, D_MODEL), F32),
        compiler_params=_cparams(("parallel",)),
        name="out_proj",
    )(merged, w_out, x2, final_norm.reshape(1, D_MODEL))


def _pack_w_in(w_in):
    wt = jnp.swapaxes(w_in, 1, 2)
    depth = wt.shape[0]
    zeros = lambda n: jnp.zeros((depth, n, D_MODEL), w_in.dtype)
    half = MLA_ROPE // 2
    kr0 = IN_OFFS[2]
    bc0 = IN_OFFS[9] + BRANCH_W
    w_tail = jnp.concatenate([
        wt[:, IN_OFFS[0]:IN_OFFS[1]],
        wt[:, bc0:IN_OFFS[10]],
        wt[:, IN_OFFS[1]:IN_OFFS[2]],
        wt[:, kr0:kr0 + MLA_ROPE], zeros(LANE - MLA_ROPE),
        wt[:, kr0 + half:kr0 + MLA_ROPE], wt[:, kr0:kr0 + half], zeros(LANE - MLA_ROPE),
    ], axis=1)
    dt0 = IN_OFFS[10]
    w_dt = jnp.concatenate([wt[:, dt0:dt0 + SSD_HEADS], zeros(LANE - SSD_HEADS),
                            wt[:, dt0 + SSD_HEADS:dt0 + 2 * SSD_HEADS], zeros(LANE - SSD_HEADS)], axis=1)
    w_tail, w_dt = lax.optimization_barrier((w_tail, w_dt))
    return wt, w_tail.astype(BF16), w_dt.astype(BF16)


def _pack_mla(w_uq, w_ukv):
    half = MLA_ROPE // 2
    wq3 = w_uq.reshape(MLA_Q_LORA, MLA_HEADS, MLA_NOPE + MLA_ROPE)
    zq = jnp.zeros((MLA_Q_LORA, MLA_HEADS, LANE - MLA_ROPE), w_uq.dtype)
    wq = jnp.concatenate([wq3, zq], axis=2).reshape(MLA_Q_LORA, MLA_HEADS * MLA_QK).astype(BF16)
    wqp = jnp.concatenate([wq3[:, :, MLA_NOPE + half:], wq3[:, :, MLA_NOPE:MLA_NOPE + half], zq],
                          axis=2).reshape(MLA_Q_LORA, MLA_HEADS * LANE).astype(BF16)
    wkv3 = w_ukv.reshape(MLA_KV_LORA, MLA_HEADS, MLA_NOPE + MLA_V)
    wk = wkv3[:, :, :MLA_NOPE].reshape(MLA_KV_LORA, MLA_HEADS * MLA_NOPE).astype(BF16)
    wv = wkv3[:, :, MLA_NOPE:].reshape(MLA_KV_LORA, MLA_HEADS * MLA_V).astype(BF16)
    return wq, wqp, wk, wv


def _rope_tables(s):
    def angles(rot):
        half = rot // 2
        inv_freq = jnp.power(ROPE_THETA, -jnp.arange(half, dtype=F32) * 2.0 / rot)
        ang = jnp.arange(s, dtype=F32)[:, None] * inv_freq[None, :]
        return jnp.cos(ang), jnp.sin(ang)

    c, sn = angles(MLA_ROPE)
    z = jnp.zeros((s, LANE - MLA_ROPE), F32)
    mla_cos = jnp.concatenate([c, c, z], axis=1)
    mla_sin = jnp.concatenate([-sn, sn, z], axis=1)
    c, sn = angles(DIFF_ROT)
    zh = jnp.zeros_like(sn)
    rest = LANE - DIFF_ROT
    d_cos = jnp.concatenate([c, c, jnp.ones((s, rest), F32)], axis=1)
    d_sa = jnp.concatenate([zh, sn, jnp.zeros((s, rest), F32)], axis=1)
    d_sb = jnp.concatenate([-sn, zh, jnp.zeros((s, rest), F32)], axis=1)
    return mla_cos, mla_sin, d_cos, d_sa, d_sb


def _trunk(x, w_inproj, packed, norm_w, mla_q_norm, mla_kv_norm, diff_lambda, diff_subln, ssd_conv_w, ssd_conv_b,
           ssd_dt_bias, ssd_a_log, ssd_d, ssd_norm, pool_w, pool_scale, w_branch, w_out, final_norm):
    b, s, _ = x.shape
    t = b * s
    mla_cos, mla_sin, d_cos, d_sa, d_sb = _rope_tables(s)
    x2 = x.reshape(t, D_MODEL)
    for l in range(DEPTH):
        wq, wqp, wk, wv = packed[l]
        lambda_init = 0.8 - 0.6 * math.exp(-0.3 * l)
        proj, dt = _inproj(x2, norm_w[l], *w_inproj, l)
        proj3 = proj.reshape(b, s, NP_OUT)
        q, k, v = _mla_prep(proj, s, mla_cos, mla_sin, mla_q_norm[l], mla_kv_norm[l], wq, wqp, wk, wv)
        br_mla = _mla_attn(q.reshape(b, s, -1), k.reshape(b, s, -1), v.reshape(b, s, -1), proj3)
        dq, dk = _diff_prep(proj, s, d_cos, d_sa, d_sb)
        br_diff = _diff_attn(dq.reshape(b, s, -1), dk.reshape(b, s, -1), proj3, diff_lambda[l],
                             diff_subln[l], lambda_init)
        xs_act, bc_act = _ssd_conv(proj3, ssd_conv_w[l], ssd_conv_b[l])
        br_ssd = _ssd_scan(xs_act, bc_act, dt.reshape(b, s, DT_W), proj3, ssd_dt_bias[l], ssd_a_log[l],
                           ssd_d[l], ssd_norm[l])
        br_pool = _pool(proj3, pool_w[l].astype(BF16), pool_scale[l])
        branches = [a.reshape(t, BRANCH_W) for a in (br_mla, br_diff, br_ssd, br_pool)]
        merged = _merge(branches, w_branch[l].astype(BF16), proj)
        x2 = _out_proj(merged, w_out[l].astype(BF16), x2, final_norm, final=(l == DEPTH - 1))
    return x2.reshape(b, s, D_MODEL)


def kernel(x_prompt, x_sample, norm_w, w_in, mla_q_norm, mla_w_uq, mla_kv_norm, mla_w_ukv, diff_lambda,
           diff_subln, ssd_conv_w, ssd_conv_b, ssd_dt_bias, ssd_a_log, ssd_d, ssd_norm, pool_w, pool_scale,
           w_branch, w_out, final_norm):
    w_inproj = _pack_w_in(w_in)
    packed = [_pack_mla(mla_w_uq[l], mla_w_ukv[l]) for l in range(DEPTH)]
    rest = (norm_w, mla_q_norm, mla_kv_norm, diff_lambda, diff_subln, ssd_conv_w, ssd_conv_b, ssd_dt_bias,
            ssd_a_log, ssd_d, ssd_norm, pool_w, pool_scale, w_branch, w_out, final_norm)
    return (_trunk(x_prompt, w_inproj, packed, *rest), _trunk(x_sample, w_inproj, packed, *rest))
```

```python
import functools
import math

import numpy as np
import jax
import jax.numpy as jnp
from jax import lax
from jax.experimental import pallas as pl
from jax.experimental.pallas import tpu as pltpu

F32 = jnp.float32
BF16 = jnp.bfloat16
HIGHEST = lax.Precision.HIGHEST

D_MODEL = 2048
DEPTH = 2
BRANCH_W = D_MODEL // 2
N_BRANCH = 4
ROPE_THETA = 500000.0
EPS = 1e-6
MLA_HEADS = 8
MLA_NOPE = 128
MLA_ROPE = 64
MLA_V = BRANCH_W // MLA_HEADS
MLA_Q_LORA = 512
MLA_KV_LORA = 256
DIFF_HEADS = 4
DIFF_HD = BRANCH_W // (2 * DIFF_HEADS)
DIFF_ROT = DIFF_HD // 4
SSD_P = 64
SSD_HEADS = BRANCH_W // SSD_P
SSD_N = 128
SSD_G = 2
SSD_CONV = 4
SSD_CHUNK = 128
SSD_BLOCK = 1024
SEQ_TILE = 1024
SSD_CONV_DIM = BRANCH_W + 2 * SSD_G * SSD_N
POOL_WINDOWS = (2, 4, 8, 16)
POOL_GROUP = BRANCH_W // 4
IN_SIZES = (MLA_Q_LORA, MLA_KV_LORA, MLA_ROPE, BRANCH_W, BRANCH_W, BRANCH_W, BRANCH_W, BRANCH_W,
            BRANCH_W, SSD_CONV_DIM, 2 * SSD_HEADS, BRANCH_W, BRANCH_W, N_BRANCH * D_MODEL)
IN_OFFS = tuple(int(v) for v in np.cumsum((0,) + IN_SIZES))

LANE = 128
HALO = 16
VMEM_LIMIT = 56 * 1024 * 1024

OFF_GMLA = 0
OFF_DQ = 1 * BRANCH_W
OFF_DK = 2 * BRANCH_W
OFF_DV = 3 * BRANCH_W
OFF_GDIFF = 4 * BRANCH_W
OFF_Z = 5 * BRANCH_W
OFF_XS = 6 * BRANCH_W
OFF_U = 7 * BRANCH_W
OFF_GPOOL = 8 * BRANCH_W
OFF_MG = 9 * BRANCH_W
OFF_CQ = OFF_MG + N_BRANCH * D_MODEL
OFF_BC = OFF_CQ + MLA_Q_LORA
OFF_CKV = OFF_BC + 2 * SSD_G * SSD_N
OFF_KR = OFF_CKV + MLA_KV_LORA
OFF_KRP = OFF_KR + LANE
NP_MAIN = OFF_KRP + LANE
DT_W = 2 * LANE

ATTN_TQ = 2048
ATTN_BUFS = 2
ATTN_SUB = 256
MLA_QK = 2 * LANE
LOG2E = math.log2(math.e)


def _cparams(sem, vmem=VMEM_LIMIT):
    return pltpu.CompilerParams(dimension_semantics=sem, vmem_limit_bytes=vmem)


def _tile(n, pref):
    t = min(n, pref)
    while n % t:
        t //= 2
    return t


def _silu(x):
    return x * jax.nn.sigmoid(x)


def _softplus(x):
    return jnp.maximum(x, 0.0) + jnp.log1p(jnp.exp(-jnp.abs(x)))


_NT = (((1,), (1,)), ((), ()))
INPROJ_TN = 1024
TAIL_W = NP_MAIN - OFF_CQ

_N_TILE_A = OFF_U // INPROJ_TN
_N_TILE_MAIN = OFF_CQ // INPROJ_TN
_N_TILE = _N_TILE_MAIN + 2
NP_OUT = _N_TILE * INPROJ_TN


def _inproj_src_row(j):
    return jnp.where(j < _N_TILE_A, IN_OFFS[3] + j * INPROJ_TN,
                     jnp.where(j < _N_TILE_MAIN, IN_OFFS[11] + (j - _N_TILE_A) * INPROJ_TN, IN_OFFS[0]))


def _inproj_kernel(x_ref, nw_ref, w_ref, wtail_ref, wdt_ref, o_ref, dt_ref, h_ref):
    j = pl.program_id(1)
    tn = INPROJ_TN

    @pl.when(j == 0)
    def _():
        x = x_ref[...]
        y = x * lax.rsqrt(jnp.mean(x * x, axis=-1, keepdims=True) + EPS)
        hb = (y * nw_ref[...]).astype(BF16)
        h_ref[...] = hb
        dt_ref[...] = lax.dot_general(hb, wdt_ref[...], _NT, preferred_element_type=F32)

    @pl.when(j < _N_TILE_MAIN)
    def _():
        o_ref[...] = lax.dot_general(h_ref[...], w_ref[...].astype(BF16), _NT,
                                     preferred_element_type=F32).astype(BF16)

    @pl.when(j == _N_TILE_MAIN)
    def _():
        o_ref[...] = lax.dot_general(h_ref[...], wtail_ref[0:tn, :], _NT,
                                     preferred_element_type=F32).astype(BF16)

    @pl.when(j == _N_TILE_MAIN + 1)
    def _():
        rest = TAIL_W - tn
        o_ref[:, 0:rest] = lax.dot_general(h_ref[...], wtail_ref[tn:TAIL_W, :], _NT,
                                           preferred_element_type=F32).astype(BF16)
        o_ref[:, rest:] = jnp.zeros((o_ref.shape[0], tn - rest), BF16)


def _inproj(x2, norm_w, wt, w_tail, w_dt, l):
    t = x2.shape[0]
    tm = _tile(t, 1024)
    tn = INPROJ_TN
    return pl.pallas_call(
        _inproj_kernel,
        grid=(t // tm, _N_TILE),
        in_specs=[
            pl.BlockSpec((tm, D_MODEL), lambda i, j: (i, 0)),
            pl.BlockSpec((1, D_MODEL), lambda i, j: (0, 0)),
            pl.BlockSpec((pl.Element(tn), pl.Element(D_MODEL)),
                         lambda i, j: (pl.multiple_of(l * IN_OFFS[-1] + _inproj_src_row(j), HALO), 0)),
            pl.BlockSpec((None, TAIL_W, D_MODEL), lambda i, j: (l, 0, 0)),
            pl.BlockSpec((None, DT_W, D_MODEL), lambda i, j: (l, 0, 0)),
        ],
        out_specs=[
            pl.BlockSpec((tm, tn), lambda i, j: (i, j)),
            pl.BlockSpec((tm, DT_W), lambda i, j: (i, 0)),
        ],
        out_shape=[jax.ShapeDtypeStruct((t, NP_OUT), BF16), jax.ShapeDtypeStruct((t, DT_W), F32)],
        scratch_shapes=[pltpu.VMEM((tm, D_MODEL), BF16)],
        compiler_params=_cparams(("parallel", "arbitrary")),
        name="inproj",
    )(x2, norm_w.reshape(1, D_MODEL), wt.reshape(-1, D_MODEL), w_tail, w_dt)


def _mla_prep_kernel(cq_ref, ckv_ref, kr_ref, krp_ref, cos_ref, sin_ref, qn_ref, kvn_ref,
                     wq_ref, wqp_ref, wk_ref, wv_ref, q_ref, k_ref, v_ref):
    cos = cos_ref[...]
    sin = sin_ref[...]
    cq = cq_ref[...].astype(F32)
    ncq = (cq * lax.rsqrt(jnp.mean(cq * cq, axis=-1, keepdims=True) + EPS) * qn_ref[...]).astype(BF16)
    qm = jnp.dot(ncq, wq_ref[...], preferred_element_type=F32)
    qp = jnp.dot(ncq, wqp_ref[...], preferred_element_type=F32)
    ckv = ckv_ref[...].astype(F32)
    nkv = (ckv * lax.rsqrt(jnp.mean(ckv * ckv, axis=-1, keepdims=True) + EPS) * kvn_ref[...]).astype(BF16)
    kn = jnp.dot(nkv, wk_ref[...], preferred_element_type=F32)
    v_ref[...] = jnp.dot(nkv, wv_ref[...], preferred_element_type=F32).astype(BF16)
    kpe = (kr_ref[...].astype(F32) * cos + krp_ref[...].astype(F32) * sin).astype(BF16)
    qscale = (MLA_NOPE + MLA_ROPE) ** -0.5 * LOG2E
    for h in range(MLA_HEADS):
        lo = h * MLA_QK
        q_ref[:, lo:lo + LANE] = (qm[:, lo:lo + LANE] * qscale).astype(BF16)
        qr = qm[:, lo + LANE:lo + 2 * LANE] * cos + qp[:, h * LANE:(h + 1) * LANE] * sin
        q_ref[:, lo + LANE:lo + 2 * LANE] = (qr * qscale).astype(BF16)
        k_ref[:, lo:lo + LANE] = kn[:, h * LANE:(h + 1) * LANE].astype(BF16)
        k_ref[:, lo + LANE:lo + 2 * LANE] = kpe


def _mla_prep(proj, s, cos_t, sin_t, q_norm, kv_norm, wq, wqp, wk, wv):
    t = proj.shape[0]
    tm = _tile(s, SEQ_TILE)
    ns = s // tm
    hq = MLA_HEADS * MLA_QK
    const = lambda shape: pl.BlockSpec(shape, lambda i: (0, 0))
    return pl.pallas_call(
        _mla_prep_kernel,
        grid=(t // tm,),
        in_specs=[
            pl.BlockSpec((tm, MLA_Q_LORA), lambda i: (i, OFF_CQ // MLA_Q_LORA)),
            pl.BlockSpec((tm, MLA_KV_LORA), lambda i: (i, OFF_CKV // MLA_KV_LORA)),
            pl.BlockSpec((tm, LANE), lambda i: (i, OFF_KR // LANE)),
            pl.BlockSpec((tm, LANE), lambda i: (i, OFF_KRP // LANE)),
            pl.BlockSpec((tm, LANE), lambda i: (i % ns, 0)),
            pl.BlockSpec((tm, LANE), lambda i: (i % ns, 0)),
            const((1, MLA_Q_LORA)),
            const((1, MLA_KV_LORA)),
            const((MLA_Q_LORA, hq)),
            const((MLA_Q_LORA, MLA_HEADS * LANE)),
            const((MLA_KV_LORA, MLA_HEADS * MLA_NOPE)),
            const((MLA_KV_LORA, MLA_HEADS * MLA_V)),
        ],
        out_specs=[
            pl.BlockSpec((tm, hq), lambda i: (i, 0)),
            pl.BlockSpec((tm, hq), lambda i: (i, 0)),
            pl.BlockSpec((tm, MLA_HEADS * MLA_V), lambda i: (i, 0)),
        ],
        out_shape=[jax.ShapeDtypeStruct((t, hq), BF16), jax.ShapeDtypeStruct((t, hq), BF16),
                   jax.ShapeDtypeStruct((t, MLA_HEADS * MLA_V), BF16)],
        compiler_params=_cparams(("parallel",)),
        name="mla_prep",
    )(proj, proj, proj, proj, cos_t, sin_t, q_norm.reshape(1, -1), kv_norm.reshape(1, -1), wq, wqp, wk, wv)


def _scores(q, k_ref):
    return lax.dot_general(q, k_ref[...], (((1,), (1,)), ((), ())), preferred_element_type=F32)


def _softmax_pv_from(sc, v_ref):
    m = jnp.max(sc, axis=-1, keepdims=True)
    p = jnp.exp2(sc - m)
    l = jnp.sum(p, axis=-1, keepdims=True)
    return jnp.dot(p.astype(BF16), v_ref[...], preferred_element_type=F32), l


def _staged(n_jobs, bufs, score_fn, consume_fn):
    ahead = len(bufs) - 1
    for n in range(min(ahead, n_jobs)):
        bufs[n % len(bufs)][...] = score_fn(n)
    for n in range(n_jobs):
        if n + ahead < n_jobs:
            bufs[(n + ahead) % len(bufs)][...] = score_fn(n + ahead)
        consume_fn(n, bufs[n % len(bufs)][...])


def _mla_attn_kernel(q_ref, k_ref, v_ref, g_ref, o_ref, *bufs, sub):
    def consume(n, sc):
        r = n * sub
        acc, l = _softmax_pv_from(sc, v_ref)
        o_ref[r:r + sub, :] = ((acc / l) * _silu(g_ref[r:r + sub, :].astype(F32))).astype(BF16)

    _staged(q_ref.shape[0] // sub, bufs, lambda n: _scores(q_ref[n * sub:(n + 1) * sub, :], k_ref), consume)


def _mla_attn(q, k, v, proj3):
    b, s, _ = q.shape
    tq = _tile(s, ATTN_TQ)
    return pl.pallas_call(
        functools.partial(_mla_attn_kernel, sub=min(tq, ATTN_SUB)),
        grid=(b, MLA_HEADS, s // tq),
        in_specs=[
            pl.BlockSpec((None, tq, MLA_QK), lambda bi, h, qi: (bi, qi, h)),
            pl.BlockSpec((None, s, MLA_QK), lambda bi, h, qi: (bi, 0, h)),
            pl.BlockSpec((None, s, MLA_V), lambda bi, h, qi: (bi, 0, h)),
            pl.BlockSpec((None, tq, MLA_V), lambda bi, h, qi: (bi, qi, OFF_GMLA // MLA_V + h)),
        ],
        out_specs=pl.BlockSpec((None, tq, MLA_V), lambda bi, h, qi: (bi, qi, h)),
        out_shape=jax.ShapeDtypeStruct((b, s, BRANCH_W), BF16),
        scratch_shapes=[pltpu.VMEM((min(tq, ATTN_SUB), s), F32)] * ATTN_BUFS,
        compiler_params=_cparams(("parallel", "parallel", "parallel")),
        name="mla_attn",
    )(q, k, v, proj3)


def _diff_rope(x, cos, sa, sb):
    half = DIFF_ROT // 2
    return x * cos + pltpu.roll(x, half, 1) * sa + pltpu.roll(x, DIFF_HD - half, 1) * sb


def _diff_attn_kernel(q1_ref, q2_ref, k1_ref, k2_ref, v_ref, g_ref, lam_ref, sub_ref,
                      cos_ref, sa_ref, sb_ref, o_ref, kr1_ref, kr2_ref, *bufs,
                      sub, lambda_init):
    lp = lam_ref[...]
    lam = (jnp.exp(jnp.sum(lp[0:1] * lp[1:2], axis=-1, keepdims=True))
           - jnp.exp(jnp.sum(lp[2:3] * lp[3:4], axis=-1, keepdims=True)) + lambda_init)
    for k_ref, kr_ref in ((k1_ref, kr1_ref), (k2_ref, kr2_ref)):
        kr_ref[...] = _diff_rope(k_ref[...].astype(F32), cos_ref[...], sa_ref[...], sb_ref[...]).astype(BF16)
    q0 = pl.program_id(2) * q1_ref.shape[0]
    qscale = DIFF_HD ** -0.5 * LOG2E
    qk = ((q1_ref, kr1_ref), (q2_ref, kr2_ref))
    first = {}

    def scores(n):
        q_ref, kr_ref = qk[n % 2]
        rows = slice((n // 2) * sub, (n // 2 + 1) * sub)
        pos = pl.ds(pl.multiple_of(q0 + rows.start, sub), sub)
        q = _diff_rope(q_ref[rows, :].astype(F32), cos_ref[pos, :], sa_ref[pos, :], sb_ref[pos, :])
        return _scores((q * qscale).astype(BF16), kr_ref)

    def consume(n, sc):
        acc, l = _softmax_pv_from(sc, v_ref)
        if n % 2 == 0:
            first[n // 2] = acc / l
            return
        r = (n // 2) * sub
        o = first.pop(n // 2) - lam * (acc / l)
        o = o * lax.rsqrt(jnp.mean(o * o, axis=-1, keepdims=True) + EPS) * sub_ref[...]
        o = o * (1.0 - lambda_init)
        o_ref[r:r + sub, :] = (o * _silu(g_ref[r:r + sub, :].astype(F32))).astype(BF16)

    _staged(2 * (q1_ref.shape[0] // sub), bufs, scores, consume)


def _diff_attn(proj3, tables, lam_params, subln, lambda_init):
    b, s, _ = proj3.shape
    tq = _tile(s, ATTN_TQ)
    dv = 2 * DIFF_HD
    hq = OFF_DQ // DIFF_HD
    hk = OFF_DK // DIFF_HD
    tab = pl.BlockSpec((s, LANE), lambda bi, h, qi: (0, 0))
    return pl.pallas_call(
        functools.partial(_diff_attn_kernel, sub=min(tq, ATTN_SUB), lambda_init=lambda_init),
        grid=(b, DIFF_HEADS, s // tq),
        in_specs=[
            pl.BlockSpec((None, tq, DIFF_HD), lambda bi, h, qi: (bi, qi, hq + 2 * h)),
            pl.BlockSpec((None, tq, DIFF_HD), lambda bi, h, qi: (bi, qi, hq + 2 * h + 1)),
            pl.BlockSpec((None, s, DIFF_HD), lambda bi, h, qi: (bi, 0, hk + 2 * h)),
            pl.BlockSpec((None, s, DIFF_HD), lambda bi, h, qi: (bi, 0, hk + 2 * h + 1)),
            pl.BlockSpec((None, s, dv), lambda bi, h, qi: (bi, 0, OFF_DV // dv + h)),
            pl.BlockSpec((None, tq, dv), lambda bi, h, qi: (bi, qi, OFF_GDIFF // dv + h)),
            pl.BlockSpec((4, DIFF_HD), lambda bi, h, qi: (0, 0)),
            pl.BlockSpec((1, dv), lambda bi, h, qi: (0, 0)),
            tab, tab, tab,
        ],
        out_specs=pl.BlockSpec((None, tq, dv), lambda bi, h, qi: (bi, qi, h)),
        out_shape=jax.ShapeDtypeStruct((b, s, BRANCH_W), BF16),
        scratch_shapes=[pltpu.VMEM((s, DIFF_HD), BF16)] * 2
        + [pltpu.VMEM((min(tq, ATTN_SUB), s), F32)] * ATTN_BUFS,
        compiler_params=_cparams(("parallel", "parallel", "parallel")),
        name="diff_attn",
    )(proj3, proj3, proj3, proj3, proj3, proj3, lam_params, subln.reshape(1, dv), *tables)


def _ssd_conv_kernel(xs_ref, xsp_ref, xsn_ref, bc_ref, bcp_ref, bcn_ref, w_ref, b_ref,
                     oxs_ref, obc_ref, ext_ref, *, tm):
    i = pl.program_id(1)
    has_prev = i > 0
    has_next = i < pl.num_programs(1) - 1
    pad_l = SSD_CONV // 2

    def conv(cur_ref, prev_ref, next_ref, out_ref, c0, width):
        ext_ref[0:HALO, 0:width] = jnp.where(has_prev, prev_ref[...].astype(F32), 0.0)
        ext_ref[HALO:HALO + tm, 0:width] = cur_ref[...].astype(F32)
        ext_ref[HALO + tm:, 0:width] = jnp.where(has_next, next_ref[...].astype(F32), 0.0)
        acc = b_ref[:, c0:c0 + width]
        for j in range(SSD_CONV):
            off = HALO - pad_l + j
            acc = acc + ext_ref[off:off + tm, 0:width] * w_ref[j:j + 1, c0:c0 + width]
        out_ref[...] = _silu(acc).astype(BF16)

    conv(xs_ref, xsp_ref, xsn_ref, oxs_ref, 0, BRANCH_W)
    conv(bc_ref, bcp_ref, bcn_ref, obc_ref, BRANCH_W, SSD_CONV_DIM - BRANCH_W)


def _halo_specs(tm, s, width, col):
    r = tm // HALO
    last = s // HALO - 1
    cur = pl.BlockSpec((None, tm, width), lambda bi, i: (bi, i, col))
    prev = pl.BlockSpec((None, HALO, width), lambda bi, i: (bi, jnp.maximum(i * r - 1, 0), col))
    nxt = pl.BlockSpec((None, HALO, width), lambda bi, i: (bi, jnp.minimum((i + 1) * r, last), col))
    return [cur, prev, nxt]


def _ssd_conv(proj3, conv_w, conv_b):
    b, s, _ = proj3.shape
    tm = _tile(s, SEQ_TILE)
    wbc = SSD_CONV_DIM - BRANCH_W
    return pl.pallas_call(
        functools.partial(_ssd_conv_kernel, tm=tm),
        grid=(b, s // tm),
        in_specs=(_halo_specs(tm, s, BRANCH_W, OFF_XS // BRANCH_W) + _halo_specs(tm, s, wbc, OFF_BC // wbc)
                  + [pl.BlockSpec((SSD_CONV, SSD_CONV_DIM), lambda bi, i: (0, 0)),
                     pl.BlockSpec((1, SSD_CONV_DIM), lambda bi, i: (0, 0))]),
        out_specs=[pl.BlockSpec((None, tm, BRANCH_W), lambda bi, i: (bi, i, 0)),
                   pl.BlockSpec((None, tm, wbc), lambda bi, i: (bi, i, 0))],
        out_shape=[jax.ShapeDtypeStruct((b, s, BRANCH_W), BF16), jax.ShapeDtypeStruct((b, s, wbc), BF16)],
        scratch_shapes=[pltpu.VMEM((tm + 2 * HALO, BRANCH_W), F32)],
        compiler_params=_cparams(("parallel", "parallel")),
        name="ssd_conv",
    )(proj3, proj3, proj3, proj3, proj3, proj3, conv_w, conv_b.reshape(1, -1))


def _ssd_chunk(xs_b, bc, dt_raw, bias, alog, st_ref, reverse):
    ln = SSD_CHUNK
    gw = BRANCH_W // SSD_G
    xs = xs_b.astype(F32)
    dt = _softplus(dt_raw + bias)
    da = dt * (-jnp.exp(alog))
    ri = lax.broadcasted_iota(jnp.int32, (ln, ln), 0)
    ci = lax.broadcasted_iota(jnp.int32, (ln, ln), 1)
    tri = (ci >= ri) if reverse else (ri >= ci)
    cs = jnp.dot(tri.astype(F32), da, precision=HIGHEST, preferred_element_type=F32)
    cs_end = cs[0:1, :] if reverse else cs[ln - 1:ln, :]
    er = lax.broadcasted_iota(jnp.int32, (2 * LANE, BRANCH_W), 0)
    ec = lax.broadcasted_iota(jnp.int32, (2 * LANE, BRANCH_W), 1)
    expand = jnp.where(ec // SSD_P == er % LANE, 1.0, 0.0).astype(BF16)
    cols = jnp.concatenate([dt, jnp.exp(cs), jnp.exp(cs_end - cs)], axis=0)
    hi = cols.astype(BF16)
    lo = (cols - hi.astype(F32)).astype(BF16)
    wide = jnp.dot(jnp.concatenate([hi, lo], axis=1), expand, preferred_element_type=F32)
    dt_e = wide[0:ln]
    dec_out_e = wide[ln:2 * ln]
    dec_st_e = wide[2 * ln:3 * ln]
    tot_e = dec_out_e[0:1, :] if reverse else dec_out_e[ln - 1:ln, :]
    cs_t = cs.T
    xd = xs * dt_e
    xdb = xd.astype(BF16)
    xst = (xd * dec_st_e).astype(BF16)
    low = lax.broadcasted_iota(jnp.int32, (ln, LANE), 1) < SSD_P
    zero = jnp.zeros((ln, LANE), BF16)
    outs = []
    for g in range(SSD_G):
        bg = bc[:, g * SSD_N:(g + 1) * SSD_N]
        cg = bc[:, (SSD_G + g) * SSD_N:(SSD_G + g + 1) * SSD_N]
        cb = lax.dot_general(cg, bg, (((1,), (1,)), ((), ())), preferred_element_type=F32)
        st = st_ref[g]
        y_off = jnp.dot(cg, st.astype(BF16), preferred_element_type=F32) * dec_out_e[:, g * gw:(g + 1) * gw]
        bt = bg.astype(F32).T.astype(BF16)
        st_ref[g] = (st * tot_e[:, g * gw:(g + 1) * gw]
                     + jnp.dot(bt, xst[:, g * gw:(g + 1) * gw], preferred_element_type=F32))
        for kp in range(gw // LANE):
            mats = []
            for hh in range(2):
                hd = g * (SSD_HEADS // SSD_G) + 2 * kp + hh
                seg = cs[:, hd:hd + 1] - cs_t[hd:hd + 1, :]
                lm = jnp.exp(jnp.where(tri, seg, -jnp.inf))
                mats.append((cb * lm).astype(BF16))
            lhs = jnp.concatenate(mats, axis=1)
            col = g * gw + kp * LANE
            xp = xdb[:, col:col + LANE]
            rhs = jnp.concatenate([jnp.where(low, xp, zero), jnp.where(low, zero, xp)], axis=0)
            y_diag = jnp.dot(lhs, rhs, preferred_element_type=F32)
            outs.append(y_diag + y_off[:, kp * LANE:(kp + 1) * LANE])
    return jnp.concatenate(outs, axis=1), xs


def _chunk_rows(n_rows, reverse):
    starts = range(0, n_rows, SSD_CHUNK)
    return [slice(r, r + SSD_CHUNK) for r in (reversed(starts) if reverse else starts)]


def _ssd_fwd_kernel(xs_ref, bc_ref, dt_ref, bias_ref, alog_ref, y_ref, st_ref):
    @pl.when(pl.program_id(1) == 0)
    def _():
        st_ref[...] = jnp.zeros_like(st_ref)

    for rows in _chunk_rows(xs_ref.shape[0], reverse=False):
        y, _ = _ssd_chunk(xs_ref[rows, :], bc_ref[rows, :], dt_ref[rows, :], bias_ref[...], alog_ref[...],
                          st_ref, reverse=False)
        y_ref[rows, :] = y


def _ssd_bwd_kernel(xs_ref, bc_ref, dt_ref, bias_ref, alog_ref, yf_ref, z_ref, d_ref, nw_ref,
                    o_ref, st_ref):
    @pl.when(pl.program_id(1) == 0)
    def _():
        st_ref[...] = jnp.zeros_like(st_ref)

    gw = BRANCH_W // SSD_G
    for rows in _chunk_rows(xs_ref.shape[0], reverse=True):
        yb, xs = _ssd_chunk(xs_ref[rows, :], bc_ref[rows, :], dt_ref[rows, :], bias_ref[...], alog_ref[...],
                            st_ref, reverse=True)
        y = yf_ref[rows, :] + yb + xs * d_ref[...]
        y = y * _silu(z_ref[rows, :].astype(F32))
        for g in range(SSD_G):
            yg = y[:, g * gw:(g + 1) * gw]
            yn = yg * lax.rsqrt(jnp.mean(yg * yg, axis=-1, keepdims=True) + EPS)
            o_ref[rows, g * gw:(g + 1) * gw] = (yn * nw_ref[:, g * gw:(g + 1) * gw]).astype(BF16)


def _ssd_scan(xs_act, bc_act, dt3, proj3, dt_bias, a_log, d_skip, norm_w):
    b, s, _ = xs_act.shape
    tb = _tile(s, SSD_BLOCK)
    nc = s // tb
    wbc = SSD_CONV_DIM - BRANCH_W
    gw = BRANCH_W // SSD_G
    pad = lambda v: jnp.pad(v.astype(F32), (0, LANE - SSD_HEADS)).reshape(1, LANE)
    scratch = [pltpu.VMEM((SSD_G, SSD_N, gw), F32)]
    vec = lambda w: pl.BlockSpec((1, w), lambda bi, c: (0, 0))

    def specs(cmap, d):
        return [pl.BlockSpec((None, tb,BRANCH_W), lambda bi, c: (bi, cmap(c), 0)),
                pl.BlockSpec((None, tb,wbc), lambda bi, c: (bi, cmap(c), 0)),
                pl.BlockSpec((None, tb,LANE), lambda bi, c: (bi, cmap(c), d)),
                vec(LANE), vec(LANE)]

    y_f = pl.pallas_call(
        _ssd_fwd_kernel,
        grid=(b, nc),
        in_specs=specs(lambda c: c, 0),
        out_specs=pl.BlockSpec((None, tb,BRANCH_W), lambda bi, c: (bi, c, 0)),
        out_shape=jax.ShapeDtypeStruct((b, s, BRANCH_W), F32),
        scratch_shapes=scratch,
        compiler_params=_cparams(("parallel", "arbitrary")),
        name="ssd_fwd",
    )(xs_act, bc_act, dt3, pad(dt_bias[0]), pad(a_log[0]))

    rev = lambda c: nc - 1 - c
    return pl.pallas_call(
        _ssd_bwd_kernel,
        grid=(b, nc),
        in_specs=specs(rev, 1) + [
            pl.BlockSpec((None, tb,BRANCH_W), lambda bi, c: (bi, rev(c), 0)),
            pl.BlockSpec((None, tb,BRANCH_W), lambda bi, c: (bi, rev(c), OFF_Z // BRANCH_W)),
            vec(BRANCH_W), vec(BRANCH_W)],
        out_specs=pl.BlockSpec((None, tb,BRANCH_W), lambda bi, c: (bi, rev(c), 0)),
        out_shape=jax.ShapeDtypeStruct((b, s, BRANCH_W), BF16),
        scratch_shapes=scratch,
        compiler_params=_cparams(("parallel", "arbitrary")),
        name="ssd_bwd",
    )(xs_act, bc_act, dt3, pad(dt_bias[1]), pad(a_log[1]), y_f, proj3,
      jnp.repeat(d_skip.astype(F32), SSD_P).reshape(1, BRANCH_W), norm_w.reshape(1, BRANCH_W))


def _pool_kernel(u_ref, up_ref, un_ref, g_ref, w_ref, sc_ref, o_ref, ext_ref, *, tm, s):
    i = pl.program_id(1)
    cur = u_ref[...].astype(F32)
    ext_ref[0:HALO, :] = jnp.where(i > 0, up_ref[...].astype(F32), 0.0)
    ext_ref[HALO:HALO + tm, :] = cur
    ext_ref[HALO + tm:, :] = jnp.where(i < pl.num_programs(1) - 1, un_ref[...].astype(F32), 0.0)
    pos = i * tm + lax.broadcasted_iota(jnp.int32, (tm, 1), 0)
    for gi, w in enumerate(POOL_WINDOWS):
        lo = w // 2
        hi = w - 1 - lo
        sl = slice(gi * POOL_GROUP, (gi + 1) * POOL_GROUP)
        acc = ext_ref[HALO - lo:HALO - lo + tm, sl]
        for d in range(-lo + 1, hi + 1):
            acc = acc + ext_ref[HALO + d:HALO + d + tm, sl]
        cnt = (jnp.minimum(pos + hi + 1, s) - jnp.maximum(pos - lo, 0)).astype(F32)
        pooled = acc / cnt - cur[:, sl]
        mixed = jnp.dot(pooled.astype(BF16), w_ref[gi], preferred_element_type=F32)
        o_ref[:, sl] = (mixed * sc_ref[:, sl] * _silu(g_ref[:, sl].astype(F32))).astype(BF16)


def _pool(proj3, pool_w, pool_scale):
    b, s, _ = proj3.shape
    tm = _tile(s, SEQ_TILE)
    return pl.pallas_call(
        functools.partial(_pool_kernel, tm=tm, s=s),
        grid=(b, s // tm),
        in_specs=_halo_specs(tm, s, BRANCH_W, OFF_U // BRANCH_W) + [
            pl.BlockSpec((None, tm, BRANCH_W), lambda bi, i: (bi, i, OFF_GPOOL // BRANCH_W)),
            pl.BlockSpec((len(POOL_WINDOWS), POOL_GROUP, POOL_GROUP), lambda bi, i: (0, 0, 0)),
            pl.BlockSpec((1, BRANCH_W), lambda bi, i: (0, 0))],
        out_specs=pl.BlockSpec((None, tm, BRANCH_W), lambda bi, i: (bi, i, 0)),
        out_shape=jax.ShapeDtypeStruct((b, s, BRANCH_W), BF16),
        scratch_shapes=[pltpu.VMEM((tm + 2 * HALO, BRANCH_W), F32)],
        compiler_params=_cparams(("parallel", "parallel")),
        name="pool",
    )(proj3, proj3, proj3, proj3, pool_w, pool_scale.reshape(1, BRANCH_W))


def _merge_kernel(*refs):
    brs, ws, gs, o_ref = refs[0:4], refs[4:8], refs[8:12], refs[12]
    acc = None
    for br, w, g in zip(brs, ws, gs):
        term = jax.nn.sigmoid(g[...].astype(F32)) * jnp.dot(br[...], w[...], preferred_element_type=F32)
        acc = term if acc is None else acc + term
    o_ref[...] = acc.astype(BF16)


def _merge(branches, w_branch, proj):
    t = proj.shape[0]
    tm = _tile(t, 1024)
    tn = 512
    br_spec = pl.BlockSpec((tm, BRANCH_W), lambda i, j: (i, 0))
    w_specs = [pl.BlockSpec((None, BRANCH_W, tn), lambda i, j, k=k: (k, 0, j)) for k in range(N_BRANCH)]
    g_specs = [pl.BlockSpec((tm, tn), lambda i, j, k=k: (i, (OFF_MG + k * D_MODEL) // tn + j))
               for k in range(N_BRANCH)]
    return pl.pallas_call(
        _merge_kernel,
        grid=(t // tm, D_MODEL // tn),
        in_specs=[br_spec] * N_BRANCH + w_specs + g_specs,
        out_specs=pl.BlockSpec((tm, tn), lambda i, j: (i, j)),
        out_shape=jax.ShapeDtypeStruct((t, D_MODEL), BF16),
        compiler_params=_cparams(("parallel", "parallel")),
        name="merge",
    )(*branches, *([w_branch] * N_BRANCH), *([proj] * N_BRANCH))


def _out_kernel(m_ref, w_ref, x_ref, fn_ref, o_ref, *, final):
    y = x_ref[...] + jnp.dot(m_ref[...], w_ref[...], preferred_element_type=F32)
    if final:
        y = y * lax.rsqrt(jnp.mean(y * y, axis=-1, keepdims=True) + EPS) * fn_ref[...]
    o_ref[...] = y


def _out_proj(merged, w_out, x2, final_norm, final):
    t = x2.shape[0]
    tm = _tile(t, 512)
    return pl.pallas_call(
        functools.partial(_out_kernel, final=final),
        grid=(t // tm,),
        in_specs=[pl.BlockSpec((tm, D_MODEL), lambda i: (i, 0)),
                  pl.BlockSpec((D_MODEL, D_MODEL), lambda i: (0, 0)),
                  pl.BlockSpec((tm, D_MODEL), lambda i: (i, 0)),
                  pl.BlockSpec((1, D_MODEL), lambda i: (0, 0))],
        out_specs=pl.BlockSpec((tm, D_MODEL), lambda i: (i, 0)),
        out_shape=jax.ShapeDtypeStruct((t, D_MODEL), F32),
        compiler_params=_cparams(("parallel",)),
        name="out_proj",
    )(merged, w_out, x2, final_norm.reshape(1, D_MODEL))


def _pack_w_in(w_in):
    wt = jnp.swapaxes(w_in, 1, 2)
    depth = wt.shape[0]
    zeros = lambda n: jnp.zeros((depth, n, D_MODEL), w_in.dtype)
    half = MLA_ROPE // 2
    kr0 = IN_OFFS[2]
    bc0 = IN_OFFS[9] + BRANCH_W
    w_tail = jnp.concatenate([
        wt[:, IN_OFFS[0]:IN_OFFS[1]],
        wt[:, bc0:IN_OFFS[10]],
        wt[:, IN_OFFS[1]:IN_OFFS[2]],
        wt[:, kr0:kr0 + MLA_ROPE], zeros(LANE - MLA_ROPE),
        wt[:, kr0 + half:kr0 + MLA_ROPE], wt[:, kr0:kr0 + half], zeros(LANE - MLA_ROPE),
    ], axis=1)
    dt0 = IN_OFFS[10]
    w_dt = jnp.concatenate([wt[:, dt0:dt0 + SSD_HEADS], zeros(LANE - SSD_HEADS),
                            wt[:, dt0 + SSD_HEADS:dt0 + 2 * SSD_HEADS], zeros(LANE - SSD_HEADS)], axis=1)
    w_tail, w_dt = lax.optimization_barrier((w_tail, w_dt))
    return wt, w_tail.astype(BF16), w_dt.astype(BF16)


def _pack_mla(w_uq, w_ukv):
    half = MLA_ROPE // 2
    wq3 = w_uq.reshape(MLA_Q_LORA, MLA_HEADS, MLA_NOPE + MLA_ROPE)
    zq = jnp.zeros((MLA_Q_LORA, MLA_HEADS, LANE - MLA_ROPE), w_uq.dtype)
    wq = jnp.concatenate([wq3, zq], axis=2).reshape(MLA_Q_LORA, MLA_HEADS * MLA_QK).astype(BF16)
    wqp = jnp.concatenate([wq3[:, :, MLA_NOPE + half:], wq3[:, :, MLA_NOPE:MLA_NOPE + half], zq],
                          axis=2).reshape(MLA_Q_LORA, MLA_HEADS * LANE).astype(BF16)
    wkv3 = w_ukv.reshape(MLA_KV_LORA, MLA_HEADS, MLA_NOPE + MLA_V)
    wk = wkv3[:, :, :MLA_NOPE].reshape(MLA_KV_LORA, MLA_HEADS * MLA_NOPE).astype(BF16)
    wv = wkv3[:, :, MLA_NOPE:].reshape(MLA_KV_LORA, MLA_HEADS * MLA_V).astype(BF16)
    return wq, wqp, wk, wv


def _rope_tables(s):
    def angles(rot):
        half = rot // 2
        inv_freq = jnp.power(ROPE_THETA, -jnp.arange(half, dtype=F32) * 2.0 / rot)
        ang = jnp.arange(s, dtype=F32)[:, None] * inv_freq[None, :]
        return jnp.cos(ang), jnp.sin(ang)

    c, sn = angles(MLA_ROPE)
    z = jnp.zeros((s, LANE - MLA_ROPE), F32)
    mla_cos = jnp.concatenate([c, c, z], axis=1)
    mla_sin = jnp.concatenate([-sn, sn, z], axis=1)
    c, sn = angles(DIFF_ROT)
    zh = jnp.zeros_like(sn)
    rest = LANE - DIFF_ROT
    d_cos = jnp.concatenate([c, c, jnp.ones((s, rest), F32)], axis=1)
    d_sa = jnp.concatenate([zh, sn, jnp.zeros((s, rest), F32)], axis=1)
    d_sb = jnp.concatenate([-sn, zh, jnp.zeros((s, rest), F32)], axis=1)
    return mla_cos, mla_sin, d_cos, d_sa, d_sb


def _trunk(x, w_inproj, packed, norm_w, mla_q_norm, mla_kv_norm, diff_lambda, diff_subln, ssd_conv_w, ssd_conv_b,
           ssd_dt_bias, ssd_a_log, ssd_d, ssd_norm, pool_w, pool_scale, w_branch, w_out, final_norm):
    b, s, _ = x.shape
    t = b * s
    mla_cos, mla_sin, d_cos, d_sa, d_sb = _rope_tables(s)
    x2 = x.reshape(t, D_MODEL)
    for l in range(DEPTH):
        wq, wqp, wk, wv = packed[l]
        lambda_init = 0.8 - 0.6 * math.exp(-0.3 * l)
        proj, dt = _inproj(x2, norm_w[l], *w_inproj, l)
        proj3 = proj.reshape(b, s, NP_OUT)
        q, k, v = _mla_prep(proj, s, mla_cos, mla_sin, mla_q_norm[l], mla_kv_norm[l], wq, wqp, wk, wv)
        br_mla = _mla_attn(q.reshape(b, s, -1), k.reshape(b, s, -1), v.reshape(b, s, -1), proj3)
        br_diff = _diff_attn(proj3, (d_cos, d_sa, d_sb), diff_lambda[l], diff_subln[l], lambda_init)
        xs_act, bc_act = _ssd_conv(proj3, ssd_conv_w[l], ssd_conv_b[l])
        br_ssd = _ssd_scan(xs_act, bc_act, dt.reshape(b, s, DT_W), proj3, ssd_dt_bias[l], ssd_a_log[l],
                           ssd_d[l], ssd_norm[l])
        br_pool = _pool(proj3, pool_w[l].astype(BF16), pool_scale[l])
        branches = [a.reshape(t, BRANCH_W) for a in (br_mla, br_diff, br_ssd, br_pool)]
        merged = _merge(branches, w_branch[l].astype(BF16), proj)
        x2 = _out_proj(merged, w_out[l].astype(BF16), x2, final_norm, final=(l == DEPTH - 1))
    return x2.reshape(b, s, D_MODEL)


def kernel(x_prompt, x_sample, norm_w, w_in, mla_q_norm, mla_w_uq, mla_kv_norm, mla_w_ukv, diff_lambda,
           diff_subln, ssd_conv_w, ssd_conv_b, ssd_dt_bias, ssd_a_log, ssd_d, ssd_norm, pool_w, pool_scale,
           w_branch, w_out, final_norm):
    w_inproj = _pack_w_in(w_in)
    packed = [_pack_mla(mla_w_uq[l], mla_w_ukv[l]) for l in range(DEPTH)]
    rest = (norm_w, mla_q_norm, mla_kv_norm, diff_lambda, diff_subln, ssd_conv_w, ssd_conv_b, ssd_dt_bias,
            ssd_a_log, ssd_d, ssd_norm, pool_w, pool_scale, w_branch, w_out, final_norm)
    return (_trunk(x_prompt, w_inproj, packed, *rest), _trunk(x_sample, w_inproj, packed, *rest))
```

```python
import functools
import math

import numpy as np
import jax
import jax.numpy as jnp
from jax import lax
from jax.experimental import pallas as pl
from jax.experimental.pallas import tpu as pltpu

F32 = jnp.float32
BF16 = jnp.bfloat16
HIGHEST = lax.Precision.HIGHEST

D_MODEL = 2048
DEPTH = 2
BRANCH_W = D_MODEL // 2
N_BRANCH = 4
ROPE_THETA = 500000.0
EPS = 1e-6
MLA_HEADS = 8
MLA_NOPE = 128
MLA_ROPE = 64
MLA_V = BRANCH_W // MLA_HEADS
MLA_Q_LORA = 512
MLA_KV_LORA = 256
DIFF_HEADS = 4
DIFF_HD = BRANCH_W // (2 * DIFF_HEADS)
DIFF_ROT = DIFF_HD // 4
SSD_P = 64
SSD_HEADS = BRANCH_W // SSD_P
SSD_N = 128
SSD_G = 2
SSD_CONV = 4
SSD_CHUNK = 128
SSD_BLOCK = 1024
SEQ_TILE = 1024
SSD_CONV_DIM = BRANCH_W + 2 * SSD_G * SSD_N
POOL_WINDOWS = (2, 4, 8, 16)
POOL_GROUP = BRANCH_W // 4
IN_SIZES = (MLA_Q_LORA, MLA_KV_LORA, MLA_ROPE, BRANCH_W, BRANCH_W, BRANCH_W, BRANCH_W, BRANCH_W,
            BRANCH_W, SSD_CONV_DIM, 2 * SSD_HEADS, BRANCH_W, BRANCH_W, N_BRANCH * D_MODEL)
IN_OFFS = tuple(int(v) for v in np.cumsum((0,) + IN_SIZES))

LANE = 128
HALO = 16
VMEM_LIMIT = 56 * 1024 * 1024

OFF_GMLA = 0
OFF_DQ = 1 * BRANCH_W
OFF_DK = 2 * BRANCH_W
OFF_DV = 3 * BRANCH_W
OFF_GDIFF = 4 * BRANCH_W
OFF_Z = 5 * BRANCH_W
OFF_XS = 6 * BRANCH_W
OFF_U = 7 * BRANCH_W
OFF_GPOOL = 8 * BRANCH_W
OFF_MG = 9 * BRANCH_W
OFF_CQ = OFF_MG + N_BRANCH * D_MODEL
OFF_BC = OFF_CQ + MLA_Q_LORA
OFF_CKV = OFF_BC + 2 * SSD_G * SSD_N
OFF_KR = OFF_CKV + MLA_KV_LORA
OFF_KRP = OFF_KR + LANE
NP_MAIN = OFF_KRP + LANE
DT_W = 2 * LANE

ATTN_TQ = 2048
ATTN_BUFS = 2
ATTN_SUB = 256
MLA_QK = 2 * LANE
LOG2E = math.log2(math.e)


def _cparams(sem, vmem=VMEM_LIMIT):
    return pltpu.CompilerParams(dimension_semantics=sem, vmem_limit_bytes=vmem)


def _tile(n, pref):
    t = min(n, pref)
    while n % t:
        t //= 2
    return t


def _silu(x):
    return x * jax.nn.sigmoid(x)


def _softplus(x):
    return jnp.maximum(x, 0.0) + jnp.log1p(jnp.exp(-jnp.abs(x)))


_NT = (((1,), (1,)), ((), ()))
INPROJ_TN = 1024
TAIL_W = NP_MAIN - OFF_CQ

_N_TILE_A = OFF_U // INPROJ_TN
_N_TILE_MAIN = OFF_CQ // INPROJ_TN
_N_TILE = _N_TILE_MAIN + 2
NP_OUT = _N_TILE * INPROJ_TN


def _inproj_src_row(j):
    return jnp.where(j < _N_TILE_A, IN_OFFS[3] + j * INPROJ_TN,
                     jnp.where(j < _N_TILE_MAIN, IN_OFFS[11] + (j - _N_TILE_A) * INPROJ_TN, IN_OFFS[0]))


def _inproj_kernel(x_ref, nw_ref, w_ref, wtail_ref, wdt_ref, o_ref, dt_ref, h_ref):
    j = pl.program_id(1)
    tn = INPROJ_TN

    @pl.when(j == 0)
    def _():
        x = x_ref[...]
        y = x * lax.rsqrt(jnp.mean(x * x, axis=-1, keepdims=True) + EPS)
        hb = (y * nw_ref[...]).astype(BF16)
        h_ref[...] = hb
        dt_ref[...] = lax.dot_general(hb, wdt_ref[...], _NT, preferred_element_type=F32)

    @pl.when(j < _N_TILE_MAIN)
    def _():
        o_ref[...] = lax.dot_general(h_ref[...], w_ref[...].astype(BF16), _NT,
                                     preferred_element_type=F32).astype(BF16)

    @pl.when(j == _N_TILE_MAIN)
    def _():
        o_ref[...] = lax.dot_general(h_ref[...], wtail_ref[0:tn, :], _NT,
                                     preferred_element_type=F32).astype(BF16)

    @pl.when(j == _N_TILE_MAIN + 1)
    def _():
        rest = TAIL_W - tn
        o_ref[:, 0:rest] = lax.dot_general(h_ref[...], wtail_ref[tn:TAIL_W, :], _NT,
                                           preferred_element_type=F32).astype(BF16)
        o_ref[:, rest:] = jnp.zeros((o_ref.shape[0], tn - rest), BF16)


def _inproj(x2, norm_w, wt, w_tail, w_dt, l):
    t = x2.shape[0]
    tm = _tile(t, 1024)
    tn = INPROJ_TN
    return pl.pallas_call(
        _inproj_kernel,
        grid=(t // tm, _N_TILE),
        in_specs=[
            pl.BlockSpec((tm, D_MODEL), lambda i, j: (i, 0)),
            pl.BlockSpec((1, D_MODEL), lambda i, j: (0, 0)),
            pl.BlockSpec((pl.Element(tn), pl.Element(D_MODEL)),
                         lambda i, j: (pl.multiple_of(l * IN_OFFS[-1] + _inproj_src_row(j), HALO), 0)),
            pl.BlockSpec((None, TAIL_W, D_MODEL), lambda i, j: (l, 0, 0)),
            pl.BlockSpec((None, DT_W, D_MODEL), lambda i, j: (l, 0, 0)),
        ],
        out_specs=[
            pl.BlockSpec((tm, tn), lambda i, j: (i, j)),
            pl.BlockSpec((tm, DT_W), lambda i, j: (i, 0)),
        ],
        out_shape=[jax.ShapeDtypeStruct((t, NP_OUT), BF16), jax.ShapeDtypeStruct((t, DT_W), F32)],
        scratch_shapes=[pltpu.VMEM((tm, D_MODEL), BF16)],
        compiler_params=_cparams(("parallel", "arbitrary")),
        name="inproj",
    )(x2, norm_w.reshape(1, D_MODEL), wt.reshape(-1, D_MODEL), w_tail, w_dt)


def _mla_prep_kernel(cq_ref, ckv_ref, kr_ref, krp_ref, cos_ref, sin_ref, qn_ref, kvn_ref,
                     wq_ref, wqp_ref, wk_ref, wv_ref, q_ref, k_ref, v_ref):
    cos = cos_ref[...]
    sin = sin_ref[...]
    cq = cq_ref[...].astype(F32)
    ncq = (cq * lax.rsqrt(jnp.mean(cq * cq, axis=-1, keepdims=True) + EPS) * qn_ref[...]).astype(BF16)
    qm = jnp.dot(ncq, wq_ref[...], preferred_element_type=F32)
    qp = jnp.dot(ncq, wqp_ref[...], preferred_element_type=F32)
    ckv = ckv_ref[...].astype(F32)
    nkv = (ckv * lax.rsqrt(jnp.mean(ckv * ckv, axis=-1, keepdims=True) + EPS) * kvn_ref[...]).astype(BF16)
    kn = jnp.dot(nkv, wk_ref[...], preferred_element_type=F32)
    v_ref[...] = jnp.dot(nkv, wv_ref[...], preferred_element_type=F32).astype(BF16)
    kpe = (kr_ref[...].astype(F32) * cos + krp_ref[...].astype(F32) * sin).astype(BF16)
    qscale = (MLA_NOPE + MLA_ROPE) ** -0.5 * LOG2E
    for h in range(MLA_HEADS):
        lo = h * MLA_QK
        q_ref[:, lo:lo + LANE] = (qm[:, lo:lo + LANE] * qscale).astype(BF16)
        qr = qm[:, lo + LANE:lo + 2 * LANE] * cos + qp[:, h * LANE:(h + 1) * LANE] * sin
        q_ref[:, lo + LANE:lo + 2 * LANE] = (qr * qscale).astype(BF16)
        k_ref[:, lo:lo + LANE] = kn[:, h * LANE:(h + 1) * LANE].astype(BF16)
        k_ref[:, lo + LANE:lo + 2 * LANE] = kpe


def _mla_prep(proj, s, cos_t, sin_t, q_norm, kv_norm, wq, wqp, wk, wv):
    t = proj.shape[0]
    tm = _tile(s, SEQ_TILE)
    ns = s // tm
    hq = MLA_HEADS * MLA_QK
    const = lambda shape: pl.BlockSpec(shape, lambda i: (0, 0))
    return pl.pallas_call(
        _mla_prep_kernel,
        grid=(t // tm,),
        in_specs=[
            pl.BlockSpec((tm, MLA_Q_LORA), lambda i: (i, OFF_CQ // MLA_Q_LORA)),
            pl.BlockSpec((tm, MLA_KV_LORA), lambda i: (i, OFF_CKV // MLA_KV_LORA)),
            pl.BlockSpec((tm, LANE), lambda i: (i, OFF_KR // LANE)),
            pl.BlockSpec((tm, LANE), lambda i: (i, OFF_KRP // LANE)),
            pl.BlockSpec((tm, LANE), lambda i: (i % ns, 0)),
            pl.BlockSpec((tm, LANE), lambda i: (i % ns, 0)),
            const((1, MLA_Q_LORA)),
            const((1, MLA_KV_LORA)),
            const((MLA_Q_LORA, hq)),
            const((MLA_Q_LORA, MLA_HEADS * LANE)),
            const((MLA_KV_LORA, MLA_HEADS * MLA_NOPE)),
            const((MLA_KV_LORA, MLA_HEADS * MLA_V)),
        ],
        out_specs=[
            pl.BlockSpec((tm, hq), lambda i: (i, 0)),
            pl.BlockSpec((tm, hq), lambda i: (i, 0)),
            pl.BlockSpec((tm, MLA_HEADS * MLA_V), lambda i: (i, 0)),
        ],
        out_shape=[jax.ShapeDtypeStruct((t, hq), BF16), jax.ShapeDtypeStruct((t, hq), BF16),
                   jax.ShapeDtypeStruct((t, MLA_HEADS * MLA_V), BF16)],
        compiler_params=_cparams(("parallel",)),
        name="mla_prep",
    )(proj, proj, proj, proj, cos_t, sin_t, q_norm.reshape(1, -1), kv_norm.reshape(1, -1), wq, wqp, wk, wv)


def _scores(q, k_ref):
    return lax.dot_general(q, k_ref[...], (((1,), (1,)), ((), ())), preferred_element_type=F32)


def _softmax_pv_from(sc, v_ref):
    m = jnp.max(sc, axis=-1, keepdims=True)
    p = jnp.exp2(sc - m)
    l = jnp.sum(p, axis=-1, keepdims=True)
    return jnp.dot(p.astype(BF16), v_ref[...], preferred_element_type=F32), l


def _staged(n_jobs, bufs, score_fn, consume_fn):
    ahead = len(bufs) - 1
    for n in range(min(ahead, n_jobs)):
        bufs[n % len(bufs)][...] = score_fn(n)
    for n in range(n_jobs):
        if n + ahead < n_jobs:
            bufs[(n + ahead) % len(bufs)][...] = score_fn(n + ahead)
        consume_fn(n, bufs[n % len(bufs)][...])


def _mla_attn_kernel(q_ref, k_ref, v_ref, g_ref, o_ref, *bufs, sub):
    def consume(n, sc):
        r = n * sub
        acc, l = _softmax_pv_from(sc, v_ref)
        o_ref[r:r + sub, :] = ((acc / l) * _silu(g_ref[r:r + sub, :].astype(F32))).astype(BF16)

    _staged(q_ref.shape[0] // sub, bufs, lambda n: _scores(q_ref[n * sub:(n + 1) * sub, :], k_ref), consume)


def _mla_attn(q, k, v, proj3):
    b, s, _ = q.shape
    tq = _tile(s, ATTN_TQ)
    return pl.pallas_call(
        functools.partial(_mla_attn_kernel, sub=min(tq, ATTN_SUB)),
        grid=(b, MLA_HEADS, s // tq),
        in_specs=[
            pl.BlockSpec((None, tq, MLA_QK), lambda bi, h, qi: (bi, qi, h)),
            pl.BlockSpec((None, s, MLA_QK), lambda bi, h, qi: (bi, 0, h)),
            pl.BlockSpec((None, s, MLA_V), lambda bi, h, qi: (bi, 0, h)),
            pl.BlockSpec((None, tq, MLA_V), lambda bi, h, qi: (bi, qi, OFF_GMLA // MLA_V + h)),
        ],
        out_specs=pl.BlockSpec((None, tq, MLA_V), lambda bi, h, qi: (bi, qi, h)),
        out_shape=jax.ShapeDtypeStruct((b, s, BRANCH_W), BF16),
        scratch_shapes=[pltpu.VMEM((min(tq, ATTN_SUB), s), F32)] * ATTN_BUFS,
        compiler_params=_cparams(("parallel", "parallel", "parallel")),
        name="mla_attn",
    )(q, k, v, proj3)


def _diff_rope(x, cos, sa, sb):
    half = DIFF_ROT // 2
    return x * cos + pltpu.roll(x, half, 1) * sa + pltpu.roll(x, DIFF_HD - half, 1) * sb


def _diff_attn_kernel(q1_ref, q2_ref, k1_ref, k2_ref, v_ref, g_ref, lam_ref, sub_ref,
                      cos_ref, sa_ref, sb_ref, o_ref, kr1_ref, kr2_ref, *bufs,
                      sub, lambda_init):
    lp = lam_ref[...]
    lam = (jnp.exp(jnp.sum(lp[0:1] * lp[1:2], axis=-1, keepdims=True))
           - jnp.exp(jnp.sum(lp[2:3] * lp[3:4], axis=-1, keepdims=True)) + lambda_init)
    for k_ref, kr_ref in ((k1_ref, kr1_ref), (k2_ref, kr2_ref)):
        kr_ref[...] = _diff_rope(k_ref[...].astype(F32), cos_ref[...], sa_ref[...], sb_ref[...]).astype(BF16)
    q0 = pl.program_id(2) * q1_ref.shape[0]
    qscale = DIFF_HD ** -0.5 * LOG2E
    qk = ((q1_ref, kr1_ref), (q2_ref, kr2_ref))
    first = {}

    def scores(n):
        q_ref, kr_ref = qk[n % 2]
        rows = slice((n // 2) * sub, (n // 2 + 1) * sub)
        pos = pl.ds(pl.multiple_of(q0 + rows.start, sub), sub)
        q = _diff_rope(q_ref[rows, :].astype(F32), cos_ref[pos, :], sa_ref[pos, :], sb_ref[pos, :])
        return _scores((q * qscale).astype(BF16), kr_ref)

    def consume(n, sc):
        acc, l = _softmax_pv_from(sc, v_ref)
        if n % 2 == 0:
            first[n // 2] = acc / l
            return
        r = (n // 2) * sub
        o = first.pop(n // 2) - lam * (acc / l)
        o = o * lax.rsqrt(jnp.mean(o * o, axis=-1, keepdims=True) + EPS) * sub_ref[...]
        o = o * (1.0 - lambda_init)
        o_ref[r:r + sub, :] = (o * _silu(g_ref[r:r + sub, :].astype(F32))).astype(BF16)

    _staged(2 * (q1_ref.shape[0] // sub), bufs, scores, consume)


def _diff_attn(proj3, tables, lam_params, subln, lambda_init):
    b, s, _ = proj3.shape
    tq = _tile(s, ATTN_TQ)
    dv = 2 * DIFF_HD
    hq = OFF_DQ // DIFF_HD
    hk = OFF_DK // DIFF_HD
    tab = pl.BlockSpec((s, LANE), lambda bi, h, qi: (0, 0))
    return pl.pallas_call(
        functools.partial(_diff_attn_kernel, sub=min(tq, ATTN_SUB), lambda_init=lambda_init),
        grid=(b, DIFF_HEADS, s // tq),
        in_specs=[
            pl.BlockSpec((None, tq, DIFF_HD), lambda bi, h, qi: (bi, qi, hq + 2 * h)),
            pl.BlockSpec((None, tq, DIFF_HD), lambda bi, h, qi: (bi, qi, hq + 2 * h + 1)),
            pl.BlockSpec((None, s, DIFF_HD), lambda bi, h, qi: (bi, 0, hk + 2 * h)),
            pl.BlockSpec((None, s, DIFF_HD), lambda bi, h, qi: (bi, 0, hk + 2 * h + 1)),
            pl.BlockSpec((None, s, dv), lambda bi, h, qi: (bi, 0, OFF_DV // dv + h)),
            pl.BlockSpec((None, tq, dv), lambda bi, h, qi: (bi, qi, OFF_GDIFF // dv + h)),
            pl.BlockSpec((4, DIFF_HD), lambda bi, h, qi: (0, 0)),
            pl.BlockSpec((1, dv), lambda bi, h, qi: (0, 0)),
            tab, tab, tab,
        ],
        out_specs=pl.BlockSpec((None, tq, dv), lambda bi, h, qi: (bi, qi, h)),
        out_shape=jax.ShapeDtypeStruct((b, s, BRANCH_W), BF16),
        scratch_shapes=[pltpu.VMEM((s, DIFF_HD), BF16)] * 2
        + [pltpu.VMEM((min(tq, ATTN_SUB), s), F32)] * ATTN_BUFS,
        compiler_params=_cparams(("parallel", "parallel", "parallel")),
        name="diff_attn",
    )(proj3, proj3, proj3, proj3, proj3, proj3, lam_params, subln.reshape(1, dv), *tables)


def _ssd_conv_kernel(xs_ref, xsp_ref, xsn_ref, bc_ref, bcp_ref, bcn_ref, w_ref, b_ref,
                     oxs_ref, obc_ref, ext_ref, *, tm):
    i = pl.program_id(1)
    has_prev = i > 0
    has_next = i < pl.num_programs(1) - 1
    pad_l = SSD_CONV // 2

    def conv(cur_ref, prev_ref, next_ref, out_ref, c0, width):
        ext_ref[0:HALO, 0:width] = jnp.where(has_prev, prev_ref[...].astype(F32), 0.0)
        ext_ref[HALO:HALO + tm, 0:width] = cur_ref[...].astype(F32)
        ext_ref[HALO + tm:, 0:width] = jnp.where(has_next, next_ref[...].astype(F32), 0.0)
        acc = b_ref[:, c0:c0 + width]
        for j in range(SSD_CONV):
            off = HALO - pad_l + j
            acc = acc + ext_ref[off:off + tm, 0:width] * w_ref[j:j + 1, c0:c0 + width]
        out_ref[...] = _silu(acc).astype(BF16)

    conv(xs_ref, xsp_ref, xsn_ref, oxs_ref, 0, BRANCH_W)
    conv(bc_ref, bcp_ref, bcn_ref, obc_ref, BRANCH_W, SSD_CONV_DIM - BRANCH_W)


def _halo_specs(tm, s, width, col):
    r = tm // HALO
    last = s // HALO - 1
    cur = pl.BlockSpec((None, tm, width), lambda bi, i: (bi, i, col))
    prev = pl.BlockSpec((None, HALO, width), lambda bi, i: (bi, jnp.maximum(i * r - 1, 0), col))
    nxt = pl.BlockSpec((None, HALO, width), lambda bi, i: (bi, jnp.minimum((i + 1) * r, last), col))
    return [cur, prev, nxt]


def _ssd_chunk(xs_b, bc, dt_raw, bias, alog, st_ref, reverse):
    ln = SSD_CHUNK
    gw = BRANCH_W // SSD_G
    xs = xs_b.astype(F32)
    dt = _softplus(dt_raw + bias)
    da = dt * (-jnp.exp(alog))
    ri = lax.broadcasted_iota(jnp.int32, (ln, ln), 0)
    ci = lax.broadcasted_iota(jnp.int32, (ln, ln), 1)
    tri = (ci >= ri) if reverse else (ri >= ci)
    cs = jnp.dot(tri.astype(F32), da, precision=HIGHEST, preferred_element_type=F32)
    cs_end = cs[0:1, :] if reverse else cs[ln - 1:ln, :]
    er = lax.broadcasted_iota(jnp.int32, (2 * LANE, BRANCH_W), 0)
    ec = lax.broadcasted_iota(jnp.int32, (2 * LANE, BRANCH_W), 1)
    expand = jnp.where(ec // SSD_P == er % LANE, 1.0, 0.0).astype(BF16)
    cols = jnp.concatenate([dt, jnp.exp(cs), jnp.exp(cs_end - cs)], axis=0)
    hi = cols.astype(BF16)
    lo = (cols - hi.astype(F32)).astype(BF16)
    wide = jnp.dot(jnp.concatenate([hi, lo], axis=1), expand, preferred_element_type=F32)
    dt_e = wide[0:ln]
    dec_out_e = wide[ln:2 * ln]
    dec_st_e = wide[2 * ln:3 * ln]
    tot_e = dec_out_e[0:1, :] if reverse else dec_out_e[ln - 1:ln, :]
    cs_t = cs.T
    xd = xs * dt_e
    xdb = xd.astype(BF16)
    xst = (xd * dec_st_e).astype(BF16)
    low = lax.broadcasted_iota(jnp.int32, (ln, LANE), 1) < SSD_P
    zero = jnp.zeros((ln, LANE), BF16)
    outs = []
    for g in range(SSD_G):
        bg = bc[:, g * SSD_N:(g + 1) * SSD_N]
        cg = bc[:, (SSD_G + g) * SSD_N:(SSD_G + g + 1) * SSD_N]
        cb = lax.dot_general(cg, bg, (((1,), (1,)), ((), ())), preferred_element_type=F32)
        st = st_ref[g]
        y_off = jnp.dot(cg, st.astype(BF16), preferred_element_type=F32) * dec_out_e[:, g * gw:(g + 1) * gw]
        bt = bg.astype(F32).T.astype(BF16)
        st_ref[g] = (st * tot_e[:, g * gw:(g + 1) * gw]
                     + jnp.dot(bt, xst[:, g * gw:(g + 1) * gw], preferred_element_type=F32))
        for kp in range(gw // LANE):
            mats = []
            for hh in range(2):
                hd = g * (SSD_HEADS // SSD_G) + 2 * kp + hh
                seg = cs[:, hd:hd + 1] - cs_t[hd:hd + 1, :]
                lm = jnp.exp(jnp.where(tri, seg, -jnp.inf))
                mats.append((cb * lm).astype(BF16))
            lhs = jnp.concatenate(mats, axis=1)
            col = g * gw + kp * LANE
            xp = xdb[:, col:col + LANE]
            rhs = jnp.concatenate([jnp.where(low, xp, zero), jnp.where(low, zero, xp)], axis=0)
            y_diag = jnp.dot(lhs, rhs, preferred_element_type=F32)
            outs.append(y_diag + y_off[:, kp * LANE:(kp + 1) * LANE])
    return jnp.concatenate(outs, axis=1), xs


def _chunk_rows(n_rows, reverse):
    starts = range(0, n_rows, SSD_CHUNK)
    return [slice(r, r + SSD_CHUNK) for r in (reversed(starts) if reverse else starts)]


def _ssd_fwd_kernel(xs_ref, xsp_ref, xsn_ref, bc_ref, bcp_ref, bcn_ref, w_ref, b_ref, dt_ref, bias_ref, alog_ref,
                    oxs_ref, obc_ref, y_ref, st_ref, ext_ref):
    @pl.when(pl.program_id(1) == 0)
    def _():
        st_ref[...] = jnp.zeros_like(st_ref)

    _ssd_conv_kernel(xs_ref, xsp_ref, xsn_ref, bc_ref, bcp_ref, bcn_ref, w_ref, b_ref, oxs_ref, obc_ref, ext_ref,
                     tm=xs_ref.shape[0])
    for rows in _chunk_rows(xs_ref.shape[0], reverse=False):
        y, _ = _ssd_chunk(oxs_ref[rows, :], obc_ref[rows, :], dt_ref[rows, :], bias_ref[...], alog_ref[...],
                          st_ref, reverse=False)
        y_ref[rows, :] = y


def _ssd_bwd_kernel(xs_ref, bc_ref, dt_ref, bias_ref, alog_ref, yf_ref, z_ref, d_ref, nw_ref,
                    o_ref, st_ref):
    @pl.when(pl.program_id(1) == 0)
    def _():
        st_ref[...] = jnp.zeros_like(st_ref)

    gw = BRANCH_W // SSD_G
    for rows in _chunk_rows(xs_ref.shape[0], reverse=True):
        yb, xs = _ssd_chunk(xs_ref[rows, :], bc_ref[rows, :], dt_ref[rows, :], bias_ref[...], alog_ref[...],
                            st_ref, reverse=True)
        y = yf_ref[rows, :] + yb + xs * d_ref[...]
        y = y * _silu(z_ref[rows, :].astype(F32))
        for g in range(SSD_G):
            yg = y[:, g * gw:(g + 1) * gw]
            yn = yg * lax.rsqrt(jnp.mean(yg * yg, axis=-1, keepdims=True) + EPS)
            o_ref[rows, g * gw:(g + 1) * gw] = (yn * nw_ref[:, g * gw:(g + 1) * gw]).astype(BF16)


def _ssd_scan(conv_w, conv_b, dt3, proj3, dt_bias, a_log, d_skip, norm_w):
    b, s, _ = proj3.shape
    tb = _tile(s, SSD_BLOCK)
    nc = s // tb
    wbc = SSD_CONV_DIM - BRANCH_W
    gw = BRANCH_W // SSD_G
    pad = lambda v: jnp.pad(v.astype(F32), (0, LANE - SSD_HEADS)).reshape(1, LANE)
    scratch = [pltpu.VMEM((SSD_G, SSD_N, gw), F32)]
    vec = lambda w: pl.BlockSpec((1, w), lambda bi, c: (0, 0))

    def specs(cmap, d):
        return [pl.BlockSpec((None, tb,BRANCH_W), lambda bi, c: (bi, cmap(c), 0)),
                pl.BlockSpec((None, tb,wbc), lambda bi, c: (bi, cmap(c), 0)),
                pl.BlockSpec((None, tb,LANE), lambda bi, c: (bi, cmap(c), d)),
                vec(LANE), vec(LANE)]

    blk = lambda w: pl.BlockSpec((None, tb, w), lambda bi, c: (bi, c, 0))
    xs_act, bc_act, y_f = pl.pallas_call(
        _ssd_fwd_kernel,
        grid=(b, nc),
        in_specs=(_halo_specs(tb, s, BRANCH_W, OFF_XS // BRANCH_W) + _halo_specs(tb, s, wbc, OFF_BC // wbc)
                  + [pl.BlockSpec((SSD_CONV, SSD_CONV_DIM), lambda bi, c: (0, 0)),
                     pl.BlockSpec((1, SSD_CONV_DIM), lambda bi, c: (0, 0)),
                     pl.BlockSpec((None, tb, LANE), lambda bi, c: (bi, c, 0)), vec(LANE), vec(LANE)]),
        out_specs=[blk(BRANCH_W), blk(wbc), blk(BRANCH_W)],
        out_shape=[jax.ShapeDtypeStruct((b, s, BRANCH_W), BF16), jax.ShapeDtypeStruct((b, s, wbc), BF16),
                   jax.ShapeDtypeStruct((b, s, BRANCH_W), F32)],
        scratch_shapes=scratch + [pltpu.VMEM((tb + 2 * HALO, BRANCH_W), F32)],
        compiler_params=_cparams(("parallel", "arbitrary")),
        name="ssd_fwd",
    )(proj3, proj3, proj3, proj3, proj3, proj3, conv_w, conv_b.reshape(1, -1), dt3, pad(dt_bias[0]), pad(a_log[0]))

    rev = lambda c: nc - 1 - c
    return pl.pallas_call(
        _ssd_bwd_kernel,
        grid=(b, nc),
        in_specs=specs(rev, 1) + [
            pl.BlockSpec((None, tb,BRANCH_W), lambda bi, c: (bi, rev(c), 0)),
            pl.BlockSpec((None, tb,BRANCH_W), lambda bi, c: (bi, rev(c), OFF_Z // BRANCH_W)),
            vec(BRANCH_W), vec(BRANCH_W)],
        out_specs=pl.BlockSpec((None, tb,BRANCH_W), lambda bi, c: (bi, rev(c), 0)),
        out_shape=jax.ShapeDtypeStruct((b, s, BRANCH_W), BF16),
        scratch_shapes=scratch,
        compiler_params=_cparams(("parallel", "arbitrary")),
        name="ssd_bwd",
    )(xs_act, bc_act, dt3, pad(dt_bias[1]), pad(a_log[1]), y_f, proj3,
      jnp.repeat(d_skip.astype(F32), SSD_P).reshape(1, BRANCH_W), norm_w.reshape(1, BRANCH_W))


def _pool_kernel(u_ref, up_ref, un_ref, g_ref, w_ref, sc_ref, o_ref, ext_ref, *, tm, s):
    i = pl.program_id(1)
    cur = u_ref[...].astype(F32)
    ext_ref[0:HALO, :] = jnp.where(i > 0, up_ref[...].astype(F32), 0.0)
    ext_ref[HALO:HALO + tm, :] = cur
    ext_ref[HALO + tm:, :] = jnp.where(i < pl.num_programs(1) - 1, un_ref[...].astype(F32), 0.0)
    pos = i * tm + lax.broadcasted_iota(jnp.int32, (tm, 1), 0)
    for gi, w in enumerate(POOL_WINDOWS):
        lo = w // 2
        hi = w - 1 - lo
        sl = slice(gi * POOL_GROUP, (gi + 1) * POOL_GROUP)
        acc = ext_ref[HALO - lo:HALO - lo + tm, sl]
        for d in range(-lo + 1, hi + 1):
            acc = acc + ext_ref[HALO + d:HALO + d + tm, sl]
        cnt = (jnp.minimum(pos + hi + 1, s) - jnp.maximum(pos - lo, 0)).astype(F32)
        pooled = acc / cnt - cur[:, sl]
        mixed = jnp.dot(pooled.astype(BF16), w_ref[gi], preferred_element_type=F32)
        o_ref[:, sl] = (mixed * sc_ref[:, sl] * _silu(g_ref[:, sl].astype(F32))).astype(BF16)


def _pool(proj3, pool_w, pool_scale):
    b, s, _ = proj3.shape
    tm = _tile(s, SEQ_TILE)
    return pl.pallas_call(
        functools.partial(_pool_kernel, tm=tm, s=s),
        grid=(b, s // tm),
        in_specs=_halo_specs(tm, s, BRANCH_W, OFF_U // BRANCH_W) + [
            pl.BlockSpec((None, tm, BRANCH_W), lambda bi, i: (bi, i, OFF_GPOOL // BRANCH_W)),
            pl.BlockSpec((len(POOL_WINDOWS), POOL_GROUP, POOL_GROUP), lambda bi, i: (0, 0, 0)),
            pl.BlockSpec((1, BRANCH_W), lambda bi, i: (0, 0))],
        out_specs=pl.BlockSpec((None, tm, BRANCH_W), lambda bi, i: (bi, i, 0)),
        out_shape=jax.ShapeDtypeStruct((b, s, BRANCH_W), BF16),
        scratch_shapes=[pltpu.VMEM((tm + 2 * HALO, BRANCH_W), F32)],
        compiler_params=_cparams(("parallel", "parallel")),
        name="pool",
    )(proj3, proj3, proj3, proj3, pool_w, pool_scale.reshape(1, BRANCH_W))


def _merge_kernel(*refs):
    brs, ws, gs, o_ref = refs[0:4], refs[4:8], refs[8:12], refs[12]
    acc = None
    for br, w, g in zip(brs, ws, gs):
        term = jax.nn.sigmoid(g[...].astype(F32)) * jnp.dot(br[...], w[...], preferred_element_type=F32)
        acc = term if acc is None else acc + term
    o_ref[...] = acc.astype(BF16)


def _merge(branches, w_branch, proj):
    t = proj.shape[0]
    tm = _tile(t, 1024)
    tn = 512
    br_spec = pl.BlockSpec((tm, BRANCH_W), lambda i, j: (i, 0))
    w_specs = [pl.BlockSpec((None, BRANCH_W, tn), lambda i, j, k=k: (k, 0, j)) for k in range(N_BRANCH)]
    g_specs = [pl.BlockSpec((tm, tn), lambda i, j, k=k: (i, (OFF_MG + k * D_MODEL) // tn + j))
               for k in range(N_BRANCH)]
    return pl.pallas_call(
        _merge_kernel,
        grid=(t // tm, D_MODEL // tn),
        in_specs=[br_spec] * N_BRANCH + w_specs + g_specs,
        out_specs=pl.BlockSpec((tm, tn), lambda i, j: (i, j)),
        out_shape=jax.ShapeDtypeStruct((t, D_MODEL), BF16),
        compiler_params=_cparams(("parallel", "parallel")),
        name="merge",
    )(*branches, *([w_branch] * N_BRANCH), *([proj] * N_BRANCH))


def _out_kernel(m_ref, w_ref, x_ref, fn_ref, o_ref, *, final):
    y = x_ref[...] + jnp.dot(m_ref[...], w_ref[...], preferred_element_type=F32)
    if final:
        y = y * lax.rsqrt(jnp.mean(y * y, axis=-1, keepdims=True) + EPS) * fn_ref[...]
    o_ref[...] = y


def _out_proj(merged, w_out, x2, final_norm, final):
    t = x2.shape[0]
    tm = _tile(t, 512)
    return pl.pallas_call(
        functools.partial(_out_kernel, final=final),
        grid=(t // tm,),
        in_specs=[pl.BlockSpec((tm, D_MODEL), lambda i: (i, 0)),
                  pl.BlockSpec((D_MODEL, D_MODEL), lambda i: (0, 0)),
                  pl.BlockSpec((tm, D_MODEL), lambda i: (i, 0)),
                  pl.BlockSpec((1, D_MODEL), lambda i: (0, 0))],
        out_specs=pl.BlockSpec((tm, D_MODEL), lambda i: (i, 0)),
        out_shape=jax.ShapeDtypeStruct((t, D_MODEL), F32),
        compiler_params=_cparams(("parallel",)),
        name="out_proj",
    )(merged, w_out, x2, final_norm.reshape(1, D_MODEL))


def _pack_w_in(w_in):
    wt = jnp.swapaxes(w_in, 1, 2)
    depth = wt.shape[0]
    zeros = lambda n: jnp.zeros((depth, n, D_MODEL), w_in.dtype)
    half = MLA_ROPE // 2
    kr0 = IN_OFFS[2]
    bc0 = IN_OFFS[9] + BRANCH_W
    w_tail = jnp.concatenate([
        wt[:, IN_OFFS[0]:IN_OFFS[1]],
        wt[:, bc0:IN_OFFS[10]],
        wt[:, IN_OFFS[1]:IN_OFFS[2]],
        wt[:, kr0:kr0 + MLA_ROPE], zeros(LANE - MLA_ROPE),
        wt[:, kr0 + half:kr0 + MLA_ROPE], wt[:, kr0:kr0 + half], zeros(LANE - MLA_ROPE),
    ], axis=1)
    dt0 = IN_OFFS[10]
    w_dt = jnp.concatenate([wt[:, dt0:dt0 + SSD_HEADS], zeros(LANE - SSD_HEADS),
                            wt[:, dt0 + SSD_HEADS:dt0 + 2 * SSD_HEADS], zeros(LANE - SSD_HEADS)], axis=1)
    w_tail, w_dt = lax.optimization_barrier((w_tail, w_dt))
    return wt, w_tail.astype(BF16), w_dt.astype(BF16)


def _pack_mla(w_uq, w_ukv):
    half = MLA_ROPE // 2
    wq3 = w_uq.reshape(MLA_Q_LORA, MLA_HEADS, MLA_NOPE + MLA_ROPE)
    zq = jnp.zeros((MLA_Q_LORA, MLA_HEADS, LANE - MLA_ROPE), w_uq.dtype)
    wq = jnp.concatenate([wq3, zq], axis=2).reshape(MLA_Q_LORA, MLA_HEADS * MLA_QK).astype(BF16)
    wqp = jnp.concatenate([wq3[:, :, MLA_NOPE + half:], wq3[:, :, MLA_NOPE:MLA_NOPE + half], zq],
                          axis=2).reshape(MLA_Q_LORA, MLA_HEADS * LANE).astype(BF16)
    wkv3 = w_ukv.reshape(MLA_KV_LORA, MLA_HEADS, MLA_NOPE + MLA_V)
    wk = wkv3[:, :, :MLA_NOPE].reshape(MLA_KV_LORA, MLA_HEADS * MLA_NOPE).astype(BF16)
    wv = wkv3[:, :, MLA_NOPE:].reshape(MLA_KV_LORA, MLA_HEADS * MLA_V).astype(BF16)
    return wq, wqp, wk, wv


def _rope_tables(s):
    def angles(rot):
        half = rot // 2
        inv_freq = jnp.power(ROPE_THETA, -jnp.arange(half, dtype=F32) * 2.0 / rot)
        ang = jnp.arange(s, dtype=F32)[:, None] * inv_freq[None, :]
        return jnp.cos(ang), jnp.sin(ang)

    c, sn = angles(MLA_ROPE)
    z = jnp.zeros((s, LANE - MLA_ROPE), F32)
    mla_cos = jnp.concatenate([c, c, z], axis=1)
    mla_sin = jnp.concatenate([-sn, sn, z], axis=1)
    c, sn = angles(DIFF_ROT)
    zh = jnp.zeros_like(sn)
    rest = LANE - DIFF_ROT
    d_cos = jnp.concatenate([c, c, jnp.ones((s, rest), F32)], axis=1)
    d_sa = jnp.concatenate([zh, sn, jnp.zeros((s, rest), F32)], axis=1)
    d_sb = jnp.concatenate([-sn, zh, jnp.zeros((s, rest), F32)], axis=1)
    return mla_cos, mla_sin, d_cos, d_sa, d_sb


def _trunk(x, w_inproj, packed, norm_w, mla_q_norm, mla_kv_norm, diff_lambda, diff_subln, ssd_conv_w, ssd_conv_b,
           ssd_dt_bias, ssd_a_log, ssd_d, ssd_norm, pool_w, pool_scale, w_branch, w_out, final_norm):
    b, s, _ = x.shape
    t = b * s
    mla_cos, mla_sin, d_cos, d_sa, d_sb = _rope_tables(s)
    x2 = x.reshape(t, D_MODEL)
    for l in range(DEPTH):
        wq, wqp, wk, wv = packed[l]
        lambda_init = 0.8 - 0.6 * math.exp(-0.3 * l)
        proj, dt = _inproj(x2, norm_w[l], *w_inproj, l)
        proj3 = proj.reshape(b, s, NP_OUT)
        q, k, v = _mla_prep(proj, s, mla_cos, mla_sin, mla_q_norm[l], mla_kv_norm[l], wq, wqp, wk, wv)
        br_mla = _mla_attn(q.reshape(b, s, -1), k.reshape(b, s, -1), v.reshape(b, s, -1), proj3)
        br_diff = _diff_attn(proj3, (d_cos, d_sa, d_sb), diff_lambda[l], diff_subln[l], lambda_init)
        br_ssd = _ssd_scan(ssd_conv_w[l], ssd_conv_b[l], dt.reshape(b, s, DT_W), proj3, ssd_dt_bias[l], ssd_a_log[l],
                           ssd_d[l], ssd_norm[l])
        br_pool = _pool(proj3, pool_w[l].astype(BF16), pool_scale[l])
        branches = [a.reshape(t, BRANCH_W) for a in (br_mla, br_diff, br_ssd, br_pool)]
        merged = _merge(branches, w_branch[l].astype(BF16), proj)
        x2 = _out_proj(merged, w_out[l].astype(BF16), x2, final_norm, final=(l == DEPTH - 1))
    return x2.reshape(b, s, D_MODEL)


def kernel(x_prompt, x_sample, norm_w, w_in, mla_q_norm, mla_w_uq, mla_kv_norm, mla_w_ukv, diff_lambda,
           diff_subln, ssd_conv_w, ssd_conv_b, ssd_dt_bias, ssd_a_log, ssd_d, ssd_norm, pool_w, pool_scale,
           w_branch, w_out, final_norm):
    w_inproj = _pack_w_in(w_in)
    packed = [_pack_mla(mla_w_uq[l], mla_w_ukv[l]) for l in range(DEPTH)]
    rest = (norm_w, mla_q_norm, mla_kv_norm, diff_lambda, diff_subln, ssd_conv_w, ssd_conv_b, ssd_dt_bias,
            ssd_a_log, ssd_d, ssd_norm, pool_w, pool_scale, w_branch, w_out, final_norm)
    return (_trunk(x_prompt, w_inproj, packed, *rest), _trunk(x_sample, w_inproj, packed, *rest))
```

```python
import functools
import math

import numpy as np
import jax
import jax.numpy as jnp
from jax import lax
from jax.experimental import pallas as pl
from jax.experimental.pallas import tpu as pltpu

F32 = jnp.float32
BF16 = jnp.bfloat16
HIGHEST = lax.Precision.HIGHEST

D_MODEL = 2048
DEPTH = 2
BRANCH_W = D_MODEL // 2
N_BRANCH = 4
ROPE_THETA = 500000.0
EPS = 1e-6
MLA_HEADS = 8
MLA_NOPE = 128
MLA_ROPE = 64
MLA_V = BRANCH_W // MLA_HEADS
MLA_Q_LORA = 512
MLA_KV_LORA = 256
DIFF_HEADS = 4
DIFF_HD = BRANCH_W // (2 * DIFF_HEADS)
DIFF_ROT = DIFF_HD // 4
SSD_P = 64
SSD_HEADS = BRANCH_W // SSD_P
SSD_N = 128
SSD_G = 2
SSD_CONV = 4
SSD_CHUNK = 128
SSD_BLOCK = 1024
SEQ_TILE = 1024
SSD_CONV_DIM = BRANCH_W + 2 * SSD_G * SSD_N
POOL_WINDOWS = (2, 4, 8, 16)
POOL_GROUP = BRANCH_W // 4
IN_SIZES = (MLA_Q_LORA, MLA_KV_LORA, MLA_ROPE, BRANCH_W, BRANCH_W, BRANCH_W, BRANCH_W, BRANCH_W,
            BRANCH_W, SSD_CONV_DIM, 2 * SSD_HEADS, BRANCH_W, BRANCH_W, N_BRANCH * D_MODEL)
IN_OFFS = tuple(int(v) for v in np.cumsum((0,) + IN_SIZES))

LANE = 128
HALO = 16
VMEM_LIMIT = 56 * 1024 * 1024

OFF_GMLA = 0
OFF_DQ = 1 * BRANCH_W
OFF_DK = 2 * BRANCH_W
OFF_DV = 3 * BRANCH_W
OFF_GDIFF = 4 * BRANCH_W
OFF_Z = 5 * BRANCH_W
OFF_XS = 6 * BRANCH_W
OFF_U = 7 * BRANCH_W
OFF_GPOOL = 8 * BRANCH_W
OFF_MG = 9 * BRANCH_W
OFF_CQ = OFF_MG + N_BRANCH * D_MODEL
OFF_BC = OFF_CQ + MLA_Q_LORA
OFF_CKV = OFF_BC + 2 * SSD_G * SSD_N
OFF_KR = OFF_CKV + MLA_KV_LORA
OFF_KRP = OFF_KR + LANE
NP_MAIN = OFF_KRP + LANE
DT_W = 2 * LANE

ATTN_TQ = 2048
ATTN_BUFS = 2
ATTN_SUB = 256
MLA_QK = 2 * LANE
LOG2E = math.log2(math.e)


def _cparams(sem, vmem=VMEM_LIMIT):
    return pltpu.CompilerParams(dimension_semantics=sem, vmem_limit_bytes=vmem)


def _tile(n, pref):
    t = min(n, pref)
    while n % t:
        t //= 2
    return t


def _silu(x):
    return x * jax.nn.sigmoid(x)


def _softplus(x):
    return jnp.maximum(x, 0.0) + jnp.log1p(jnp.exp(-jnp.abs(x)))


_NT = (((1,), (1,)), ((), ()))
INPROJ_TN = 1024
TAIL_W = NP_MAIN - OFF_CQ

_N_TILE_A = OFF_U // INPROJ_TN
_N_TILE_MAIN = OFF_CQ // INPROJ_TN
_N_TILE = _N_TILE_MAIN + 2
NP_OUT = _N_TILE * INPROJ_TN


def _inproj_src_row(j):
    return jnp.where(j < _N_TILE_A, IN_OFFS[3] + j * INPROJ_TN,
                     jnp.where(j < _N_TILE_MAIN, IN_OFFS[11] + (j - _N_TILE_A) * INPROJ_TN, IN_OFFS[0]))


def _inproj_kernel(x_ref, nw_ref, w_ref, wtail_ref, wdt_ref, o_ref, dt_ref, h_ref):
    j = pl.program_id(1)
    tn = INPROJ_TN

    @pl.when(j == 0)
    def _():
        x = x_ref[...]
        y = x * lax.rsqrt(jnp.mean(x * x, axis=-1, keepdims=True) + EPS)
        hb = (y * nw_ref[...]).astype(BF16)
        h_ref[...] = hb
        dt_ref[...] = lax.dot_general(hb, wdt_ref[...], _NT, preferred_element_type=F32)

    @pl.when(j < _N_TILE_MAIN)
    def _():
        o_ref[...] = lax.dot_general(h_ref[...], w_ref[...].astype(BF16), _NT,
                                     preferred_element_type=F32).astype(BF16)

    @pl.when(j == _N_TILE_MAIN)
    def _():
        o_ref[...] = lax.dot_general(h_ref[...], wtail_ref[0:tn, :], _NT,
                                     preferred_element_type=F32).astype(BF16)

    @pl.when(j == _N_TILE_MAIN + 1)
    def _():
        rest = TAIL_W - tn
        o_ref[:, 0:rest] = lax.dot_general(h_ref[...], wtail_ref[tn:TAIL_W, :], _NT,
                                           preferred_element_type=F32).astype(BF16)
        o_ref[:, rest:] = jnp.zeros((o_ref.shape[0], tn - rest), BF16)


def _inproj(x2, norm_w, wt, w_tail, w_dt, l):
    t = x2.shape[0]
    tm = _tile(t, 1024)
    tn = INPROJ_TN
    return pl.pallas_call(
        _inproj_kernel,
        grid=(t // tm, _N_TILE),
        in_specs=[
            pl.BlockSpec((tm, D_MODEL), lambda i, j: (i, 0)),
            pl.BlockSpec((1, D_MODEL), lambda i, j: (0, 0)),
            pl.BlockSpec((pl.Element(tn), pl.Element(D_MODEL)),
                         lambda i, j: (pl.multiple_of(l * IN_OFFS[-1] + _inproj_src_row(j), HALO), 0)),
            pl.BlockSpec((None, TAIL_W, D_MODEL), lambda i, j: (l, 0, 0)),
            pl.BlockSpec((None, DT_W, D_MODEL), lambda i, j: (l, 0, 0)),
        ],
        out_specs=[
            pl.BlockSpec((tm, tn), lambda i, j: (i, j)),
            pl.BlockSpec((tm, DT_W), lambda i, j: (i, 0)),
        ],
        out_shape=[jax.ShapeDtypeStruct((t, NP_OUT), BF16), jax.ShapeDtypeStruct((t, DT_W), F32)],
        scratch_shapes=[pltpu.VMEM((tm, D_MODEL), BF16)],
        compiler_params=_cparams(("parallel", "arbitrary")),
        name="inproj",
    )(x2, norm_w.reshape(1, D_MODEL), wt.reshape(-1, D_MODEL), w_tail, w_dt)


def _mla_prep_kernel(cq_ref, ckv_ref, kr_ref, krp_ref, cos_ref, sin_ref, qn_ref, kvn_ref,
                     wq_ref, wqp_ref, wk_ref, wv_ref, q_ref, k_ref, v_ref):
    cos = cos_ref[...]
    sin = sin_ref[...]
    cq = cq_ref[...].astype(F32)
    ncq = (cq * lax.rsqrt(jnp.mean(cq * cq, axis=-1, keepdims=True) + EPS) * qn_ref[...]).astype(BF16)
    qm = jnp.dot(ncq, wq_ref[...], preferred_element_type=F32)
    qp = jnp.dot(ncq, wqp_ref[...], preferred_element_type=F32)
    ckv = ckv_ref[...].astype(F32)
    nkv = (ckv * lax.rsqrt(jnp.mean(ckv * ckv, axis=-1, keepdims=True) + EPS) * kvn_ref[...]).astype(BF16)
    kn = jnp.dot(nkv, wk_ref[...], preferred_element_type=F32)
    v_ref[...] = jnp.dot(nkv, wv_ref[...], preferred_element_type=F32).astype(BF16)
    kpe = (kr_ref[...].astype(F32) * cos + krp_ref[...].astype(F32) * sin).astype(BF16)
    qscale = (MLA_NOPE + MLA_ROPE) ** -0.5 * LOG2E
    for h in range(MLA_HEADS):
        lo = h * MLA_QK
        q_ref[:, lo:lo + LANE] = (qm[:, lo:lo + LANE] * qscale).astype(BF16)
        qr = qm[:, lo + LANE:lo + 2 * LANE] * cos + qp[:, h * LANE:(h + 1) * LANE] * sin
        q_ref[:, lo + LANE:lo + 2 * LANE] = (qr * qscale).astype(BF16)
        k_ref[:, lo:lo + LANE] = kn[:, h * LANE:(h + 1) * LANE].astype(BF16)
        k_ref[:, lo + LANE:lo + 2 * LANE] = kpe


def _mla_prep(proj, s, cos_t, sin_t, q_norm, kv_norm, wq, wqp, wk, wv):
    t = proj.shape[0]
    tm = _tile(s, SEQ_TILE)
    ns = s // tm
    hq = MLA_HEADS * MLA_QK
    const = lambda shape: pl.BlockSpec(shape, lambda i: (0, 0))
    return pl.pallas_call(
        _mla_prep_kernel,
        grid=(t // tm,),
        in_specs=[
            pl.BlockSpec((tm, MLA_Q_LORA), lambda i: (i, OFF_CQ // MLA_Q_LORA)),
            pl.BlockSpec((tm, MLA_KV_LORA), lambda i: (i, OFF_CKV // MLA_KV_LORA)),
            pl.BlockSpec((tm, LANE), lambda i: (i, OFF_KR // LANE)),
            pl.BlockSpec((tm, LANE), lambda i: (i, OFF_KRP // LANE)),
            pl.BlockSpec((tm, LANE), lambda i: (i % ns, 0)),
            pl.BlockSpec((tm, LANE), lambda i: (i % ns, 0)),
            const((1, MLA_Q_LORA)),
            const((1, MLA_KV_LORA)),
            const((MLA_Q_LORA, hq)),
            const((MLA_Q_LORA, MLA_HEADS * LANE)),
            const((MLA_KV_LORA, MLA_HEADS * MLA_NOPE)),
            const((MLA_KV_LORA, MLA_HEADS * MLA_V)),
        ],
        out_specs=[
            pl.BlockSpec((tm, hq), lambda i: (i, 0)),
            pl.BlockSpec((tm, hq), lambda i: (i, 0)),
            pl.BlockSpec((tm, MLA_HEADS * MLA_V), lambda i: (i, 0)),
        ],
        out_shape=[jax.ShapeDtypeStruct((t, hq), BF16), jax.ShapeDtypeStruct((t, hq), BF16),
                   jax.ShapeDtypeStruct((t, MLA_HEADS * MLA_V), BF16)],
        compiler_params=_cparams(("parallel",)),
        name="mla_prep",
    )(proj, proj, proj, proj, cos_t, sin_t, q_norm.reshape(1, -1), kv_norm.reshape(1, -1), wq, wqp, wk, wv)


def _scores(q, k_ref):
    return lax.dot_general(q, k_ref[...], (((1,), (1,)), ((), ())), preferred_element_type=F32)


def _softmax_pv_from(sc, v_ref):
    m = jnp.max(sc, axis=-1, keepdims=True)
    p = jnp.exp2(sc - m)
    l = jnp.sum(p, axis=-1, keepdims=True)
    return jnp.dot(p.astype(BF16), v_ref[...], preferred_element_type=F32), l


def _staged(n_jobs, bufs, score_fn, consume_fn):
    ahead = len(bufs) - 1
    for n in range(min(ahead, n_jobs)):
        bufs[n % len(bufs)][...] = score_fn(n)
    for n in range(n_jobs):
        if n + ahead < n_jobs:
            bufs[(n + ahead) % len(bufs)][...] = score_fn(n + ahead)
        consume_fn(n, bufs[n % len(bufs)][...])


def _mla_attn_kernel(q_ref, k_ref, v_ref, g_ref, o_ref, *bufs, sub):
    def consume(n, sc):
        r = n * sub
        acc, l = _softmax_pv_from(sc, v_ref)
        o_ref[r:r + sub, :] = ((acc / l) * _silu(g_ref[r:r + sub, :].astype(F32))).astype(BF16)

    _staged(q_ref.shape[0] // sub, bufs, lambda n: _scores(q_ref[n * sub:(n + 1) * sub, :], k_ref), consume)


def _mla_attn(q, k, v, proj3):
    b, s, _ = q.shape
    tq = _tile(s, ATTN_TQ)
    return pl.pallas_call(
        functools.partial(_mla_attn_kernel, sub=min(tq, ATTN_SUB)),
        grid=(b, MLA_HEADS, s // tq),
        in_specs=[
            pl.BlockSpec((None, tq, MLA_QK), lambda bi, h, qi: (bi, qi, h)),
            pl.BlockSpec((None, s, MLA_QK), lambda bi, h, qi: (bi, 0, h)),
            pl.BlockSpec((None, s, MLA_V), lambda bi, h, qi: (bi, 0, h)),
            pl.BlockSpec((None, tq, MLA_V), lambda bi, h, qi: (bi, qi, OFF_GMLA // MLA_V + h)),
        ],
        out_specs=pl.BlockSpec((None, tq, MLA_V), lambda bi, h, qi: (bi, qi, h)),
        out_shape=jax.ShapeDtypeStruct((b, s, BRANCH_W), BF16),
        scratch_shapes=[pltpu.VMEM((min(tq, ATTN_SUB), s), F32)] * ATTN_BUFS,
        compiler_params=_cparams(("parallel", "parallel", "parallel")),
        name="mla_attn",
    )(q, k, v, proj3)


def _diff_rope(x, cos, sa, sb):
    half = DIFF_ROT // 2
    return x * cos + pltpu.roll(x, half, 1) * sa + pltpu.roll(x, DIFF_HD - half, 1) * sb


def _diff_attn_kernel(q1_ref, q2_ref, k1_ref, k2_ref, v_ref, g_ref, lam_ref, sub_ref,
                      cos_ref, sa_ref, sb_ref, o_ref, kr1_ref, kr2_ref, *bufs,
                      sub, lambda_init):
    lp = lam_ref[...]
    lam = (jnp.exp(jnp.sum(lp[0:1] * lp[1:2], axis=-1, keepdims=True))
           - jnp.exp(jnp.sum(lp[2:3] * lp[3:4], axis=-1, keepdims=True)) + lambda_init)
    for k_ref, kr_ref in ((k1_ref, kr1_ref), (k2_ref, kr2_ref)):
        kr_ref[...] = _diff_rope(k_ref[...].astype(F32), cos_ref[...], sa_ref[...], sb_ref[...]).astype(BF16)
    q0 = pl.program_id(2) * q1_ref.shape[0]
    qscale = DIFF_HD ** -0.5 * LOG2E
    qk = ((q1_ref, kr1_ref), (q2_ref, kr2_ref))
    first = {}

    def scores(n):
        q_ref, kr_ref = qk[n % 2]
        rows = slice((n // 2) * sub, (n // 2 + 1) * sub)
        pos = pl.ds(pl.multiple_of(q0 + rows.start, sub), sub)
        q = _diff_rope(q_ref[rows, :].astype(F32), cos_ref[pos, :], sa_ref[pos, :], sb_ref[pos, :])
        return _scores((q * qscale).astype(BF16), kr_ref)

    def consume(n, sc):
        acc, l = _softmax_pv_from(sc, v_ref)
        if n % 2 == 0:
            first[n // 2] = acc / l
            return
        r = (n // 2) * sub
        o = first.pop(n // 2) - lam * (acc / l)
        o = o * lax.rsqrt(jnp.mean(o * o, axis=-1, keepdims=True) + EPS) * sub_ref[...]
        o = o * (1.0 - lambda_init)
        o_ref[r:r + sub, :] = (o * _silu(g_ref[r:r + sub, :].astype(F32))).astype(BF16)

    _staged(2 * (q1_ref.shape[0] // sub), bufs, scores, consume)


def _diff_attn(proj3, tables, lam_params, subln, lambda_init):
    b, s, _ = proj3.shape
    tq = _tile(s, ATTN_TQ)
    dv = 2 * DIFF_HD
    hq = OFF_DQ // DIFF_HD
    hk = OFF_DK // DIFF_HD
    tab = pl.BlockSpec((s, LANE), lambda bi, h, qi: (0, 0))
    return pl.pallas_call(
        functools.partial(_diff_attn_kernel, sub=min(tq, ATTN_SUB), lambda_init=lambda_init),
        grid=(b, DIFF_HEADS, s // tq),
        in_specs=[
            pl.BlockSpec((None, tq, DIFF_HD), lambda bi, h, qi: (bi, qi, hq + 2 * h)),
            pl.BlockSpec((None, tq, DIFF_HD), lambda bi, h, qi: (bi, qi, hq + 2 * h + 1)),
            pl.BlockSpec((None, s, DIFF_HD), lambda bi, h, qi: (bi, 0, hk + 2 * h)),
            pl.BlockSpec((None, s, DIFF_HD), lambda bi, h, qi: (bi, 0, hk + 2 * h + 1)),
            pl.BlockSpec((None, s, dv), lambda bi, h, qi: (bi, 0, OFF_DV // dv + h)),
            pl.BlockSpec((None, tq, dv), lambda bi, h, qi: (bi, qi, OFF_GDIFF // dv + h)),
            pl.BlockSpec((4, DIFF_HD), lambda bi, h, qi: (0, 0)),
            pl.BlockSpec((1, dv), lambda bi, h, qi: (0, 0)),
            tab, tab, tab,
        ],
        out_specs=pl.BlockSpec((None, tq, dv), lambda bi, h, qi: (bi, qi, h)),
        out_shape=jax.ShapeDtypeStruct((b, s, BRANCH_W), BF16),
        scratch_shapes=[pltpu.VMEM((s, DIFF_HD), BF16)] * 2
        + [pltpu.VMEM((min(tq, ATTN_SUB), s), F32)] * ATTN_BUFS,
        compiler_params=_cparams(("parallel", "parallel", "parallel")),
        name="diff_attn",
    )(proj3, proj3, proj3, proj3, proj3, proj3, lam_params, subln.reshape(1, dv), *tables)


def _ssd_conv_kernel(xs_ref, xsp_ref, xsn_ref, bc_ref, bcp_ref, bcn_ref, w_ref, b_ref,
                     oxs_ref, obc_ref, ext_ref, *, tm):
    i = pl.program_id(1)
    has_prev = i > 0
    has_next = i < pl.num_programs(1) - 1
    pad_l = SSD_CONV // 2

    def conv(cur_ref, prev_ref, next_ref, out_ref, c0, width):
        ext_ref[0:HALO, 0:width] = jnp.where(has_prev, prev_ref[...].astype(F32), 0.0)
        ext_ref[HALO:HALO + tm, 0:width] = cur_ref[...].astype(F32)
        ext_ref[HALO + tm:, 0:width] = jnp.where(has_next, next_ref[...].astype(F32), 0.0)
        acc = b_ref[:, c0:c0 + width]
        for j in range(SSD_CONV):
            off = HALO - pad_l + j
            acc = acc + ext_ref[off:off + tm, 0:width] * w_ref[j:j + 1, c0:c0 + width]
        out_ref[...] = _silu(acc).astype(BF16)

    conv(xs_ref, xsp_ref, xsn_ref, oxs_ref, 0, BRANCH_W)
    conv(bc_ref, bcp_ref, bcn_ref, obc_ref, BRANCH_W, SSD_CONV_DIM - BRANCH_W)


def _halo_specs(tm, s, width, col):
    r = tm // HALO
    last = s // HALO - 1
    cur = pl.BlockSpec((None, tm, width), lambda bi, i: (bi, i, col))
    prev = pl.BlockSpec((None, HALO, width), lambda bi, i: (bi, jnp.maximum(i * r - 1, 0), col))
    nxt = pl.BlockSpec((None, HALO, width), lambda bi, i: (bi, jnp.minimum((i + 1) * r, last), col))
    return [cur, prev, nxt]


def _ssd_chunk(xs_b, bc, dt_raw, bias, alog, st_ref, reverse):
    ln = SSD_CHUNK
    gw = BRANCH_W // SSD_G
    xs = xs_b.astype(F32)
    dt = _softplus(dt_raw + bias)
    da = dt * (-jnp.exp(alog))
    ri = lax.broadcasted_iota(jnp.int32, (ln, ln), 0)
    ci = lax.broadcasted_iota(jnp.int32, (ln, ln), 1)
    tri = (ci >= ri) if reverse else (ri >= ci)
    cs = jnp.dot(tri.astype(F32), da, precision=HIGHEST, preferred_element_type=F32)
    cs_end = cs[0:1, :] if reverse else cs[ln - 1:ln, :]
    er = lax.broadcasted_iota(jnp.int32, (2 * LANE, BRANCH_W), 0)
    ec = lax.broadcasted_iota(jnp.int32, (2 * LANE, BRANCH_W), 1)
    expand = jnp.where(ec // SSD_P == er % LANE, 1.0, 0.0).astype(BF16)
    cols = jnp.concatenate([dt, jnp.exp(cs), jnp.exp(cs_end - cs)], axis=0)
    hi = cols.astype(BF16)
    lo = (cols - hi.astype(F32)).astype(BF16)
    wide = jnp.dot(jnp.concatenate([hi, lo], axis=1), expand, preferred_element_type=F32)
    dt_e = wide[0:ln]
    dec_out_e = wide[ln:2 * ln]
    dec_st_e = wide[2 * ln:3 * ln]
    tot_e = dec_out_e[0:1, :] if reverse else dec_out_e[ln - 1:ln, :]
    cs_t = cs.T
    xd = xs * dt_e
    xdb = xd.astype(BF16)
    xst = (xd * dec_st_e).astype(BF16)
    low = lax.broadcasted_iota(jnp.int32, (ln, LANE), 1) < SSD_P
    zero = jnp.zeros((ln, LANE), BF16)
    outs = []
    for g in range(SSD_G):
        bg = bc[:, g * SSD_N:(g + 1) * SSD_N]
        cg = bc[:, (SSD_G + g) * SSD_N:(SSD_G + g + 1) * SSD_N]
        cb = lax.dot_general(cg, bg, (((1,), (1,)), ((), ())), preferred_element_type=F32)
        st = st_ref[g]
        y_off = jnp.dot(cg, st.astype(BF16), preferred_element_type=F32) * dec_out_e[:, g * gw:(g + 1) * gw]
        bt = bg.astype(F32).T.astype(BF16)
        st_ref[g] = (st * tot_e[:, g * gw:(g + 1) * gw]
                     + jnp.dot(bt, xst[:, g * gw:(g + 1) * gw], preferred_element_type=F32))
        for kp in range(gw // LANE):
            mats = []
            for hh in range(2):
                hd = g * (SSD_HEADS // SSD_G) + 2 * kp + hh
                seg = cs[:, hd:hd + 1] - cs_t[hd:hd + 1, :]
                lm = jnp.exp(jnp.where(tri, seg, -jnp.inf))
                mats.append((cb * lm).astype(BF16))
            lhs = jnp.concatenate(mats, axis=1)
            col = g * gw + kp * LANE
            xp = xdb[:, col:col + LANE]
            rhs = jnp.concatenate([jnp.where(low, xp, zero), jnp.where(low, zero, xp)], axis=0)
            y_diag = jnp.dot(lhs, rhs, preferred_element_type=F32)
            outs.append(y_diag + y_off[:, kp * LANE:(kp + 1) * LANE])
    return jnp.concatenate(outs, axis=1), xs


def _chunk_rows(n_rows, reverse):
    starts = range(0, n_rows, SSD_CHUNK)
    return [slice(r, r + SSD_CHUNK) for r in (reversed(starts) if reverse else starts)]


def _ssd_fwd_kernel(xs_ref, xsp_ref, xsn_ref, bc_ref, bcp_ref, bcn_ref, w_ref, b_ref, dt_ref, bias_ref, alog_ref,
                    u_ref, up_ref, un_ref, g_ref, pw_ref, psc_ref,
                    oxs_ref, obc_ref, y_ref, opool_ref, st_ref, ext_ref, pext_ref, *, s):
    @pl.when(pl.program_id(1) == 0)
    def _():
        st_ref[...] = jnp.zeros_like(st_ref)

    _pool_kernel(u_ref, up_ref, un_ref, g_ref, pw_ref, psc_ref, opool_ref, pext_ref, tm=u_ref.shape[0], s=s)

    _ssd_conv_kernel(xs_ref, xsp_ref, xsn_ref, bc_ref, bcp_ref, bcn_ref, w_ref, b_ref, oxs_ref, obc_ref, ext_ref,
                     tm=xs_ref.shape[0])
    for rows in _chunk_rows(xs_ref.shape[0], reverse=False):
        y, _ = _ssd_chunk(oxs_ref[rows, :], obc_ref[rows, :], dt_ref[rows, :], bias_ref[...], alog_ref[...],
                          st_ref, reverse=False)
        y_ref[rows, :] = y


def _ssd_bwd_kernel(xs_ref, bc_ref, dt_ref, bias_ref, alog_ref, yf_ref, z_ref, d_ref, nw_ref,
                    o_ref, st_ref):
    @pl.when(pl.program_id(1) == 0)
    def _():
        st_ref[...] = jnp.zeros_like(st_ref)

    gw = BRANCH_W // SSD_G
    for rows in _chunk_rows(xs_ref.shape[0], reverse=True):
        yb, xs = _ssd_chunk(xs_ref[rows, :], bc_ref[rows, :], dt_ref[rows, :], bias_ref[...], alog_ref[...],
                            st_ref, reverse=True)
        y = yf_ref[rows, :] + yb + xs * d_ref[...]
        y = y * _silu(z_ref[rows, :].astype(F32))
        for g in range(SSD_G):
            yg = y[:, g * gw:(g + 1) * gw]
            yn = yg * lax.rsqrt(jnp.mean(yg * yg, axis=-1, keepdims=True) + EPS)
            o_ref[rows, g * gw:(g + 1) * gw] = (yn * nw_ref[:, g * gw:(g + 1) * gw]).astype(BF16)


def _ssd_scan(conv_w, conv_b, dt3, proj3, dt_bias, a_log, d_skip, norm_w, pool_w, pool_scale):
    b, s, _ = proj3.shape
    tb = _tile(s, SSD_BLOCK)
    nc = s // tb
    wbc = SSD_CONV_DIM - BRANCH_W
    gw = BRANCH_W // SSD_G
    pad = lambda v: jnp.pad(v.astype(F32), (0, LANE - SSD_HEADS)).reshape(1, LANE)
    scratch = [pltpu.VMEM((SSD_G, SSD_N, gw), F32)]
    vec = lambda w: pl.BlockSpec((1, w), lambda bi, c: (0, 0))

    def specs(cmap, d):
        return [pl.BlockSpec((None, tb,BRANCH_W), lambda bi, c: (bi, cmap(c), 0)),
                pl.BlockSpec((None, tb,wbc), lambda bi, c: (bi, cmap(c), 0)),
                pl.BlockSpec((None, tb,LANE), lambda bi, c: (bi, cmap(c), d)),
                vec(LANE), vec(LANE)]

    blk = lambda w: pl.BlockSpec((None, tb, w), lambda bi, c: (bi, c, 0))
    xs_act, bc_act, y_f, br_pool = pl.pallas_call(
        functools.partial(_ssd_fwd_kernel, s=s),
        grid=(b, nc),
        in_specs=(_halo_specs(tb, s, BRANCH_W, OFF_XS // BRANCH_W) + _halo_specs(tb, s, wbc, OFF_BC // wbc)
                  + [pl.BlockSpec((SSD_CONV, SSD_CONV_DIM), lambda bi, c: (0, 0)),
                     pl.BlockSpec((1, SSD_CONV_DIM), lambda bi, c: (0, 0)),
                     pl.BlockSpec((None, tb, LANE), lambda bi, c: (bi, c, 0)), vec(LANE), vec(LANE)]
                  + _halo_specs(tb, s, BRANCH_W, OFF_U // BRANCH_W)
                  + [pl.BlockSpec((None, tb, BRANCH_W), lambda bi, c: (bi, c, OFF_GPOOL // BRANCH_W)),
                     pl.BlockSpec((len(POOL_WINDOWS), POOL_GROUP, POOL_GROUP), lambda bi, c: (0, 0, 0)),
                     vec(BRANCH_W)]),
        out_specs=[blk(BRANCH_W), blk(wbc), blk(BRANCH_W), blk(BRANCH_W)],
        out_shape=[jax.ShapeDtypeStruct((b, s, BRANCH_W), BF16), jax.ShapeDtypeStruct((b, s, wbc), BF16),
                   jax.ShapeDtypeStruct((b, s, BRANCH_W), F32), jax.ShapeDtypeStruct((b, s, BRANCH_W), BF16)],
        scratch_shapes=scratch + [pltpu.VMEM((tb + 2 * HALO, BRANCH_W), F32)] * 2,
        compiler_params=_cparams(("parallel", "arbitrary")),
        name="ssd_fwd",
    )(proj3, proj3, proj3, proj3, proj3, proj3, conv_w, conv_b.reshape(1, -1), dt3, pad(dt_bias[0]), pad(a_log[0]),
      proj3, proj3, proj3, proj3, pool_w, pool_scale.reshape(1, BRANCH_W))

    rev = lambda c: nc - 1 - c
    br_ssd = pl.pallas_call(
        _ssd_bwd_kernel,
        grid=(b, nc),
        in_specs=specs(rev, 1) + [
            pl.BlockSpec((None, tb,BRANCH_W), lambda bi, c: (bi, rev(c), 0)),
            pl.BlockSpec((None, tb,BRANCH_W), lambda bi, c: (bi, rev(c), OFF_Z // BRANCH_W)),
            vec(BRANCH_W), vec(BRANCH_W)],
        out_specs=pl.BlockSpec((None, tb,BRANCH_W), lambda bi, c: (bi, rev(c), 0)),
        out_shape=jax.ShapeDtypeStruct((b, s, BRANCH_W), BF16),
        scratch_shapes=scratch,
        compiler_params=_cparams(("parallel", "arbitrary")),
        name="ssd_bwd",
    )(xs_act, bc_act, dt3, pad(dt_bias[1]), pad(a_log[1]), y_f, proj3,
      jnp.repeat(d_skip.astype(F32), SSD_P).reshape(1, BRANCH_W), norm_w.reshape(1, BRANCH_W))
    return br_ssd, br_pool


def _pool_kernel(u_ref, up_ref, un_ref, g_ref, w_ref, sc_ref, o_ref, ext_ref, *, tm, s):
    i = pl.program_id(1)
    cur = u_ref[...].astype(F32)
    ext_ref[0:HALO, :] = jnp.where(i > 0, up_ref[...].astype(F32), 0.0)
    ext_ref[HALO:HALO + tm, :] = cur
    ext_ref[HALO + tm:, :] = jnp.where(i < pl.num_programs(1) - 1, un_ref[...].astype(F32), 0.0)
    pos = i * tm + lax.broadcasted_iota(jnp.int32, (tm, 1), 0)
    for gi, w in enumerate(POOL_WINDOWS):
        lo = w // 2
        hi = w - 1 - lo
        sl = slice(gi * POOL_GROUP, (gi + 1) * POOL_GROUP)
        acc = ext_ref[HALO - lo:HALO - lo + tm, sl]
        for d in range(-lo + 1, hi + 1):
            acc = acc + ext_ref[HALO + d:HALO + d + tm, sl]
        cnt = (jnp.minimum(pos + hi + 1, s) - jnp.maximum(pos - lo, 0)).astype(F32)
        pooled = acc / cnt - cur[:, sl]
        mixed = jnp.dot(pooled.astype(BF16), w_ref[gi], preferred_element_type=F32)
        o_ref[:, sl] = (mixed * sc_ref[:, sl] * _silu(g_ref[:, sl].astype(F32))).astype(BF16)


def _merge_kernel(*refs):
    brs, ws, gs, o_ref = refs[0:4], refs[4:8], refs[8:12], refs[12]
    acc = None
    for br, w, g in zip(brs, ws, gs):
        term = jax.nn.sigmoid(g[...].astype(F32)) * jnp.dot(br[...], w[...], preferred_element_type=F32)
        acc = term if acc is None else acc + term
    o_ref[...] = acc.astype(BF16)


def _merge(branches, w_branch, proj):
    t = proj.shape[0]
    tm = _tile(t, 1024)
    tn = 512
    br_spec = pl.BlockSpec((tm, BRANCH_W), lambda i, j: (i, 0))
    w_specs = [pl.BlockSpec((None, BRANCH_W, tn), lambda i, j, k=k: (k, 0, j)) for k in range(N_BRANCH)]
    g_specs = [pl.BlockSpec((tm, tn), lambda i, j, k=k: (i, (OFF_MG + k * D_MODEL) // tn + j))
               for k in range(N_BRANCH)]
    return pl.pallas_call(
        _merge_kernel,
        grid=(t // tm, D_MODEL // tn),
        in_specs=[br_spec] * N_BRANCH + w_specs + g_specs,
        out_specs=pl.BlockSpec((tm, tn), lambda i, j: (i, j)),
        out_shape=jax.ShapeDtypeStruct((t, D_MODEL), BF16),
        compiler_params=_cparams(("parallel", "parallel")),
        name="merge",
    )(*branches, *([w_branch] * N_BRANCH), *([proj] * N_BRANCH))


def _out_kernel(m_ref, w_ref, x_ref, fn_ref, o_ref, *, final):
    y = x_ref[...] + jnp.dot(m_ref[...], w_ref[...], preferred_element_type=F32)
    if final:
        y = y * lax.rsqrt(jnp.mean(y * y, axis=-1, keepdims=True) + EPS) * fn_ref[...]
    o_ref[...] = y


def _out_proj(merged, w_out, x2, final_norm, final):
    t = x2.shape[0]
    tm = _tile(t, 512)
    return pl.pallas_call(
        functools.partial(_out_kernel, final=final),
        grid=(t // tm,),
        in_specs=[pl.BlockSpec((tm, D_MODEL), lambda i: (i, 0)),
                  pl.BlockSpec((D_MODEL, D_MODEL), lambda i: (0, 0)),
                  pl.BlockSpec((tm, D_MODEL), lambda i: (i, 0)),
                  pl.BlockSpec((1, D_MODEL), lambda i: (0, 0))],
        out_specs=pl.BlockSpec((tm, D_MODEL), lambda i: (i, 0)),
        out_shape=jax.ShapeDtypeStruct((t, D_MODEL), F32),
        compiler_params=_cparams(("parallel",)),
        name="out_proj",
    )(merged, w_out, x2, final_norm.reshape(1, D_MODEL))


def _pack_w_in(w_in):
    wt = jnp.swapaxes(w_in, 1, 2)
    depth = wt.shape[0]
    zeros = lambda n: jnp.zeros((depth, n, D_MODEL), w_in.dtype)
    half = MLA_ROPE // 2
    kr0 = IN_OFFS[2]
    bc0 = IN_OFFS[9] + BRANCH_W
    w_tail = jnp.concatenate([
        wt[:, IN_OFFS[0]:IN_OFFS[1]],
        wt[:, bc0:IN_OFFS[10]],
        wt[:, IN_OFFS[1]:IN_OFFS[2]],
        wt[:, kr0:kr0 + MLA_ROPE], zeros(LANE - MLA_ROPE),
        wt[:, kr0 + half:kr0 + MLA_ROPE], wt[:, kr0:kr0 + half], zeros(LANE - MLA_ROPE),
    ], axis=1)
    dt0 = IN_OFFS[10]
    w_dt = jnp.concatenate([wt[:, dt0:dt0 + SSD_HEADS], zeros(LANE - SSD_HEADS),
                            wt[:, dt0 + SSD_HEADS:dt0 + 2 * SSD_HEADS], zeros(LANE - SSD_HEADS)], axis=1)
    w_tail, w_dt = lax.optimization_barrier((w_tail, w_dt))
    return wt, w_tail.astype(BF16), w_dt.astype(BF16)


def _pack_mla(w_uq, w_ukv):
    half = MLA_ROPE // 2
    wq3 = w_uq.reshape(MLA_Q_LORA, MLA_HEADS, MLA_NOPE + MLA_ROPE)
    zq = jnp.zeros((MLA_Q_LORA, MLA_HEADS, LANE - MLA_ROPE), w_uq.dtype)
    wq = jnp.concatenate([wq3, zq], axis=2).reshape(MLA_Q_LORA, MLA_HEADS * MLA_QK).astype(BF16)
    wqp = jnp.concatenate([wq3[:, :, MLA_NOPE + half:], wq3[:, :, MLA_NOPE:MLA_NOPE + half], zq],
                          axis=2).reshape(MLA_Q_LORA, MLA_HEADS * LANE).astype(BF16)
    wkv3 = w_ukv.reshape(MLA_KV_LORA, MLA_HEADS, MLA_NOPE + MLA_V)
    wk = wkv3[:, :, :MLA_NOPE].reshape(MLA_KV_LORA, MLA_HEADS * MLA_NOPE).astype(BF16)
    wv = wkv3[:, :, MLA_NOPE:].reshape(MLA_KV_LORA, MLA_HEADS * MLA_V).astype(BF16)
    return wq, wqp, wk, wv


def _rope_tables(s):
    def angles(rot):
        half = rot // 2
        inv_freq = jnp.power(ROPE_THETA, -jnp.arange(half, dtype=F32) * 2.0 / rot)
        ang = jnp.arange(s, dtype=F32)[:, None] * inv_freq[None, :]
        return jnp.cos(ang), jnp.sin(ang)

    c, sn = angles(MLA_ROPE)
    z = jnp.zeros((s, LANE - MLA_ROPE), F32)
    mla_cos = jnp.concatenate([c, c, z], axis=1)
    mla_sin = jnp.concatenate([-sn, sn, z], axis=1)
    c, sn = angles(DIFF_ROT)
    zh = jnp.zeros_like(sn)
    rest = LANE - DIFF_ROT
    d_cos = jnp.concatenate([c, c, jnp.ones((s, rest), F32)], axis=1)
    d_sa = jnp.concatenate([zh, sn, jnp.zeros((s, rest), F32)], axis=1)
    d_sb = jnp.concatenate([-sn, zh, jnp.zeros((s, rest), F32)], axis=1)
    return mla_cos, mla_sin, d_cos, d_sa, d_sb


def _trunk(x, w_inproj, packed, norm_w, mla_q_norm, mla_kv_norm, diff_lambda, diff_subln, ssd_conv_w, ssd_conv_b,
           ssd_dt_bias, ssd_a_log, ssd_d, ssd_norm, pool_w, pool_scale, w_branch, w_out, final_norm):
    b, s, _ = x.shape
    t = b * s
    mla_cos, mla_sin, d_cos, d_sa, d_sb = _rope_tables(s)
    x2 = x.reshape(t, D_MODEL)
    for l in range(DEPTH):
        wq, wqp, wk, wv = packed[l]
        lambda_init = 0.8 - 0.6 * math.exp(-0.3 * l)
        proj, dt = _inproj(x2, norm_w[l], *w_inproj, l)
        proj3 = proj.reshape(b, s, NP_OUT)
        q, k, v = _mla_prep(proj, s, mla_cos, mla_sin, mla_q_norm[l], mla_kv_norm[l], wq, wqp, wk, wv)
        br_mla = _mla_attn(q.reshape(b, s, -1), k.reshape(b, s, -1), v.reshape(b, s, -1), proj3)
        br_diff = _diff_attn(proj3, (d_cos, d_sa, d_sb), diff_lambda[l], diff_subln[l], lambda_init)
        br_ssd, br_pool = _ssd_scan(ssd_conv_w[l], ssd_conv_b[l], dt.reshape(b, s, DT_W), proj3, ssd_dt_bias[l],
                                    ssd_a_log[l], ssd_d[l], ssd_norm[l], pool_w[l].astype(BF16), pool_scale[l])
        branches = [a.reshape(t, BRANCH_W) for a in (br_mla, br_diff, br_ssd, br_pool)]
        merged = _merge(branches, w_branch[l].astype(BF16), proj)
        x2 = _out_proj(merged, w_out[l].astype(BF16), x2, final_norm, final=(l == DEPTH - 1))
    return x2.reshape(b, s, D_MODEL)


def kernel(x_prompt, x_sample, norm_w, w_in, mla_q_norm, mla_w_uq, mla_kv_norm, mla_w_ukv, diff_lambda,
           diff_subln, ssd_conv_w, ssd_conv_b, ssd_dt_bias, ssd_a_log, ssd_d, ssd_norm, pool_w, pool_scale,
           w_branch, w_out, final_norm):
    w_inproj = _pack_w_in(w_in)
    packed = [_pack_mla(mla_w_uq[l], mla_w_ukv[l]) for l in range(DEPTH)]
    rest = (norm_w, mla_q_norm, mla_kv_norm, diff_lambda, diff_subln, ssd_conv_w, ssd_conv_b, ssd_dt_bias,
            ssd_a_log, ssd_d, ssd_norm, pool_w, pool_scale, w_branch, w_out, final_norm)
    return (_trunk(x_prompt, w_inproj, packed, *rest), _trunk(x_sample, w_inproj, packed, *rest))
```
